```python
import jax, jax.numpy as jnp
from jax import lax
import numpy as np

D_MODEL = 1024
BATCH = 4
SEQ = 8192
DEPTH = 1

NSA_HEADS = 8
NSA_KV_GROUPS = 2
HEAD_DIM = 64
NSA_WIDTH = NSA_HEADS * HEAD_DIM
KV_WIDTH = NSA_KV_GROUPS * HEAD_DIM
CMP_LEN = 32
CMP_STRIDE = 16
CMP_HIDDEN = 256
SLC_LEN = 64
SLC_TOPK = 16
WINDOW = 512
Q_BLOCK = 128
ROPE_THETA = 10000.0
RWKV_HEADS = 8
RWKV_HEAD = 64
RWKV_WIDTH = RWKV_HEADS * RWKV_HEAD
DECAY_LORA = 32
AAA_LORA = 32
GATE_LORA = 96
LNX_EPS = 64e-5
D_FF = ((-(-8 * D_MODEL // 3) + 255) // 256) * 256
NORM_EPS = 1e-6
BIG = 1e30

SPLIT_SIZES = (NSA_WIDTH, KV_WIDTH, KV_WIDTH, KV_WIDTH, KV_WIDTH, KV_WIDTH, KV_WIDTH, 3 * NSA_HEADS,
               RWKV_WIDTH, RWKV_WIDTH, RWKV_WIDTH, DECAY_LORA, AAA_LORA, GATE_LORA, 2 * D_MODEL)
IN_COLS = sum(SPLIT_SIZES)

kernel_name = 'hybrid_nsa_rwkv7_block'


def _split_points():
    pts, acc = [], 0
    for s in SPLIT_SIZES[:-1]:
        acc += s
        pts.append(acc)
    return pts


def rms_norm(x, g):
    xf = x.astype(jnp.float32)
    y = xf * lax.rsqrt(jnp.mean(xf * xf, axis=-1, keepdims=True) + NORM_EPS)
    return (y * g.astype(jnp.float32)).astype(x.dtype)


def rope_tables(n, dim):
    inv = 1.0 / (ROPE_THETA ** (jnp.arange(0, dim, 2, dtype=jnp.float32) / dim))
    ang = jnp.arange(n, dtype=jnp.float32)[:, None] * inv[None, :]
    return jnp.cos(ang), jnp.sin(ang)


def apply_rope(x, cos, sin):
    x1, x2 = jnp.split(x.astype(jnp.float32), 2, axis=-1)
    c, s = cos[None, :, None, :], sin[None, :, None, :]
    return jnp.concatenate([x1 * c - x2 * s, x1 * s + x2 * c], axis=-1).astype(x.dtype)


def token_shift(z, mu):
    prev = jnp.pad(z, ((0, 0), (1, 0), (0, 0)))[:, :-1]
    return z + (prev - z) * mu


def masked_softmax(s, mask):
    s = jnp.where(mask, s.astype(jnp.float32), -BIG)
    return jnp.where(mask, jax.nn.softmax(s, axis=-1), 0.0)


def compress(z, pe, w1, b1, w2):
    bsz, seq, groups, dim = z.shape
    n_cmp = (seq - CMP_LEN) // CMP_STRIDE + 1
    idx = jnp.arange(n_cmp)[:, None] * CMP_STRIDE + jnp.arange(CMP_LEN)[None, :]
    blk = z[:, idx] + pe[None, None, :, None, :]
    blk = blk.transpose(0, 1, 3, 2, 4).reshape(bsz, n_cmp, groups, CMP_LEN * dim)
    return jax.nn.gelu(blk @ w1 + b1) @ w2


def nsa_attention(q, k_cmp, v_cmp, k_slc, v_slc, k_win, v_win, gate_logits,
                  pe_k, w1_k, b1_k, w2_k, pe_v, w1_v, b1_v, w2_v):
    bsz, seq = q.shape[:2]
    G, R, d = NSA_KV_GROUPS, NSA_HEADS // NSA_KV_GROUPS, HEAD_DIM
    scale = d ** -0.5
    cos, sin = rope_tables(seq, d)

    kc = compress(k_cmp, pe_k, w1_k, b1_k, w2_k)
    vc = compress(v_cmp, pe_v, w1_v, b1_v, w2_v)
    n_cmp = kc.shape[1]
    cmp_start = jnp.arange(n_cmp) * CMP_STRIDE
    cmp_end = cmp_start + CMP_LEN - 1
    n_slc = seq // SLC_LEN
    n_sel = min(SLC_TOPK, n_slc)
    slc_start = jnp.arange(n_slc) * SLC_LEN
    overlap = ((cmp_start[:, None] < slc_start[None, :] + SLC_LEN)
               & (cmp_end[:, None] >= slc_start[None, :])).astype(jnp.float32)

    q_plain = q.reshape(bsz, seq, G, R, d)
    q_rot = apply_rope(q, cos, sin).reshape(bsz, seq, G, R, d)
    k_slc_blk = apply_rope(k_slc, cos, sin).reshape(bsz, n_slc, SLC_LEN, G, d).transpose(0, 3, 1, 2, 4)
    v_slc_blk = v_slc.reshape(bsz, n_slc, SLC_LEN, G, d).transpose(0, 3, 1, 2, 4)
    pad = ((0, 0), (WINDOW, 0), (0, 0), (0, 0))
    k_win_pad = jnp.pad(apply_rope(k_win, cos, sin), pad)
    v_win_pad = jnp.pad(v_win, pad)
    gate = jax.nn.sigmoid(gate_logits).reshape(bsz, seq, 3, G, R)

    b_ix = jnp.arange(bsz)[:, None, None, None]
    g_ix = jnp.arange(G)[None, :, None, None]
    blk_ids = jnp.arange(n_slc)
    in_blk = jnp.arange(SLC_LEN)
    win_off = jnp.arange(WINDOW + Q_BLOCK) - WINDOW

    def query_block(i):
        q0 = i * Q_BLOCK
        t = q0 + jnp.arange(Q_BLOCK)
        qp = lax.dynamic_slice_in_dim(q_plain, q0, Q_BLOCK, 1)
        qr = lax.dynamic_slice_in_dim(q_rot, q0, Q_BLOCK, 1)
        gb = lax.dynamic_slice_in_dim(gate, q0, Q_BLOCK, 1)

        s_c = jnp.einsum('bqgrd,bngd->bgrqn', qp, kc) * scale
        p_c = masked_softmax(s_c, cmp_end[None, :] <= t[:, None])
        o_c = jnp.einsum('bgrqn,bngd->bqgrd', p_c.astype(vc.dtype), vc)

        imp = jnp.einsum('bgrqn,nj->bgqj', p_c, overlap)
        cur = (t // SLC_LEN)[:, None]
        imp = jnp.where(blk_ids[None, :] > cur, -BIG, imp)
        imp = jnp.where((blk_ids[None, :] == 0) | (blk_ids[None, :] == cur), BIG, imp)
        _, sel = lax.top_k(imp, n_sel)

        k_sel = k_slc_blk[b_ix, g_ix, sel].reshape(bsz, G, Q_BLOCK, n_sel * SLC_LEN, d)
        v_sel = v_slc_blk[b_ix, g_ix, sel].reshape(bsz, G, Q_BLOCK, n_sel * SLC_LEN, d)
        pos = (sel[..., None] * SLC_LEN + in_blk).reshape(bsz, G, Q_BLOCK, n_sel * SLC_LEN)
        s_s = jnp.einsum('bqgrd,bgqkd->bgrqk', qr, k_sel) * scale
        p_s = masked_softmax(s_s, (pos <= t[:, None])[:, :, None])
        o_s = jnp.einsum('bgrqk,bgqkd->bqgrd', p_s.astype(v_sel.dtype), v_sel)

        k_w = lax.dynamic_slice_in_dim(k_win_pad, q0, WINDOW + Q_BLOCK, 1)
        v_w = lax.dynamic_slice_in_dim(v_win_pad, q0, WINDOW + Q_BLOCK, 1)
        kpos = q0 + win_off
        m_w = ((kpos[None, :] <= t[:, None]) & (kpos[None, :] > t[:, None] - WINDOW)
               & (kpos[None, :] >= 0))
        s_w = jnp.einsum('bqgrd,bkgd->bgrqk', qr, k_w) * scale
        p_w = masked_softmax(s_w, m_w)
        o_w = jnp.einsum('bgrqk,bkgd->bqgrd', p_w.astype(v_w.dtype), v_w)

        o = (gb[:, :, 0, :, :, None] * o_c + gb[:, :, 1, :, :, None] * o_s
             + gb[:, :, 2, :, :, None] * o_w)
        return o.reshape(bsz, Q_BLOCK, NSA_WIDTH)

    out = lax.map(query_block, jnp.arange(seq // Q_BLOCK))
    return out.transpose(1, 0, 2, 3).reshape(bsz, seq, NSA_WIDTH)


def rwkv7_time_mix(r, k, v, w_lo, a_lo, g_lo, mu_r, mu_k, mu_v, mu_w, mu_a, mu_g,
                   w0, w_w2, a0, w_a2, w_g2, k_k, k_a, r_k, lnx_w, lnx_b):
    out_dtype = r.dtype
    bsz, seq, _ = r.shape
    H, N = RWKV_HEADS, RWKV_HEAD
    f32 = jnp.float32
    r = token_shift(r, mu_r).astype(f32)
    k = token_shift(k, mu_k).astype(f32)
    v = token_shift(v, mu_v).astype(f32)
    w_lo = token_shift(w_lo, mu_w)
    a_lo = token_shift(a_lo, mu_a)
    g_lo = token_shift(g_lo, mu_g)

    w_log = -jax.nn.softplus(-(w0 + jnp.tanh(w_lo) @ w_w2).astype(f32)) - 0.5
    decay = jnp.exp(-jnp.exp(w_log))
    a = jax.nn.sigmoid((a0 + a_lo @ w_a2).astype(f32))
    g = jax.nn.sigmoid(g_lo) @ w_g2

    kk = (k * k_k).reshape(bsz, seq, H, N)
    kk = kk / jnp.maximum(jnp.sqrt(jnp.sum(kk * kk, axis=-1, keepdims=True)), 1e-12)
    k = k * (1.0 + (a - 1.0) * k_a)

    def heads(z):
        return z.reshape(bsz, seq, H, N).transpose(1, 0, 2, 3)

    xs = (heads(r), heads(decay), heads(k), heads(v), kk.transpose(1, 0, 2, 3), heads(a))

    def step(state, inp):
        r_t, w_t, k_t, v_t, kk_t, a_t = inp
        s_kk = jnp.einsum('bhij,bhj->bhi', state, kk_t)
        state = (state * w_t[:, :, None, :] - s_kk[..., None] * (kk_t * a_t)[:, :, None, :]
                 + v_t[..., None] * k_t[:, :, None, :])
        return state, jnp.einsum('bhij,bhj->bhi', state, r_t)

    _, y = lax.scan(step, jnp.zeros((bsz, H, N, N), f32), xs)
    y = y.transpose(1, 0, 2, 3)
    mean = jnp.mean(y, axis=-1, keepdims=True)
    var = jnp.mean(jnp.square(y - mean), axis=-1, keepdims=True)
    y = ((y - mean) * lax.rsqrt(var + LNX_EPS)).reshape(bsz, seq, H * N) * lnx_w + lnx_b
    rh, kh, vh = r.reshape(bsz, seq, H, N), k.reshape(bsz, seq, H, N), v.reshape(bsz, seq, H, N)
    bonus = (jnp.sum(rh * kh * r_k, axis=-1, keepdims=True) * vh).reshape(bsz, seq, H * N)
    return ((y + bonus) * g).astype(out_dtype)


def hybrid_layer(x, norm1_pre, norm1_post, w_in,
                 cmp_pe_k, cmp_w1_k, cmp_b1_k, cmp_w2_k, cmp_pe_v, cmp_w1_v, cmp_b1_v, cmp_w2_v,
                 mu_r, mu_k, mu_v, mu_w, mu_a, mu_g, w0, w_w2, a0, w_a2, w_g2,
                 k_k, k_a, r_k, lnx_w, lnx_b, w_branch_a, w_branch_b, w_out,
                 norm2_pre, norm2_post, w_gate, w_up, w_down):
    bsz, seq, _ = x.shape
    h = rms_norm(x, norm1_pre)
    z = h @ w_in
    (q, kc, vc, ks, vs, kw, vw, nsa_g, r, k, v, w_lo, a_lo, g_lo, merge_g) = jnp.split(
        z, _split_points(), axis=-1)

    def kvh(t):
        return t.reshape(bsz, seq, NSA_KV_GROUPS, HEAD_DIM)

    o_a = nsa_attention(q.reshape(bsz, seq, NSA_HEADS, HEAD_DIM), kvh(kc), kvh(vc), kvh(ks), kvh(vs),
                        kvh(kw), kvh(vw), nsa_g,
                        cmp_pe_k, cmp_w1_k, cmp_b1_k, cmp_w2_k, cmp_pe_v, cmp_w1_v, cmp_b1_v, cmp_w2_v)
    o_b = rwkv7_time_mix(r, k, v, w_lo, a_lo, g_lo, mu_r, mu_k, mu_v, mu_w, mu_a, mu_g,
                         w0, w_w2, a0, w_a2, w_g2, k_k, k_a, r_k, lnx_w, lnx_b)

    gate_a, gate_b = jnp.split(jax.nn.sigmoid(merge_g), 2, axis=-1)
    mixed = (gate_a * (o_a @ w_branch_a) + gate_b * (o_b @ w_branch_b)) @ w_out
    x = x + rms_norm(mixed, norm1_post)

    h2 = rms_norm(x, norm2_pre)
    f = (jax.nn.silu(h2 @ w_gate) * (h2 @ w_up)) @ w_down
    return x + rms_norm(f, norm2_post)


def setup_inputs(seed: int = 0) -> dict:
    key = jax.random.key(seed)
    ks = iter(jax.random.split(key, 40))
    f32 = jnp.float32

    def nrm(shape, scale):
        return scale * jax.random.normal(next(ks), shape, f32)

    def unif(shape, lo, hi):
        return jax.random.uniform(next(ks), shape, f32, lo, hi)

    L = DEPTH
    return {
        'x': jax.random.normal(next(ks), (BATCH, SEQ, D_MODEL), f32),
        'norm1_pre': 1.0 + nrm((L, D_MODEL), 0.05),
        'norm1_post': 1.0 + nrm((L, D_MODEL), 0.05),
        'w_in': nrm((L, D_MODEL, IN_COLS), D_MODEL ** -0.5),
        'cmp_pe_k': nrm((L, CMP_LEN, HEAD_DIM), 0.1),
        'cmp_w1_k': nrm((L, CMP_LEN * HEAD_DIM, CMP_HIDDEN), (CMP_LEN * HEAD_DIM) ** -0.5),
        'cmp_b1_k': nrm((L, CMP_HIDDEN), 0.01),
        'cmp_w2_k': nrm((L, CMP_HIDDEN, HEAD_DIM), 2.0 * CMP_HIDDEN ** -0.5),
        'cmp_pe_v': nrm((L, CMP_LEN, HEAD_DIM), 0.1),
        'cmp_w1_v': nrm((L, CMP_LEN * HEAD_DIM, CMP_HIDDEN), (CMP_LEN * HEAD_DIM) ** -0.5),
        'cmp_b1_v': nrm((L, CMP_HIDDEN), 0.01),
        'cmp_w2_v': nrm((L, CMP_HIDDEN, HEAD_DIM), 2.0 * CMP_HIDDEN ** -0.5),
        'mu_r': unif((L, RWKV_WIDTH), 0.0, 1.0),
        'mu_k': unif((L, RWKV_WIDTH), 0.0, 1.0),
        'mu_v': unif((L, RWKV_WIDTH), 0.0, 1.0),
        'mu_w': unif((L, DECAY_LORA), 0.0, 1.0),
        'mu_a': unif((L, AAA_LORA), 0.0, 1.0),
        'mu_g': unif((L, GATE_LORA), 0.0, 1.0),
        'w0': unif((L, RWKV_WIDTH), -3.0, 0.5),
        'w_w2': nrm((L, DECAY_LORA, RWKV_WIDTH), 0.1),
        'a0': nrm((L, RWKV_WIDTH), 0.1),
        'w_a2': nrm((L, AAA_LORA, RWKV_WIDTH), 0.1),
        'w_g2': nrm((L, GATE_LORA, RWKV_WIDTH), GATE_LORA ** -0.5),
        'k_k': 0.85 + nrm((L, RWKV_WIDTH), 0.02),
        'k_a': 1.0 + nrm((L, RWKV_WIDTH), 0.02),
        'r_k': nrm((L, RWKV_HEADS, RWKV_HEAD), 0.1),
        'lnx_w': 1.0 + nrm((L, RWKV_WIDTH), 0.05),
        'lnx_b': nrm((L, RWKV_WIDTH), 0.01),
        'w_branch_a': nrm((L, NSA_WIDTH, D_MODEL), NSA_WIDTH ** -0.5),
        'w_branch_b': nrm((L, RWKV_WIDTH, D_MODEL), RWKV_WIDTH ** -0.5),
        'w_out': nrm((L, D_MODEL, D_MODEL), D_MODEL ** -0.5),
        'norm2_pre': 1.0 + nrm((L, D_MODEL), 0.05),
        'norm2_post': 1.0 + nrm((L, D_MODEL), 0.05),
        'w_gate': nrm((L, D_MODEL, D_FF), D_MODEL ** -0.5),
        'w_up': nrm((L, D_MODEL, D_FF), D_MODEL ** -0.5),
        'w_down': nrm((L, D_FF, D_MODEL), D_FF ** -0.5),
    }


def reference(x, norm1_pre, norm1_post, w_in,
              cmp_pe_k, cmp_w1_k, cmp_b1_k, cmp_w2_k, cmp_pe_v, cmp_w1_v, cmp_b1_v, cmp_w2_v,
              mu_r, mu_k, mu_v, mu_w, mu_a, mu_g, w0, w_w2, a0, w_a2, w_g2,
              k_k, k_a, r_k, lnx_w, lnx_b, w_branch_a, w_branch_b, w_out,
              norm2_pre, norm2_post, w_gate, w_up, w_down):
    layer_params = (norm1_pre, norm1_post, w_in,
                    cmp_pe_k, cmp_w1_k, cmp_b1_k, cmp_w2_k, cmp_pe_v, cmp_w1_v, cmp_b1_v, cmp_w2_v,
                    mu_r, mu_k, mu_v, mu_w, mu_a, mu_g, w0, w_w2, a0, w_a2, w_g2,
                    k_k, k_a, r_k, lnx_w, lnx_b, w_branch_a, w_branch_b, w_out,
                    norm2_pre, norm2_post, w_gate, w_up, w_down)
    for layer in range(DEPTH):
        x = hybrid_layer(x, *[p[layer] for p in layer_params])
    return x
```

```python
import functools
import math

import jax
import jax.numpy as jnp
from jax import lax
from jax.experimental import pallas as pl
from jax.experimental.pallas import tpu as pltpu

F32 = jnp.float32
BF16 = jnp.bfloat16

D_MODEL = 1024
NSA_HEADS = 8
NSA_KV_GROUPS = 2
GROUP_HEADS = NSA_HEADS // NSA_KV_GROUPS
HEAD_DIM = 64
NSA_WIDTH = NSA_HEADS * HEAD_DIM
KV_WIDTH = NSA_KV_GROUPS * HEAD_DIM
CMP_LEN = 32
CMP_STRIDE = 16
CMP_HIDDEN = 256
SLC_LEN = 64
SLC_TOPK = 16
WINDOW = 512
Q_BLOCK = 128
ROPE_THETA = 10000.0
RWKV_HEADS = 8
RWKV_HEAD = 64
RWKV_WIDTH = RWKV_HEADS * RWKV_HEAD
DECAY_LORA = 32
AAA_LORA = 32
GATE_LORA = 96
LNX_EPS = 64e-5
D_FF = 2816
NORM_EPS = 1e-6
BIG = 1e30

LANES = 128
MAX_SLC_BLOCKS = 128
UNSELECTED_BIAS = -30000.0
REMOVED = -3.0e38

NSA_COLS = NSA_WIDTH + 6 * KV_WIDTH + LANES
RWKV_LORA = DECAY_LORA + AAA_LORA + GATE_LORA
RWKV_LORA_PAD = 256
RWKV_COLS = 3 * RWKV_WIDTH + RWKV_LORA_PAD
MERGE_COLS = 2 * D_MODEL
COL_KC, COL_VC, COL_KS, COL_VS, COL_KW, COL_VW = (NSA_WIDTH + i * KV_WIDTH for i in range(6))
COL_GATE = NSA_WIDTH + 6 * KV_WIDTH

RWKV_CHUNK = 64
RWKV_STEP = 512
VMEM_LIMIT = 56 * 1024 * 1024

_NT = (((1,), (1,)), ((), ()))
_TN = (((0,), (0,)), ((), ()))


def _params(*sem):
    return pltpu.CompilerParams(dimension_semantics=sem, vmem_limit_bytes=VMEM_LIMIT)


def _sigmoid(x):
    return 1.0 / (1.0 + jnp.exp(-x))


def _rms_norm(x, g):
    return x * lax.rsqrt(jnp.mean(x * x, axis=-1, keepdims=True) + NORM_EPS) * g


def _dot(a, b):
    return jnp.dot(a.astype(BF16), b.astype(BF16), preferred_element_type=F32)


def _dot_nt(a, b):
    return lax.dot_general(a.astype(BF16), b.astype(BF16), _NT, preferred_element_type=F32)


def _dot_tn(a, b):
    return lax.dot_general(a.astype(BF16), b.astype(BF16), _TN, preferred_element_type=F32)


def _col_chunks(width, step=512):
    return [(c, min(step, width - c)) for c in range(0, width, step)]


def _in_proj_kernel(x_ref, g_ref, w_ref, za_ref, zb_ref, zc_ref):
    h = _rms_norm(x_ref[...], g_ref[...]).astype(BF16)
    base = 0
    for o_ref in (za_ref, zb_ref, zc_ref):
        for c, n in _col_chunks(o_ref.shape[1]):
            o_ref[:, c:c + n] = jnp.dot(h, w_ref[:, base + c:base + c + n], preferred_element_type=F32)
        base += o_ref.shape[1]


def _in_proj(x2, g, w, tm=256):
    t = x2.shape[0]
    ncols = w.shape[1]
    return pl.pallas_call(
        _in_proj_kernel,
        grid=(t // tm,),
        in_specs=[pl.BlockSpec((tm, D_MODEL), lambda i: (i, 0)),
                  pl.BlockSpec((1, D_MODEL), lambda i: (0, 0)),
                  pl.BlockSpec((D_MODEL, ncols), lambda i: (0, 0))],
        out_specs=[pl.BlockSpec((tm, NSA_COLS), lambda i: (i, 0)),
                   pl.BlockSpec((tm, RWKV_COLS), lambda i: (i, 0)),
                   pl.BlockSpec((tm, MERGE_COLS), lambda i: (i, 0))],
        out_shape=[jax.ShapeDtypeStruct((t, NSA_COLS), F32),
                   jax.ShapeDtypeStruct((t, RWKV_COLS), F32),
                   jax.ShapeDtypeStruct((t, MERGE_COLS), F32)],
        compiler_params=_params("parallel"),
    )(x2, g, w)


def _rope(x, cos, sin_signed):
    w = x.shape[1]
    lane = lax.broadcasted_iota(jnp.int32, x.shape, 1)
    rot = jnp.where((lane & (HEAD_DIM - 1)) < HEAD_DIM // 2,
                    pltpu.roll(x, w - HEAD_DIM // 2, 1), pltpu.roll(x, HEAD_DIM // 2, 1))
    return x * cos + rot * sin_signed


def _nsa_prep_kernel(za_ref, cos_ref, sin_ref, qp_ref, qr_ref, kr_ref, gate_ref):
    scale = HEAD_DIM ** -0.5
    cos, sin = cos_ref[...], sin_ref[...]
    cos4 = jnp.concatenate([cos] * (NSA_WIDTH // LANES), axis=1)
    sin4 = jnp.concatenate([sin] * (NSA_WIDTH // LANES), axis=1)
    q = za_ref[:, 0:NSA_WIDTH]
    qp_ref[...] = (q * scale).astype(BF16)
    qr_ref[...] = (_rope(q, cos4, sin4) * scale).astype(BF16)
    kr_ref[:, 0:KV_WIDTH] = _rope(za_ref[:, COL_KS:COL_KS + KV_WIDTH], cos, sin).astype(BF16)
    kr_ref[:, KV_WIDTH:2 * KV_WIDTH] = _rope(za_ref[:, COL_KW:COL_KW + KV_WIDTH], cos, sin).astype(BF16)
    gate_ref[...] = _sigmoid(za_ref[:, COL_GATE:COL_GATE + LANES])


def _nsa_prep(za, cos_t, sin_t, seq, tm=512):
    t = za.shape[0]
    per_seq = seq // tm
    return pl.pallas_call(
        _nsa_prep_kernel,
        grid=(t // tm,),
        in_specs=[pl.BlockSpec((tm, NSA_COLS), lambda i: (i, 0)),
                  pl.BlockSpec((tm, LANES), lambda i: (i % per_seq, 0)),
                  pl.BlockSpec((tm, LANES), lambda i: (i % per_seq, 0))],
        out_specs=[pl.BlockSpec((tm, NSA_WIDTH), lambda i: (i, 0)),
                   pl.BlockSpec((tm, NSA_WIDTH), lambda i: (i, 0)),
                   pl.BlockSpec((tm, 2 * KV_WIDTH), lambda i: (i, 0)),
                   pl.BlockSpec((tm, LANES), lambda i: (i, 0))],
        out_shape=[jax.ShapeDtypeStruct((t, NSA_WIDTH), BF16),
                   jax.ShapeDtypeStruct((t, NSA_WIDTH), BF16),
                   jax.ShapeDtypeStruct((t, 2 * KV_WIDTH), BF16),
                   jax.ShapeDtypeStruct((t, LANES), F32)],
        compiler_params=_params("parallel"),
    )(za, cos_t, sin_t)


def _compress_kernel(c_ref, pe_ref, w1_ref, b1_ref, w2_ref, o_ref):
    half = CMP_STRIDE * HEAD_DIM
    c = c_ref[...]
    nrow = c.shape[0]
    y_lo = _dot(c + pe_ref[:, 0:half], w1_ref[0:half, :])
    y_hi = _dot(c + pe_ref[:, half:2 * half], w1_ref[half:2 * half, :])
    pre = y_lo + pltpu.roll(y_hi, nrow - 1, 0) + b1_ref[...]
    h = 0.5 * pre * (1.0 + jnp.tanh(math.sqrt(2.0 / math.pi) * (pre + 0.044715 * (pre * pre * pre))))
    o_ref[...] = _dot(h, w2_ref[...]).astype(o_ref.dtype)


def _compress(c, pe, w1, b1, w2):
    _, bsz, groups, nrow, width = c.shape
    return pl.pallas_call(
        _compress_kernel,
        grid=(2, bsz, groups),
        in_specs=[pl.BlockSpec((None, None, None, nrow, width), lambda s, b, g: (s, b, g, 0, 0)),
                  pl.BlockSpec((None, 1, width * 2), lambda s, b, g: (s, 0, 0)),
                  pl.BlockSpec((None, width * 2, CMP_HIDDEN), lambda s, b, g: (s, 0, 0)),
                  pl.BlockSpec((None, 1, CMP_HIDDEN), lambda s, b, g: (s, 0, 0)),
                  pl.BlockSpec((None, CMP_HIDDEN, HEAD_DIM), lambda s, b, g: (s, 0, 0))],
        out_specs=pl.BlockSpec((None, None, None, nrow, HEAD_DIM), lambda s, b, g: (s, b, g, 0, 0)),
        out_shape=jax.ShapeDtypeStruct((2, bsz, groups, nrow, HEAD_DIM), BF16),
        compiler_params=_params("parallel", "parallel", "parallel"),
    )(c, pe, w1, b1, w2)


def _masked_softmax(s, valid):
    s = jnp.where(valid, s, -BIG)
    e = jnp.exp(s - jnp.max(s, axis=-1, keepdims=True))
    return jnp.where(valid, e / jnp.sum(e, axis=-1, keepdims=True), 0.0)


def _nsa_attn_kernel(qp_ref, qr_ref, kc_ref, vc_ref, kaug_ref, vs_ref, kw_ref, vw_ref, gate_ref, ovt_ref,
                     o_ref, *, seq, key_tile, n_sel):
    rows = GROUP_HEADS * Q_BLOCK
    q0 = pl.program_id(2) * Q_BLOCK
    t_col = q0 + (lax.broadcasted_iota(jnp.int32, (rows, 1), 0) & (Q_BLOCK - 1))

    qp = qp_ref[...].reshape(rows, HEAD_DIM)
    n_cmp = kc_ref.shape[0]
    s_c = _dot_nt(qp, kc_ref[...])
    cmp_end = lax.broadcasted_iota(jnp.int32, (1, n_cmp), 1) * CMP_STRIDE + (CMP_LEN - 1)
    p_c = _masked_softmax(s_c, cmp_end <= t_col)
    o_c = _dot(p_c, vc_ref[...])

    p_sum = p_c[0:Q_BLOCK]
    for r in range(1, GROUP_HEADS):
        p_sum = p_sum + p_c[r * Q_BLOCK:(r + 1) * Q_BLOCK]
    p_hi = p_sum.astype(BF16)
    p_lo = (p_sum - p_hi.astype(F32)).astype(BF16)
    ovt = ovt_ref[...]
    imp = (lax.dot_general(ovt, p_hi, _NT, preferred_element_type=F32)
           + lax.dot_general(ovt, p_lo, _NT, preferred_element_type=F32))
    blk = lax.broadcasted_iota(jnp.int32, imp.shape, 0)
    cur = jnp.right_shift(q0 + lax.broadcasted_iota(jnp.int32, imp.shape, 1), int(math.log2(SLC_LEN)))
    imp = jnp.where(blk > cur, -BIG, imp)
    imp = jnp.where((blk == 0) | (blk == cur), BIG, imp)

    blk_f = blk.astype(F32)
    bias_t = jnp.full(imp.shape, UNSELECTED_BIAS, F32)
    for _ in range(n_sel):
        mx = jnp.max(imp, axis=0, keepdims=True)
        first = jnp.min(jnp.where(imp == mx, blk_f, float(MAX_SLC_BLOCKS)), axis=0, keepdims=True)
        hit = blk_f == first
        bias_t = jnp.where(hit, 0.0, bias_t)
        imp = jnp.where(hit, REMOVED, imp)
    bias = jnp.transpose(bias_t).astype(BF16)

    qr = qr_ref[...].reshape(rows, LANES)
    q_aug = jnp.concatenate([jnp.concatenate([bias] * GROUP_HEADS, axis=0), qr], axis=1)
    n_tiles = (q0 + Q_BLOCK + key_tile - 1) // key_tile
    k_iota = lax.broadcasted_iota(jnp.int32, (1, key_tile), 1)

    def body(kt, carry):
        m, l, acc = carry
        k0 = pl.multiple_of(kt * key_tile, key_tile)
        s = lax.dot_general(q_aug, kaug_ref[pl.ds(k0, key_tile), :], _NT, preferred_element_type=F32)
        s = jnp.where(k0 + k_iota <= t_col, s, -BIG)
        m_new = jnp.maximum(m, jnp.max(s, axis=-1, keepdims=True))
        alpha = jnp.exp(m - m_new)
        p = jnp.exp(s - m_new)
        l = alpha * l + jnp.sum(p, axis=-1, keepdims=True)
        acc = alpha * acc + jnp.dot(p.astype(BF16), vs_ref[pl.ds(k0, key_tile), :],
                                    preferred_element_type=F32)
        return m_new, l, acc

    _, l_s, acc_s = lax.fori_loop(
        0, n_tiles, body,
        (jnp.full((rows, 1), -BIG, F32), jnp.zeros((rows, 1), F32), jnp.zeros((rows, HEAD_DIM), F32)))
    o_s = acc_s / l_s

    span = WINDOW + Q_BLOCK
    kstart = pl.multiple_of(jnp.maximum(q0 - WINDOW, 0), Q_BLOCK)
    s_w = lax.dot_general(qr, kw_ref[pl.ds(kstart, span), :], _NT, preferred_element_type=F32)
    kpos = kstart + lax.broadcasted_iota(jnp.int32, (1, span), 1)
    p_w = _masked_softmax(s_w, (kpos <= t_col) & (kpos > t_col - WINDOW))
    o_w = jnp.dot(p_w.astype(BF16), vw_ref[pl.ds(kstart, span), :], preferred_element_type=F32)

    gate = gate_ref[...]
    for r in range(GROUP_HEADS):
        sl = slice(r * Q_BLOCK, (r + 1) * Q_BLOCK)
        o_ref[r] = (gate[:, r:r + 1] * o_c[sl]
                    + gate[:, GROUP_HEADS + r:GROUP_HEADS + r + 1] * o_s[sl]
                    + gate[:, 2 * GROUP_HEADS + r:2 * GROUP_HEADS + r + 1] * o_w[sl])


def _nsa_attn(qp, qr, kcmp, vcmp, kaug, vs, kw, vw, gate, ovt):
    bsz, _, seq, _ = qp.shape
    n_cmp = kcmp.shape[2]
    key_tile = min(512, seq)
    n_sel = min(SLC_TOPK, seq // SLC_LEN)
    kern = functools.partial(_nsa_attn_kernel, seq=seq, key_tile=key_tile, n_sel=n_sel)
    full = lambda width: pl.BlockSpec((None, None, seq, width), lambda b, g, i: (b, g, 0, 0))
    return pl.pallas_call(
        kern,
        grid=(bsz, NSA_KV_GROUPS, seq // Q_BLOCK),
        in_specs=[pl.BlockSpec((None, GROUP_HEADS, Q_BLOCK, HEAD_DIM), lambda b, g, i: (b, g, i, 0)),
                  pl.BlockSpec((None, GROUP_HEADS, Q_BLOCK, LANES), lambda b, g, i: (b, g, i, 0)),
                  pl.BlockSpec((None, None, n_cmp, HEAD_DIM), lambda b, g, i: (b, g, 0, 0)),
                  pl.BlockSpec((None, None, n_cmp, HEAD_DIM), lambda b, g, i: (b, g, 0, 0)),
                  full(2 * LANES), full(HEAD_DIM), full(LANES), full(HEAD_DIM),
                  pl.BlockSpec((None, None, Q_BLOCK, LANES), lambda b, g, i: (b, g, i, 0)),
                  pl.BlockSpec((MAX_SLC_BLOCKS, n_cmp), lambda b, g, i: (0, 0))],
        out_specs=pl.BlockSpec((None, GROUP_HEADS, Q_BLOCK, HEAD_DIM), lambda b, g, i: (b, g, i, 0)),
        out_shape=jax.ShapeDtypeStruct((bsz, NSA_HEADS, seq, HEAD_DIM), F32),
        compiler_params=_params("parallel", "parallel", "arbitrary"),
    )(qp, qr, kcmp, vcmp, kaug, vs, kw, vw, gate, ovt)


def _rwkv_prep_kernel(zb_ref, zprev_ref, mu_ref, w0_ref, a0_ref, kk_ref, ka_ref, ww2_ref, wa2_ref, wg2_ref,
                      r_ref, lw_ref, k_ref, v_ref, kkraw_ref, a_ref, g_ref, *, tiles_per_seq):
    z = zb_ref[...]
    first_of_seq = (pl.program_id(0) % tiles_per_seq) == 0
    carry_row = jnp.where(first_of_seq, 0.0, zprev_ref[7:8, :])
    row = lax.broadcasted_iota(jnp.int32, z.shape, 0)
    prev = jnp.where(row == 0, carry_row, pltpu.roll(z, 1, 0))
    zs = z + (prev - z) * mu_ref[...]
    w = RWKV_WIDTH
    r, k, v, lo = zs[:, 0:w], zs[:, w:2 * w], zs[:, 2 * w:3 * w], zs[:, 3 * w:]
    wl = w0_ref[...] + _dot(jnp.tanh(lo), ww2_ref[...])
    w_log = -(jnp.maximum(-wl, 0.0) + jnp.log1p(jnp.exp(-jnp.abs(wl)))) - 0.5
    a = _sigmoid(a0_ref[...] + _dot(lo, wa2_ref[...]))
    r_ref[...] = r
    lw_ref[...] = -jnp.exp(w_log)
    k_ref[...] = k * (1.0 + (a - 1.0) * ka_ref[...])
    v_ref[...] = v
    kkraw_ref[...] = k * kk_ref[...]
    a_ref[...] = a
    g_ref[...] = _dot(_sigmoid(lo), wg2_ref[...])


def _rwkv_prep(zb, mu, w0, a0, k_k, k_a, ww2, wa2, wg2, seq, tm=256):
    t = zb.shape[0]
    row = lambda n: pl.BlockSpec((1, n), lambda i: (0, 0))
    mat = pl.BlockSpec((RWKV_LORA_PAD, RWKV_WIDTH), lambda i: (0, 0))
    out = pl.BlockSpec((tm, RWKV_WIDTH), lambda i: (i, 0))
    kern = functools.partial(_rwkv_prep_kernel, tiles_per_seq=seq // tm)
    return pl.pallas_call(
        kern,
        grid=(t // tm,),
        in_specs=[pl.BlockSpec((tm, RWKV_COLS), lambda i: (i, 0)),
                  pl.BlockSpec((8, RWKV_COLS), lambda i: (jnp.maximum(i * (tm // 8) - 1, 0), 0)),
                  row(RWKV_COLS), row(RWKV_WIDTH), row(RWKV_WIDTH), row(RWKV_WIDTH), row(RWKV_WIDTH),
                  mat, mat, mat],
        out_specs=[out] * 7,
        out_shape=[jax.ShapeDtypeStruct((t, RWKV_WIDTH), F32)] * 7,
        compiler_params=_params("parallel"),
    )(zb, zb, mu, w0, a0, k_k, k_a, ww2, wa2, wg2)


def _rwkv_chunk_terms(r, lw, k, v, kkraw, a, tri_incl, incl, strict, eye):
    kk = kkraw / jnp.maximum(jnp.sqrt(jnp.sum(kkraw * kkraw, axis=-1, keepdims=True)), 1e-12)
    cum = jnp.dot(tri_incl, lw, precision=lax.Precision.HIGHEST, preferred_element_type=F32)
    tot = cum[RWKV_CHUNK - 1:RWKV_CHUNK, :]
    b = kk * a
    e_neg = jnp.exp(-cum)
    e_rem = jnp.exp(tot - cum)
    r_t = r * jnp.exp(cum)
    k_t = k * e_neg
    b_t = b * e_neg
    k_a = kk * jnp.exp(cum - lw)
    lhs = jnp.concatenate([r_t, k_a], axis=0)
    to_k = _dot_nt(lhs, k_t)
    to_b = _dot_nt(lhs, b_t)
    a_rk = jnp.where(incl, to_k[0:RWKV_CHUNK], 0.0)
    a_kk = jnp.where(strict, to_k[RWKV_CHUNK:], 0.0)
    a_rb = jnp.where(incl, to_b[0:RWKV_CHUNK], 0.0)
    a_kb = jnp.where(strict, to_b[RWKV_CHUNK:], 0.0)
    p = -a_kb
    t_inv = jnp.where(eye, 1.0, 0.0) + p
    for _ in range(int(math.log2(RWKV_CHUNK)) - 1):
        p = _dot(p, p)
        t_inv = t_inv + _dot(t_inv, p)
    w_k = _dot(t_inv, k_a)
    u_v = _dot(t_inv, _dot(a_kk, v))
    r_q = r_t - _dot(a_rb, w_k)
    y_v = _dot(a_rk, v) - _dot(a_rb, u_v)
    b_h = b * e_rem
    m = jnp.where(eye, jnp.exp(tot), 0.0) - _dot_tn(b_h, w_k)
    g = _dot_tn(k * e_rem, v) - _dot_tn(b_h, u_v)
    return r_q, y_v, m, g


def _rwkv_scan_kernel(r_ref, lw_ref, k_ref, v_ref, kkraw_ref, a_ref, rk_ref, lnw_ref, lnb_ref,
                      o_ref, h_ref):
    @pl.when(pl.program_id(2) == 0)
    def _():
        h_ref[...] = jnp.zeros_like(h_ref)

    c = RWKV_CHUNK
    ri = lax.broadcasted_iota(jnp.int32, (c, c), 0)
    ci = lax.broadcasted_iota(jnp.int32, (c, c), 1)
    incl, strict, eye = ri >= ci, ri > ci, ri == ci
    tri_incl = jnp.where(incl, 1.0, 0.0)
    n_chunks = r_ref.shape[0] // c
    terms = []
    for j in range(n_chunks):
        sl = slice(j * c, (j + 1) * c)
        terms.append(_rwkv_chunk_terms(r_ref[sl, :], lw_ref[sl, :], k_ref[sl, :], v_ref[sl, :],
                                       kkraw_ref[sl, :], a_ref[sl, :], tri_incl, incl, strict, eye))
    h = h_ref[...]
    for j in range(n_chunks):
        sl = slice(j * c, (j + 1) * c)
        r_q, y_v, m, g = terms[j]
        y = _dot(r_q, h) + y_v
        h = _dot(m, h) + g
        mean = jnp.mean(y, axis=-1, keepdims=True)
        var = jnp.mean(jnp.square(y - mean), axis=-1, keepdims=True)
        yn = (y - mean) * lax.rsqrt(var + LNX_EPS) * lnw_ref[...] + lnb_ref[...]
        bonus = jnp.sum(r_ref[sl, :] * k_ref[sl, :] * rk_ref[...], axis=-1, keepdims=True) * v_ref[sl, :]
        o_ref[sl, :] = yn + bonus
    h_ref[...] = h


def _rwkv_scan(r, lw, k, v, kkraw, a, r_k, lnx_w, lnx_b):
    bsz, heads, seq, n = r.shape
    step = min(RWKV_STEP, seq)
    tok = pl.BlockSpec((None, None, step, n), lambda b, h, i: (b, h, i, 0))
    par = pl.BlockSpec((None, 1, n), lambda b, h, i: (h, 0, 0))
    return pl.pallas_call(
        _rwkv_scan_kernel,
        grid=(bsz, heads, seq // step),
        in_specs=[tok] * 6 + [par] * 3,
        out_specs=tok,
        out_shape=jax.ShapeDtypeStruct((bsz, heads, seq, n), F32),
        scratch_shapes=[pltpu.VMEM((n, n), F32)],
        compiler_params=_params("parallel", "parallel", "arbitrary"),
    )(r, lw, k, v, kkraw, a, r_k, lnx_w, lnx_b)


def _merge_kernel(x_ref, oa_ref, yb_ref, g_ref, zc_ref, wa_ref, wb_ref, wo_ref, gn_ref, o_ref):
    gate_a = _sigmoid(zc_ref[:, 0:D_MODEL])
    gate_b = _sigmoid(zc_ref[:, D_MODEL:2 * D_MODEL])
    mixed = (gate_a * _dot(oa_ref[...], wa_ref[...])
             + gate_b * _dot(yb_ref[...] * g_ref[...], wb_ref[...]))
    o_ref[...] = x_ref[...] + _rms_norm(_dot(mixed, wo_ref[...]), gn_ref[...])


def _merge(x2, oa, yb, g, zc, wa, wb, wo, gn, tm=256):
    t = x2.shape[0]
    tile = lambda n: pl.BlockSpec((tm, n), lambda i: (i, 0))
    const = lambda a, b: pl.BlockSpec((a, b), lambda i: (0, 0))
    return pl.pallas_call(
        _merge_kernel,
        grid=(t // tm,),
        in_specs=[tile(D_MODEL), tile(NSA_WIDTH), tile(RWKV_WIDTH), tile(RWKV_WIDTH), tile(MERGE_COLS),
                  const(NSA_WIDTH, D_MODEL), const(RWKV_WIDTH, D_MODEL), const(D_MODEL, D_MODEL),
                  const(1, D_MODEL)],
        out_specs=tile(D_MODEL),
        out_shape=jax.ShapeDtypeStruct((t, D_MODEL), F32),
        compiler_params=_params("parallel"),
    )(x2, oa, yb, g, zc, wa, wb, wo, gn)


def _ffn_kernel(x_ref, gpre_ref, gpost_ref, wg_ref, wu_ref, wd_ref, o_ref, *, ff_chunk):
    x = x_ref[...]
    h = _rms_norm(x, gpre_ref[...]).astype(BF16)
    acc = jnp.zeros(x.shape, F32)
    for c in range(0, D_FF, ff_chunk):
        gt = jnp.dot(h, wg_ref[:, c:c + ff_chunk], preferred_element_type=F32)
        up = jnp.dot(h, wu_ref[:, c:c + ff_chunk], preferred_element_type=F32)
        acc = acc + _dot(gt * _sigmoid(gt) * up, wd_ref[c:c + ff_chunk, :])
    o_ref[...] = x + _rms_norm(acc, gpost_ref[...])


def _ffn(x2, gpre, gpost, wg, wu, wd, tm=256, ff_chunk=256):
    t = x2.shape[0]
    const = lambda a, b: pl.BlockSpec((a, b), lambda i: (0, 0))
    return pl.pallas_call(
        functools.partial(_ffn_kernel, ff_chunk=ff_chunk),
        grid=(t // tm,),
        in_specs=[pl.BlockSpec((tm, D_MODEL), lambda i: (i, 0)), const(1, D_MODEL), const(1, D_MODEL),
                  const(D_MODEL, D_FF), const(D_MODEL, D_FF), const(D_FF, D_MODEL)],
        out_specs=pl.BlockSpec((tm, D_MODEL), lambda i: (i, 0)),
        out_shape=jax.ShapeDtypeStruct((t, D_MODEL), F32),
        compiler_params=_params("parallel"),
    )(x2, gpre, gpost, wg, wu, wd)


def _pad_cols(a, n):
    return jnp.pad(a, ((0, 0), (0, n - a.shape[1])))


def _pack_w_in(w_in):
    g0 = NSA_WIDTH + 6 * KV_WIDTH
    r0 = g0 + 3 * NSA_HEADS
    l0 = r0 + 3 * RWKV_WIDTH
    m0 = l0 + RWKV_LORA
    return jnp.concatenate([w_in[:, :g0], _pad_cols(w_in[:, g0:r0], LANES),
                            w_in[:, r0:l0], _pad_cols(w_in[:, l0:m0], RWKV_LORA_PAD),
                            w_in[:, m0:]], axis=1).astype(BF16)


def _rope_tables(seq):
    inv = 1.0 / (ROPE_THETA ** (jnp.arange(0, HEAD_DIM, 2, dtype=F32) / HEAD_DIM))
    ang = jnp.arange(seq, dtype=F32)[:, None] * inv[None, :]
    cos, sin = jnp.cos(ang), jnp.sin(ang)
    reps = LANES // HEAD_DIM
    return (jnp.concatenate([cos, cos] * reps, axis=1), jnp.concatenate([-sin, sin] * reps, axis=1))


def _heads_major(a, heads):
    bsz, seq, _ = a.shape
    return a.reshape(bsz, seq, heads, -1).transpose(0, 2, 1, 3)


def _layer(x, norm1_pre, norm1_post, w_in,
           cmp_pe_k, cmp_w1_k, cmp_b1_k, cmp_w2_k, cmp_pe_v, cmp_w1_v, cmp_b1_v, cmp_w2_v,
           mu_r, mu_k, mu_v, mu_w, mu_a, mu_g, w0, w_w2, a0, w_a2, w_g2,
           k_k, k_a, r_k, lnx_w, lnx_b, w_branch_a, w_branch_b, w_out,
           norm2_pre, norm2_post, w_gate, w_up, w_down):
    bsz, seq, _ = x.shape
    assert seq % RWKV_STEP == 0 or seq < RWKV_STEP
    assert seq // SLC_LEN <= MAX_SLC_BLOCKS and seq >= WINDOW + Q_BLOCK
    t = bsz * seq
    x2 = x.reshape(t, D_MODEL)
    row = lambda a: a.reshape(1, -1)

    za, zb, zc = _in_proj(x2, row(norm1_pre), _pack_w_in(w_in))

    cos_t, sin_t = _rope_tables(seq)
    qp, qr, kr, gate = _nsa_prep(za, cos_t, sin_t, seq)
    g_, hd = NSA_KV_GROUPS, HEAD_DIM

    def kv_major(cols):
        return _heads_major(cols.reshape(bsz, seq, KV_WIDTH), g_)

    n_half = seq // CMP_STRIDE
    cmp_in = jnp.stack([kv_major(za[:, COL_KC:COL_KC + KV_WIDTH]),
                        kv_major(za[:, COL_VC:COL_VC + KV_WIDTH])]).reshape(
                            2, bsz, g_, n_half, CMP_STRIDE * hd)
    cmp_out = _compress(cmp_in,
                        jnp.stack([cmp_pe_k, cmp_pe_v]).reshape(2, 1, CMP_LEN * hd),
                        jnp.stack([cmp_w1_k, cmp_w1_v]).astype(BF16),
                        jnp.stack([cmp_b1_k, cmp_b1_v]).reshape(2, 1, CMP_HIDDEN),
                        jnp.stack([cmp_w2_k, cmp_w2_v]).astype(BF16))

    n_slc = seq // SLC_LEN
    onehot = (jnp.arange(seq)[:, None] // SLC_LEN == jnp.arange(MAX_SLC_BLOCKS)[None, :]).astype(BF16)
    zeros64 = jnp.zeros((bsz, g_, seq, LANES - hd), BF16)
    ks_rot = kv_major(kr[:, 0:KV_WIDTH])
    kaug = jnp.concatenate([jnp.broadcast_to(onehot, (bsz, g_, seq, MAX_SLC_BLOCKS)), ks_rot, zeros64], axis=-1)
    kwin = jnp.concatenate([kv_major(kr[:, KV_WIDTH:2 * KV_WIDTH]), zeros64], axis=-1)
    vsel = kv_major(za[:, COL_VS:COL_VS + KV_WIDTH]).astype(BF16)
    vwin = kv_major(za[:, COL_VW:COL_VW + KV_WIDTH]).astype(BF16)
    qp_h = _heads_major(qp.reshape(bsz, seq, NSA_WIDTH), NSA_HEADS)
    qr_h = jnp.pad(_heads_major(qr.reshape(bsz, seq, NSA_WIDTH), NSA_HEADS),
                   ((0, 0), (0, 0), (0, 0), (0, LANES - hd)))
    gate_g = gate[:, :3 * NSA_HEADS].reshape(bsz, seq, 3, g_, GROUP_HEADS).transpose(0, 3, 1, 2, 4)
    gate_g = jnp.pad(gate_g.reshape(bsz, g_, seq, 3 * GROUP_HEADS), ((0, 0), (0, 0), (0, 0), (0, LANES - 3 * GROUP_HEADS)))
    cmp_start = jnp.arange(n_half) * CMP_STRIDE
    slc_start = jnp.arange(MAX_SLC_BLOCKS) * SLC_LEN
    ovt = ((cmp_start[None, :] < slc_start[:, None] + SLC_LEN)
           & (cmp_start[None, :] + CMP_LEN - 1 >= slc_start[:, None])
           & (jnp.arange(MAX_SLC_BLOCKS)[:, None] < n_slc)).astype(BF16)
    o_a = _nsa_attn(qp_h, qr_h, cmp_out[0], cmp_out[1], kaug, vsel, kwin, vwin, gate_g, ovt)
    o_a = o_a.transpose(0, 2, 1, 3).reshape(t, NSA_WIDTH)

    mu = _pad_cols(jnp.concatenate([mu_r, mu_k, mu_v, mu_w, mu_a, mu_g]).reshape(1, -1), RWKV_COLS)
    lora = jnp.zeros((3, RWKV_LORA_PAD, RWKV_WIDTH), F32)
    lora = lora.at[0, 0:DECAY_LORA].set(w_w2)
    lora = lora.at[1, DECAY_LORA:DECAY_LORA + AAA_LORA].set(w_a2)
    lora = lora.at[2, DECAY_LORA + AAA_LORA:RWKV_LORA].set(w_g2).astype(BF16)
    rw = _rwkv_prep(zb, mu, row(w0), row(a0), row(k_k), row(k_a), lora[0], lora[1], lora[2], seq)
    r_, lw_, k_, v_, kkraw_, a_ = (_heads_major(a.reshape(bsz, seq, RWKV_WIDTH), RWKV_HEADS) for a in rw[:6])
    hp = lambda a: a.reshape(RWKV_HEADS, 1, RWKV_HEAD)
    y_b = _rwkv_scan(r_, lw_, k_, v_, kkraw_, a_, hp(r_k), hp(lnx_w), hp(lnx_b))
    y_b = y_b.transpose(0, 2, 1, 3).reshape(t, RWKV_WIDTH)

    x1 = _merge(x2, o_a, y_b, rw[6], zc, w_branch_a.astype(BF16), w_branch_b.astype(BF16),
                w_out.astype(BF16), row(norm1_post))
    out = _ffn(x1, row(norm2_pre), row(norm2_post), w_gate.astype(BF16), w_up.astype(BF16),
               w_down.astype(BF16))
    return out.reshape(bsz, seq, D_MODEL)


def kernel(x, norm1_pre, norm1_post, w_in, cmp_pe_k, cmp_w1_k, cmp_b1_k, cmp_w2_k, cmp_pe_v, cmp_w1_v, cmp_b1_v, cmp_w2_v, mu_r, mu_k, mu_v, mu_w, mu_a, mu_g, w0, w_w2, a0, w_a2, w_g2, k_k, k_a, r_k, lnx_w, lnx_b, w_branch_a, w_branch_b, w_out, norm2_pre, norm2_post, w_gate, w_up, w_down):
    params = (norm1_pre, norm1_post, w_in, cmp_pe_k, cmp_w1_k, cmp_b1_k, cmp_w2_k, cmp_pe_v, cmp_w1_v,
              cmp_b1_v, cmp_w2_v, mu_r, mu_k, mu_v, mu_w, mu_a, mu_g, w0, w_w2, a0, w_a2, w_g2,
              k_k, k_a, r_k, lnx_w, lnx_b, w_branch_a, w_branch_b, w_out,
              norm2_pre, norm2_post, w_gate, w_up, w_down)
    for layer in range(norm1_pre.shape[0]):
        x = _layer(x, *[p[layer] for p in params])
    return x
```

```python
import functools
import math

import jax
import jax.numpy as jnp
from jax import lax
from jax.experimental import pallas as pl
from jax.experimental.pallas import tpu as pltpu

F32 = jnp.float32
BF16 = jnp.bfloat16

D_MODEL = 1024
NSA_HEADS = 8
NSA_KV_GROUPS = 2
GROUP_HEADS = NSA_HEADS // NSA_KV_GROUPS
HEAD_DIM = 64
NSA_WIDTH = NSA_HEADS * HEAD_DIM
KV_WIDTH = NSA_KV_GROUPS * HEAD_DIM
CMP_LEN = 32
CMP_STRIDE = 16
CMP_HIDDEN = 256
SLC_LEN = 64
SLC_TOPK = 16
WINDOW = 512
Q_BLOCK = 128
ROPE_THETA = 10000.0
RWKV_HEADS = 8
RWKV_HEAD = 64
RWKV_WIDTH = RWKV_HEADS * RWKV_HEAD
DECAY_LORA = 32
AAA_LORA = 32
GATE_LORA = 96
LNX_EPS = 64e-5
D_FF = 2816
NORM_EPS = 1e-6
BIG = 1e30

LANES = 128
MAX_SLC_BLOCKS = 128
UNSELECTED_BIAS = -30000.0
REMOVED = -3.0e38

NSA_COLS = NSA_WIDTH + 6 * KV_WIDTH + NSA_KV_GROUPS * LANES
RWKV_LORA = DECAY_LORA + AAA_LORA + GATE_LORA
RWKV_LORA_PAD = 256
RWKV_COLS = 3 * RWKV_WIDTH + RWKV_LORA_PAD
MERGE_COLS = 2 * D_MODEL
COL_KC, COL_VC, COL_KS, COL_VS, COL_KW, COL_VW = (NSA_WIDTH + i * KV_WIDTH for i in range(6))
COL_GATE = NSA_WIDTH + 6 * KV_WIDTH

RWKV_CHUNK = 64
RWKV_STEP = 128
VMEM_LIMIT = 56 * 1024 * 1024

_NT = (((1,), (1,)), ((), ()))
_TN = (((0,), (0,)), ((), ()))


def _params(*sem):
    return pltpu.CompilerParams(dimension_semantics=sem, vmem_limit_bytes=VMEM_LIMIT)


def _sigmoid(x):
    return 1.0 / (1.0 + jnp.exp(-x))


def _rms_norm(x, g):
    return x * lax.rsqrt(jnp.mean(x * x, axis=-1, keepdims=True) + NORM_EPS) * g


def _dot(a, b):
    return jnp.dot(a.astype(BF16), b.astype(BF16), preferred_element_type=F32)


def _dot_nt(a, b):
    return lax.dot_general(a.astype(BF16), b.astype(BF16), _NT, preferred_element_type=F32)


def _dot_tn(a, b):
    return lax.dot_general(a.astype(BF16), b.astype(BF16), _TN, preferred_element_type=F32)


def _col_chunks(width, step=512):
    return [(c, min(step, width - c)) for c in range(0, width, step)]


def _in_proj_kernel(x_ref, g_ref, w_ref, za_ref, zb_ref, zc_ref):
    h = _rms_norm(x_ref[...], g_ref[...]).astype(BF16)
    base = 0
    for o_ref in (za_ref, zb_ref, zc_ref):
        for c, n in _col_chunks(o_ref.shape[1]):
            o_ref[:, c:c + n] = jnp.dot(h, w_ref[:, base + c:base + c + n], preferred_element_type=F32)
        base += o_ref.shape[1]


def _in_proj(x2, g, w, tm=256):
    t = x2.shape[0]
    ncols = w.shape[1]
    return pl.pallas_call(
        _in_proj_kernel,
        grid=(t // tm,),
        in_specs=[pl.BlockSpec((tm, D_MODEL), lambda i: (i, 0)),
                  pl.BlockSpec((1, D_MODEL), lambda i: (0, 0)),
                  pl.BlockSpec((D_MODEL, ncols), lambda i: (0, 0))],
        out_specs=[pl.BlockSpec((tm, NSA_COLS), lambda i: (i, 0)),
                   pl.BlockSpec((tm, RWKV_COLS), lambda i: (i, 0)),
                   pl.BlockSpec((tm, MERGE_COLS), lambda i: (i, 0))],
        out_shape=[jax.ShapeDtypeStruct((t, NSA_COLS), F32),
                   jax.ShapeDtypeStruct((t, RWKV_COLS), F32),
                   jax.ShapeDtypeStruct((t, MERGE_COLS), F32)],
        compiler_params=_params("parallel"),
    )(x2, g, w)


def _rope(x, cos, sin_signed):
    w = x.shape[1]
    lane = lax.broadcasted_iota(jnp.int32, x.shape, 1)
    rot = jnp.where((lane & (HEAD_DIM - 1)) < HEAD_DIM // 2,
                    pltpu.roll(x, w - HEAD_DIM // 2, 1), pltpu.roll(x, HEAD_DIM // 2, 1))
    return x * cos + rot * sin_signed


def _pad_heads(x):
    low = lax.broadcasted_iota(jnp.int32, x.shape, 1) < HEAD_DIM
    return jnp.where(low, x, 0.0), jnp.where(low, pltpu.roll(x, HEAD_DIM, 1), 0.0)


def _nsa_prep_kernel(za_ref, cos_ref, sin_ref, qp_ref, qr_ref, kaug_ref, kwin_ref, vsel_ref, vwin_ref):
    scale = HEAD_DIM ** -0.5
    cos, sin = cos_ref[...], sin_ref[...]
    tm = cos.shape[0]
    for pair in range(NSA_HEADS // 2):
        q = za_ref[:, pair * LANES:(pair + 1) * LANES]
        for o_ref, val in ((qp_ref, q * scale), (qr_ref, _rope(q, cos, sin) * scale)):
            even, odd = _pad_heads(val)
            o_ref[:, (2 * pair) * LANES:(2 * pair + 1) * LANES] = even.astype(BF16)
            o_ref[:, (2 * pair + 1) * LANES:(2 * pair + 2) * LANES] = odd.astype(BF16)
    pos = pl.program_id(1) * tm + lax.broadcasted_iota(jnp.int32, (tm, LANES), 0)
    blk = jnp.right_shift(pos, int(math.log2(SLC_LEN)))
    onehot = jnp.where(blk == lax.broadcasted_iota(jnp.int32, (tm, LANES), 1), 1.0, 0.0).astype(BF16)
    ks = _pad_heads(_rope(za_ref[:, COL_KS:COL_KS + LANES], cos, sin))
    kw = _pad_heads(_rope(za_ref[:, COL_KW:COL_KW + LANES], cos, sin))
    vs = _pad_heads(za_ref[:, COL_VS:COL_VS + LANES])
    vw = _pad_heads(za_ref[:, COL_VW:COL_VW + LANES])
    for g in range(NSA_KV_GROUPS):
        kaug_ref[g, :, 0:LANES] = onehot
        kaug_ref[g, :, LANES:2 * LANES] = ks[g].astype(BF16)
        kwin_ref[g] = kw[g].astype(BF16)
        vsel_ref[g] = vs[g].astype(BF16)
        vwin_ref[g] = vw[g].astype(BF16)


def _nsa_prep(za, cos_t, sin_t, bsz, seq, tm=512):
    per_seq = seq // tm
    t = bsz * seq
    qspec = pl.BlockSpec((tm, NSA_HEADS * LANES), lambda b, i: (b * per_seq + i, 0))
    kv = lambda w: pl.BlockSpec((None, NSA_KV_GROUPS, tm, w), lambda b, i: (b, 0, i, 0))
    kvs = lambda w: jax.ShapeDtypeStruct((bsz, NSA_KV_GROUPS, seq, w), BF16)
    return pl.pallas_call(
        _nsa_prep_kernel,
        grid=(bsz, per_seq),
        in_specs=[pl.BlockSpec((tm, NSA_COLS), lambda b, i: (b * per_seq + i, 0)),
                  pl.BlockSpec((tm, LANES), lambda b, i: (i, 0)),
                  pl.BlockSpec((tm, LANES), lambda b, i: (i, 0))],
        out_specs=[qspec, qspec, kv(2 * LANES), kv(LANES), kv(LANES), kv(LANES)],
        out_shape=[jax.ShapeDtypeStruct((t, NSA_HEADS * LANES), BF16),
                   jax.ShapeDtypeStruct((t, NSA_HEADS * LANES), BF16),
                   kvs(2 * LANES), kvs(LANES), kvs(LANES), kvs(LANES)],
        compiler_params=_params("parallel", "parallel"),
    )(za, cos_t, sin_t)


def _compress_kernel(z_ref, pe_ref, w1_ref, b1_ref, w2_ref, o_ref):
    n_half = z_ref.shape[0] // CMP_STRIDE
    y_lo = jnp.zeros((n_half, NSA_KV_GROUPS * CMP_HIDDEN), F32)
    y_hi = jnp.zeros((n_half, NSA_KV_GROUPS * CMP_HIDDEN), F32)
    for l in range(CMP_STRIDE):
        rows = z_ref[pl.ds(l, n_half, stride=CMP_STRIDE), :]
        y_lo = y_lo + _dot(rows + pe_ref[l:l + 1, :], w1_ref[l])
        y_hi = y_hi + _dot(rows + pe_ref[CMP_STRIDE + l:CMP_STRIDE + l + 1, :], w1_ref[CMP_STRIDE + l])
    pre = y_lo + pltpu.roll(y_hi, n_half - 1, 0) + b1_ref[...]
    h = 0.5 * pre * (1.0 + jnp.tanh(math.sqrt(2.0 / math.pi) * (pre + 0.044715 * (pre * pre * pre))))
    out = _dot(h, w2_ref[...]).astype(o_ref.dtype)
    for g in range(NSA_KV_GROUPS):
        o_ref[g] = out[:, g * LANES:(g + 1) * LANES]


def _compress(za, pe, w1, b1, w2, bsz, seq):
    n_half = seq // CMP_STRIDE
    hid = NSA_KV_GROUPS * CMP_HIDDEN
    return pl.pallas_call(
        _compress_kernel,
        grid=(2, bsz),
        in_specs=[pl.BlockSpec((seq, LANES), lambda s, b: (b, COL_KC // LANES + s)),
                  pl.BlockSpec((None, CMP_LEN, LANES), lambda s, b: (s, 0, 0)),
                  pl.BlockSpec((None, CMP_LEN, LANES, hid), lambda s, b: (s, 0, 0, 0)),
                  pl.BlockSpec((None, 1, hid), lambda s, b: (s, 0, 0)),
                  pl.BlockSpec((None, hid, NSA_KV_GROUPS * LANES), lambda s, b: (s, 0, 0))],
        out_specs=pl.BlockSpec((None, None, NSA_KV_GROUPS, n_half, LANES), lambda s, b: (s, b, 0, 0, 0)),
        out_shape=jax.ShapeDtypeStruct((2, bsz, NSA_KV_GROUPS, n_half, LANES), BF16),
        compiler_params=_params("parallel", "parallel"),
    )(za, pe, w1, b1, w2)


def _masked_softmax(s, valid):
    s = jnp.where(valid, s, -BIG)
    e = jnp.exp(s - jnp.max(s, axis=-1, keepdims=True))
    return jnp.where(valid, e * (1.0 / jnp.sum(e, axis=-1, keepdims=True)), 0.0)


def _stack_heads(q_ref):
    return jnp.concatenate([q_ref[:, r * LANES:(r + 1) * LANES] for r in range(GROUP_HEADS)], axis=0)


def _nsa_attn_kernel(qp_ref, qr_ref, kc_ref, vc_ref, kaug_ref, vs_ref, kw_ref, vw_ref, gate_ref, ovt_ref,
                     o_ref, *, seq, key_tile, n_sel):
    rows = GROUP_HEADS * Q_BLOCK
    q0 = pl.program_id(2) * Q_BLOCK
    t_col = q0 + (lax.broadcasted_iota(jnp.int32, (rows, 1), 0) & (Q_BLOCK - 1))

    n_cmp = kc_ref.shape[0]
    s_c = _dot_nt(_stack_heads(qp_ref), kc_ref[...])
    cmp_end = lax.broadcasted_iota(jnp.int32, (1, n_cmp), 1) * CMP_STRIDE + (CMP_LEN - 1)
    p_c = _masked_softmax(s_c, cmp_end <= t_col)
    o_c = _dot(p_c, vc_ref[...])

    p_sum = p_c[0:Q_BLOCK]
    for r in range(1, GROUP_HEADS):
        p_sum = p_sum + p_c[r * Q_BLOCK:(r + 1) * Q_BLOCK]
    p_hi = p_sum.astype(BF16)
    p_lo = (p_sum - p_hi.astype(F32)).astype(BF16)
    ovt = ovt_ref[...]
    imp = (lax.dot_general(ovt, p_hi, _NT, preferred_element_type=F32)
           + lax.dot_general(ovt, p_lo, _NT, preferred_element_type=F32))
    blk = lax.broadcasted_iota(jnp.int32, imp.shape, 0)
    cur = jnp.right_shift(q0 + lax.broadcasted_iota(jnp.int32, imp.shape, 1), int(math.log2(SLC_LEN)))
    imp = jnp.where(blk > cur, -BIG, imp)
    imp = jnp.where((blk == 0) | (blk == cur), BIG, imp)

    blk_f = blk.astype(F32)
    bias_t = jnp.full(imp.shape, UNSELECTED_BIAS, F32)
    for _ in range(n_sel):
        mx = jnp.max(imp, axis=0, keepdims=True)
        first = jnp.min(jnp.where(imp == mx, blk_f, float(MAX_SLC_BLOCKS)), axis=0, keepdims=True)
        hit = blk_f == first
        bias_t = jnp.where(hit, 0.0, bias_t)
        imp = jnp.where(hit, REMOVED, imp)
    bias = jnp.transpose(bias_t).astype(BF16)

    qr = _stack_heads(qr_ref)
    q_aug = jnp.concatenate([jnp.concatenate([bias] * GROUP_HEADS, axis=0), qr], axis=1)
    k_iota = lax.broadcasted_iota(jnp.int32, (1, key_tile), 1)

    def tile_update(kt, carry, causal):
        m, l, acc = carry
        k0 = pl.multiple_of(kt * key_tile, key_tile)
        s = lax.dot_general(q_aug, kaug_ref[pl.ds(k0, key_tile), :], _NT, preferred_element_type=F32)
        if causal:
            s = jnp.where(k0 + k_iota <= t_col, s, -BIG)
        m_new = jnp.maximum(m, jnp.max(s, axis=-1, keepdims=True))
        alpha = jnp.exp(m - m_new)
        p = jnp.exp(s - m_new)
        l = alpha * l + jnp.sum(p, axis=-1, keepdims=True)
        acc = alpha * acc + jnp.dot(p.astype(BF16), vs_ref[pl.ds(k0, key_tile), :],
                                    preferred_element_type=F32)
        return m_new, l, acc

    n_full = q0 // key_tile
    carry = lax.fori_loop(
        0, n_full, functools.partial(tile_update, causal=False),
        (jnp.full((rows, 1), -BIG, F32), jnp.zeros((rows, 1), F32), jnp.zeros((rows, LANES), F32)))
    _, l_s, acc_s = tile_update(n_full, carry, causal=True)
    o_s = acc_s * (1.0 / l_s)

    span = WINDOW + Q_BLOCK
    kstart = pl.multiple_of(jnp.maximum(q0 - WINDOW, 0), Q_BLOCK)
    s_w = lax.dot_general(qr, kw_ref[pl.ds(kstart, span), :], _NT, preferred_element_type=F32)
    kpos = kstart + lax.broadcasted_iota(jnp.int32, (1, span), 1)
    p_w = _masked_softmax(s_w, (kpos <= t_col) & (kpos > t_col - WINDOW))
    o_w = jnp.dot(p_w.astype(BF16), vw_ref[pl.ds(kstart, span), :], preferred_element_type=F32)

    gate = _sigmoid(gate_ref[...])
    heads = []
    for r in range(GROUP_HEADS):
        sl = slice(r * Q_BLOCK, (r + 1) * Q_BLOCK)
        heads.append(gate[:, r:r + 1] * o_c[sl]
                     + gate[:, GROUP_HEADS + r:GROUP_HEADS + r + 1] * o_s[sl]
                     + gate[:, 2 * GROUP_HEADS + r:2 * GROUP_HEADS + r + 1] * o_w[sl])
    for pair in range(GROUP_HEADS // 2):
        o_ref[:, pair * LANES:(pair + 1) * LANES] = heads[2 * pair] + pltpu.roll(heads[2 * pair + 1], HEAD_DIM, 1)


def _nsa_attn(qp, qr, kcmp, vcmp, kaug, vs, kw, vw, za, ovt, bsz, seq):
    n_cmp = kcmp.shape[2]
    n_q = seq // Q_BLOCK
    key_tile = min(512, seq)
    n_sel = min(SLC_TOPK, seq // SLC_LEN)
    kern = functools.partial(_nsa_attn_kernel, seq=seq, key_tile=key_tile, n_sel=n_sel)
    full = lambda rows, width: pl.BlockSpec((None, None, rows, width), lambda b, g, i: (b, g, 0, 0))
    qspec = pl.BlockSpec((Q_BLOCK, GROUP_HEADS * LANES), lambda b, g, i: (b * n_q + i, g))
    return pl.pallas_call(
        kern,
        grid=(bsz, NSA_KV_GROUPS, n_q),
        in_specs=[qspec, qspec, full(n_cmp, LANES), full(n_cmp, LANES),
                  full(seq, 2 * LANES), full(seq, LANES), full(seq, LANES), full(seq, LANES),
                  pl.BlockSpec((Q_BLOCK, LANES), lambda b, g, i: (b * n_q + i, COL_GATE // LANES + g)),
                  pl.BlockSpec((MAX_SLC_BLOCKS, n_cmp), lambda b, g, i: (0, 0))],
        out_specs=pl.BlockSpec((Q_BLOCK, GROUP_HEADS * HEAD_DIM), lambda b, g, i: (b * n_q + i, g)),
        out_shape=jax.ShapeDtypeStruct((bsz * seq, NSA_WIDTH), F32),
        compiler_params=_params("parallel", "parallel", "arbitrary"),
    )(qp, qr, kcmp, vcmp, kaug, vs, kw, vw, za, ovt)


def _rwkv_token_terms(z, prev_row, mu, w0, a0, k_k, k_a, ww2, wa2, wg2):
    row = lax.broadcasted_iota(jnp.int32, z.shape, 0)
    prev = jnp.where(row == 0, prev_row, pltpu.roll(z, 1, 0))
    zs = z + (prev - z) * mu
    w = RWKV_WIDTH
    r, k, v, lo = zs[:, 0:w], zs[:, w:2 * w], zs[:, 2 * w:3 * w], zs[:, 3 * w:]
    wl = w0 + _dot(jnp.tanh(lo), ww2)
    w_log = -(jnp.maximum(-wl, 0.0) + jnp.log1p(jnp.exp(-jnp.abs(wl)))) - 0.5
    a = _sigmoid(a0 + _dot(lo, wa2))
    lw = -jnp.exp(w_log)
    g = _dot(_sigmoid(lo), wg2)
    return r, lw, k * (1.0 + (a - 1.0) * k_a), v, k * k_k, a, g


def _split_heads(x, heads):
    n = x.shape[1] // heads
    return jnp.stack([x[:, h * n:(h + 1) * n] for h in range(heads)])


def _merge_heads(x):
    return jnp.concatenate([x[h] for h in range(x.shape[0])], axis=-1)


def _bdot(a, b, dims):
    return lax.dot_general(a.astype(BF16), b.astype(BF16), dims, preferred_element_type=F32)


_B_NN = (((2,), (1,)), ((0,), (0,)))
_B_NT = (((2,), (2,)), ((0,), (0,)))
_B_TN = (((1,), (1,)), ((0,), (0,)))


def _rwkv_chunk_terms(r, lw, k, v, kkraw, a):
    nb, c, _ = r.shape
    ri = lax.broadcasted_iota(jnp.int32, (nb, c, c), 1)
    ci = lax.broadcasted_iota(jnp.int32, (nb, c, c), 2)
    incl, strict, eye = ri >= ci, ri > ci, ri == ci
    kk = kkraw / jnp.maximum(jnp.sqrt(jnp.sum(kkraw * kkraw, axis=-1, keepdims=True)), 1e-12)
    cum = lax.dot_general(jnp.where(incl, 1.0, 0.0), lw, _B_NN, precision=lax.Precision.HIGHEST,
                          preferred_element_type=F32)
    tot = cum[:, c - 1:c, :]
    b = kk * a
    e_neg = jnp.exp(-cum)
    e_rem = jnp.exp(tot - cum)
    r_t = r * jnp.exp(cum)
    k_a = kk * jnp.exp(cum - lw)
    lhs = jnp.concatenate([r_t, k_a], axis=1)
    to_k = _bdot(lhs, k * e_neg, _B_NT)
    to_b = _bdot(lhs, b * e_neg, _B_NT)
    a_rk = jnp.where(incl, to_k[:, 0:c], 0.0)
    a_kk = jnp.where(strict, to_k[:, c:], 0.0)
    a_rb = jnp.where(incl, to_b[:, 0:c], 0.0)
    a_kb = jnp.where(strict, to_b[:, c:], 0.0)
    p = -a_kb
    t_inv = jnp.where(eye, 1.0, 0.0) + p
    for _ in range(int(math.log2(c)) - 1):
        p = _bdot(p, p, _B_NN)
        t_inv = t_inv + _bdot(t_inv, p, _B_NN)
    w_k = _bdot(t_inv, k_a, _B_NN)
    u_v = _bdot(t_inv, _bdot(a_kk, v, _B_NN), _B_NN)
    r_q = r_t - _bdot(a_rb, w_k, _B_NN)
    y_v = _bdot(a_rk, v, _B_NN) - _bdot(a_rb, u_v, _B_NN)
    b_h = b * e_rem
    m = jnp.where(eye, jnp.exp(tot), 0.0) - _bdot(b_h, w_k, _B_TN)
    g = _bdot(k * e_rem, v, _B_TN) - _bdot(b_h, u_v, _B_TN)
    return r_q, y_v, m, g


def _rwkv_kernel(zb_ref, mu_ref, w0_ref, a0_ref, kk_ref, ka_ref, ww2_ref, wa2_ref, wg2_ref,
                 rk_ref, lnw_ref, lnb_ref, o_ref, h_ref, prev_ref):
    @pl.when(pl.program_id(1) == 0)
    def _():
        h_ref[...] = jnp.zeros_like(h_ref)
        prev_ref[...] = jnp.zeros_like(prev_ref)

    z = zb_ref[...]
    step = z.shape[0]
    heads, _, n = h_ref.shape
    tok = _rwkv_token_terms(z, prev_ref[...], mu_ref[...], w0_ref[...], a0_ref[...], kk_ref[...],
                            ka_ref[...], ww2_ref[...], wa2_ref[...], wg2_ref[...])
    prev_ref[...] = z[step - 1:step, :]
    r, lw, k, v, kkraw, a = (_split_heads(x, heads) for x in tok[:6])
    c = RWKV_CHUNK
    n_chunks = step // c
    chunks = lambda x: x.reshape(heads * n_chunks, c, n)
    terms = _rwkv_chunk_terms(*(chunks(x) for x in (r, lw, k, v, kkraw, a)))
    r_q, y_v, m, g = (x.reshape(heads, n_chunks, c, n) for x in terms)
    h = h_ref[...]
    ys = []
    for j in range(n_chunks):
        ys.append(_bdot(r_q[:, j], h, _B_NN) + y_v[:, j])
        h = _bdot(m[:, j], h, _B_NN) + g[:, j]
    h_ref[...] = h
    y = jnp.concatenate(ys, axis=1)
    mean = jnp.mean(y, axis=-1, keepdims=True)
    var = jnp.mean(jnp.square(y - mean), axis=-1, keepdims=True)
    yn = (y - mean) * lax.rsqrt(var + LNX_EPS) * lnw_ref[...] + lnb_ref[...]
    bonus = jnp.sum(r * k * rk_ref[...], axis=-1, keepdims=True) * v
    o_ref[...] = _merge_heads(yn + bonus) * tok[6]


def _rwkv(zb, mu, w0, a0, k_k, k_a, ww2, wa2, wg2, r_k, lnx_w, lnx_b, bsz, seq):
    step = min(RWKV_STEP, seq)
    per_seq = seq // step
    row = lambda n: pl.BlockSpec((1, n), lambda b, i: (0, 0))
    mat = pl.BlockSpec((RWKV_LORA_PAD, RWKV_WIDTH), lambda b, i: (0, 0))
    par = pl.BlockSpec((RWKV_HEADS, 1, RWKV_HEAD), lambda b, i: (0, 0, 0))
    return pl.pallas_call(
        _rwkv_kernel,
        grid=(bsz, per_seq),
        in_specs=[pl.BlockSpec((step, RWKV_COLS), lambda b, i: (b * per_seq + i, 0)),
                  row(RWKV_COLS), row(RWKV_WIDTH), row(RWKV_WIDTH), row(RWKV_WIDTH), row(RWKV_WIDTH),
                  mat, mat, mat, par, par, par],
        out_specs=pl.BlockSpec((step, RWKV_WIDTH), lambda b, i: (b * per_seq + i, 0)),
        out_shape=jax.ShapeDtypeStruct((bsz * seq, RWKV_WIDTH), F32),
        scratch_shapes=[pltpu.VMEM((RWKV_HEADS, RWKV_HEAD, RWKV_HEAD), F32), pltpu.VMEM((1, RWKV_COLS), F32)],
        compiler_params=_params("parallel", "arbitrary"),
    )(zb, mu, w0, a0, k_k, k_a, ww2, wa2, wg2, r_k, lnx_w, lnx_b)


def _merge_kernel(x_ref, oa_ref, ob_ref, zc_ref, wa_ref, wb_ref, wo_ref, gn_ref, o_ref):
    gate_a = _sigmoid(zc_ref[:, 0:D_MODEL])
    gate_b = _sigmoid(zc_ref[:, D_MODEL:2 * D_MODEL])
    mixed = gate_a * _dot(oa_ref[...], wa_ref[...]) + gate_b * _dot(ob_ref[...], wb_ref[...])
    o_ref[...] = x_ref[...] + _rms_norm(_dot(mixed, wo_ref[...]), gn_ref[...])


def _merge(x2, oa, ob, zc, wa, wb, wo, gn, tm=256):
    t = x2.shape[0]
    tile = lambda n: pl.BlockSpec((tm, n), lambda i: (i, 0))
    const = lambda a, b: pl.BlockSpec((a, b), lambda i: (0, 0))
    return pl.pallas_call(
        _merge_kernel,
        grid=(t // tm,),
        in_specs=[tile(D_MODEL), tile(NSA_WIDTH), tile(RWKV_WIDTH), tile(MERGE_COLS),
                  const(NSA_WIDTH, D_MODEL), const(RWKV_WIDTH, D_MODEL), const(D_MODEL, D_MODEL),
                  const(1, D_MODEL)],
        out_specs=tile(D_MODEL),
        out_shape=jax.ShapeDtypeStruct((t, D_MODEL), F32),
        compiler_params=_params("parallel"),
    )(x2, oa, ob, zc, wa, wb, wo, gn)


def _ffn_kernel(x_ref, gpre_ref, gpost_ref, wg_ref, wu_ref, wd_ref, o_ref, *, ff_chunk):
    x = x_ref[...]
    h = _rms_norm(x, gpre_ref[...]).astype(BF16)
    acc = jnp.zeros(x.shape, F32)
    for c in range(0, D_FF, ff_chunk):
        gt = jnp.dot(h, wg_ref[:, c:c + ff_chunk], preferred_element_type=F32)
        up = jnp.dot(h, wu_ref[:, c:c + ff_chunk], preferred_element_type=F32)
        acc = acc + _dot(gt * _sigmoid(gt) * up, wd_ref[c:c + ff_chunk, :])
    o_ref[...] = x + _rms_norm(acc, gpost_ref[...])


def _ffn(x2, gpre, gpost, wg, wu, wd, tm=256, ff_chunk=256):
    t = x2.shape[0]
    const = lambda a, b: pl.BlockSpec((a, b), lambda i: (0, 0))
    return pl.pallas_call(
        functools.partial(_ffn_kernel, ff_chunk=ff_chunk),
        grid=(t // tm,),
        in_specs=[pl.BlockSpec((tm, D_MODEL), lambda i: (i, 0)), const(1, D_MODEL), const(1, D_MODEL),
                  const(D_MODEL, D_FF), const(D_MODEL, D_FF), const(D_FF, D_MODEL)],
        out_specs=pl.BlockSpec((tm, D_MODEL), lambda i: (i, 0)),
        out_shape=jax.ShapeDtypeStruct((t, D_MODEL), F32),
        compiler_params=_params("parallel"),
    )(x2, gpre, gpost, wg, wu, wd)


def _pad_cols(a, n):
    return jnp.pad(a, ((0, 0), (0, n - a.shape[1])))


def _pack_w_in(w_in):
    g0 = NSA_WIDTH + 6 * KV_WIDTH
    r0 = g0 + 3 * NSA_HEADS
    l0 = r0 + 3 * RWKV_WIDTH
    m0 = l0 + RWKV_LORA
    gates = w_in[:, g0:r0].reshape(-1, 3, NSA_KV_GROUPS, GROUP_HEADS).transpose(0, 2, 1, 3)
    gates = jnp.pad(gates.reshape(-1, NSA_KV_GROUPS, 3 * GROUP_HEADS),
                    ((0, 0), (0, 0), (0, LANES - 3 * GROUP_HEADS))).reshape(-1, NSA_KV_GROUPS * LANES)
    return jnp.concatenate([w_in[:, :g0], gates,
                            w_in[:, r0:l0], _pad_cols(w_in[:, l0:m0], RWKV_LORA_PAD),
                            w_in[:, m0:]], axis=1).astype(BF16)


def _rope_tables(seq):
    inv = 1.0 / (ROPE_THETA ** (jnp.arange(0, HEAD_DIM, 2, dtype=F32) / HEAD_DIM))
    ang = jnp.arange(seq, dtype=F32)[:, None] * inv[None, :]
    cos, sin = jnp.cos(ang), jnp.sin(ang)
    reps = LANES // HEAD_DIM
    return (jnp.concatenate([cos, cos] * reps, axis=1), jnp.concatenate([-sin, sin] * reps, axis=1))


def _layer(x, norm1_pre, norm1_post, w_in,
           cmp_pe_k, cmp_w1_k, cmp_b1_k, cmp_w2_k, cmp_pe_v, cmp_w1_v, cmp_b1_v, cmp_w2_v,
           mu_r, mu_k, mu_v, mu_w, mu_a, mu_g, w0, w_w2, a0, w_a2, w_g2,
           k_k, k_a, r_k, lnx_w, lnx_b, w_branch_a, w_branch_b, w_out,
           norm2_pre, norm2_post, w_gate, w_up, w_down):
    bsz, seq, _ = x.shape
    assert seq % RWKV_STEP == 0 or seq < RWKV_STEP
    assert seq // SLC_LEN <= MAX_SLC_BLOCKS and seq >= WINDOW + Q_BLOCK
    t = bsz * seq
    x2 = x.reshape(t, D_MODEL)
    row = lambda a: a.reshape(1, -1)

    za, zb, zc = _in_proj(x2, row(norm1_pre), _pack_w_in(w_in))

    cos_t, sin_t = _rope_tables(seq)
    qp, qr, kaug, kwin, vsel, vwin = _nsa_prep(za, cos_t, sin_t, bsz, seq)
    g_, hd = NSA_KV_GROUPS, HEAD_DIM

    def both_groups(w, cols):
        z = jnp.zeros(w.shape[:-2] + (g_ * w.shape[-2], g_ * cols), w.dtype)
        for g in range(g_):
            z = z.at[..., g * w.shape[-2]:(g + 1) * w.shape[-2], g * cols:g * cols + w.shape[-1]].set(w)
        return z

    w1 = jnp.stack([cmp_w1_k, cmp_w1_v]).reshape(2, CMP_LEN, hd, CMP_HIDDEN)
    cmp_out = _compress(za,
                        jnp.tile(jnp.stack([cmp_pe_k, cmp_pe_v]), (1, 1, g_)),
                        both_groups(w1, CMP_HIDDEN).astype(BF16),
                        jnp.tile(jnp.stack([cmp_b1_k, cmp_b1_v]).reshape(2, 1, CMP_HIDDEN), (1, 1, g_)),
                        both_groups(jnp.stack([cmp_w2_k, cmp_w2_v]), LANES).astype(BF16),
                        bsz, seq)

    n_half = seq // CMP_STRIDE
    n_slc = seq // SLC_LEN
    cmp_start = jnp.arange(n_half) * CMP_STRIDE
    slc_start = jnp.arange(MAX_SLC_BLOCKS) * SLC_LEN
    ovt = ((cmp_start[None, :] < slc_start[:, None] + SLC_LEN)
           & (cmp_start[None, :] + CMP_LEN - 1 >= slc_start[:, None])
           & (jnp.arange(MAX_SLC_BLOCKS)[:, None] < n_slc)).astype(BF16)
    o_a = _nsa_attn(qp, qr, cmp_out[0], cmp_out[1], kaug, vsel, kwin, vwin, za, ovt, bsz, seq)

    mu = _pad_cols(jnp.concatenate([mu_r, mu_k, mu_v, mu_w, mu_a, mu_g]).reshape(1, -1), RWKV_COLS)
    lora = jnp.zeros((3, RWKV_LORA_PAD, RWKV_WIDTH), F32)
    lora = lora.at[0, 0:DECAY_LORA].set(w_w2)
    lora = lora.at[1, DECAY_LORA:DECAY_LORA + AAA_LORA].set(w_a2)
    lora = lora.at[2, DECAY_LORA + AAA_LORA:RWKV_LORA].set(w_g2).astype(BF16)
    hp = lambda a: a.reshape(RWKV_HEADS, 1, RWKV_HEAD)
    o_b = _rwkv(zb, mu, row(w0), row(a0), row(k_k), row(k_a), lora[0], lora[1], lora[2],
                hp(r_k), hp(lnx_w), hp(lnx_b), bsz, seq)

    x1 = _merge(x2, o_a, o_b, zc, w_branch_a.astype(BF16), w_branch_b.astype(BF16),
                w_out.astype(BF16), row(norm1_post))
    out = _ffn(x1, row(norm2_pre), row(norm2_post), w_gate.astype(BF16), w_up.astype(BF16),
               w_down.astype(BF16))
    return out.reshape(bsz, seq, D_MODEL)


def kernel(x, norm1_pre, norm1_post, w_in, cmp_pe_k, cmp_w1_k, cmp_b1_k, cmp_w2_k, cmp_pe_v, cmp_w1_v, cmp_b1_v, cmp_w2_v, mu_r, mu_k, mu_v, mu_w, mu_a, mu_g, w0, w_w2, a0, w_a2, w_g2, k_k, k_a, r_k, lnx_w, lnx_b, w_branch_a, w_branch_b, w_out, norm2_pre, norm2_post, w_gate, w_up, w_down):
    params = (norm1_pre, norm1_post, w_in, cmp_pe_k, cmp_w1_k, cmp_b1_k, cmp_w2_k, cmp_pe_v, cmp_w1_v,
              cmp_b1_v, cmp_w2_v, mu_r, mu_k, mu_v, mu_w, mu_a, mu_g, w0, w_w2, a0, w_a2, w_g2,
              k_k, k_a, r_k, lnx_w, lnx_b, w_branch_a, w_branch_b, w_out,
              norm2_pre, norm2_post, w_gate, w_up, w_down)
    for layer in range(norm1_pre.shape[0]):
        x = _layer(x, *[p[layer] for p in params])
    return x
```

```python
import functools
import math

import jax
import jax.numpy as jnp
from jax import lax
from jax.experimental import pallas as pl
from jax.experimental.pallas import tpu as pltpu

F32 = jnp.float32
BF16 = jnp.bfloat16

D_MODEL = 1024
NSA_HEADS = 8
NSA_KV_GROUPS = 2
GROUP_HEADS = NSA_HEADS // NSA_KV_GROUPS
HEAD_DIM = 64
NSA_WIDTH = NSA_HEADS * HEAD_DIM
KV_WIDTH = NSA_KV_GROUPS * HEAD_DIM
CMP_LEN = 32
CMP_STRIDE = 16
CMP_HIDDEN = 256
SLC_LEN = 64
SLC_TOPK = 16
WINDOW = 512
Q_BLOCK = 128
ROPE_THETA = 10000.0
RWKV_HEADS = 8
RWKV_HEAD = 64
RWKV_WIDTH = RWKV_HEADS * RWKV_HEAD
DECAY_LORA = 32
AAA_LORA = 32
GATE_LORA = 96
LNX_EPS = 64e-5
D_FF = 2816
NORM_EPS = 1e-6
BIG = 1e30

LANES = 128
MAX_SLC_BLOCKS = 128
UNSELECTED_BIAS = -30000.0
REMOVED = -3.0e38

NSA_COLS = NSA_WIDTH + 6 * KV_WIDTH + NSA_KV_GROUPS * LANES
RWKV_LORA = DECAY_LORA + AAA_LORA + GATE_LORA
RWKV_LORA_PAD = 256
RWKV_COLS = 3 * RWKV_WIDTH + RWKV_LORA_PAD
MERGE_COLS = 2 * D_MODEL
COL_KC, COL_VC, COL_KS, COL_VS, COL_KW, COL_VW = (NSA_WIDTH + i * KV_WIDTH for i in range(6))
COL_GATE = NSA_WIDTH + 6 * KV_WIDTH

RWKV_CHUNK = 64
RWKV_STEP = 128
VMEM_LIMIT = 56 * 1024 * 1024

_NT = (((1,), (1,)), ((), ()))
_TN = (((0,), (0,)), ((), ()))


def _params(*sem):
    return pltpu.CompilerParams(dimension_semantics=sem, vmem_limit_bytes=VMEM_LIMIT)


def _sigmoid(x):
    return 1.0 / (1.0 + jnp.exp(-x))


def _rms_norm(x, g):
    return x * lax.rsqrt(jnp.mean(x * x, axis=-1, keepdims=True) + NORM_EPS) * g


def _dot(a, b):
    return jnp.dot(a.astype(BF16), b.astype(BF16), preferred_element_type=F32)


def _dot_nt(a, b):
    return lax.dot_general(a.astype(BF16), b.astype(BF16), _NT, preferred_element_type=F32)


def _dot_tn(a, b):
    return lax.dot_general(a.astype(BF16), b.astype(BF16), _TN, preferred_element_type=F32)


def _col_chunks(width, step=512):
    return [(c, min(step, width - c)) for c in range(0, width, step)]


def _in_proj_kernel(x_ref, g_ref, w_ref, za_ref, zb_ref, zc_ref):
    h = _rms_norm(x_ref[...], g_ref[...]).astype(BF16)
    base = 0
    for o_ref in (za_ref, zb_ref, zc_ref):
        for c, n in _col_chunks(o_ref.shape[1]):
            o_ref[:, c:c + n] = jnp.dot(h, w_ref[:, base + c:base + c + n], preferred_element_type=F32)
        base += o_ref.shape[1]


def _in_proj(x2, g, w, tm=256):
    t = x2.shape[0]
    ncols = w.shape[1]
    return pl.pallas_call(
        _in_proj_kernel,
        grid=(t // tm,),
        in_specs=[pl.BlockSpec((tm, D_MODEL), lambda i: (i, 0)),
                  pl.BlockSpec((1, D_MODEL), lambda i: (0, 0)),
                  pl.BlockSpec((D_MODEL, ncols), lambda i: (0, 0))],
        out_specs=[pl.BlockSpec((tm, NSA_COLS), lambda i: (i, 0)),
                   pl.BlockSpec((tm, RWKV_COLS), lambda i: (i, 0)),
                   pl.BlockSpec((tm, MERGE_COLS), lambda i: (i, 0))],
        out_shape=[jax.ShapeDtypeStruct((t, NSA_COLS), F32),
                   jax.ShapeDtypeStruct((t, RWKV_COLS), F32),
                   jax.ShapeDtypeStruct((t, MERGE_COLS), F32)],
        compiler_params=_params("parallel"),
    )(x2, g, w)


def _rope(x, cos, sin_signed):
    w = x.shape[1]
    lane = lax.broadcasted_iota(jnp.int32, x.shape, 1)
    rot = jnp.where((lane & (HEAD_DIM - 1)) < HEAD_DIM // 2,
                    pltpu.roll(x, w - HEAD_DIM // 2, 1), pltpu.roll(x, HEAD_DIM // 2, 1))
    return x * cos + rot * sin_signed


def _pad_heads(x):
    low = lax.broadcasted_iota(jnp.int32, x.shape, 1) < HEAD_DIM
    return jnp.where(low, x, 0.0), jnp.where(low, pltpu.roll(x, HEAD_DIM, 1), 0.0)


def _nsa_prep_kernel(za_ref, cos_ref, sin_ref, qpt_ref, qrt_ref, kaug_ref, kwin_ref, vselt_ref, vwint_ref):
    qscale = HEAD_DIM ** -0.5 * math.log2(math.e)
    cos, sin = cos_ref[...], sin_ref[...]
    tm = cos.shape[0]
    n_qb = tm // Q_BLOCK
    for pair in range(NSA_HEADS // 2):
        q = za_ref[:, pair * LANES:(pair + 1) * LANES]
        for o_ref, val in ((qpt_ref, q * qscale), (qrt_ref, _rope(q, cos, sin) * qscale)):
            for head, padded in zip((2 * pair, 2 * pair + 1), _pad_heads(val)):
                g, r = divmod(head, GROUP_HEADS)
                for qb in range(n_qb):
                    o_ref[g, qb, :, r * Q_BLOCK:(r + 1) * Q_BLOCK] = jnp.transpose(
                        padded[qb * Q_BLOCK:(qb + 1) * Q_BLOCK]).astype(BF16)
    lane = lax.broadcasted_iota(jnp.int32, (tm, LANES), 1)
    pos = pl.program_id(1) * tm + lax.broadcasted_iota(jnp.int32, (tm, LANES), 0)
    onehot = jnp.where(jnp.right_shift(pos, int(math.log2(SLC_LEN))) == lane, 1.0, 0.0).astype(BF16)
    ks = _pad_heads(_rope(za_ref[:, COL_KS:COL_KS + LANES], cos, sin))
    kw = _pad_heads(_rope(za_ref[:, COL_KW:COL_KW + LANES], cos, sin))
    vs = _pad_heads(za_ref[:, COL_VS:COL_VS + LANES])
    vw = _pad_heads(za_ref[:, COL_VW:COL_VW + LANES])
    ones_row = lane == HEAD_DIM
    for g in range(NSA_KV_GROUPS):
        kaug_ref[g, :, 0:LANES] = onehot
        kaug_ref[g, :, LANES:2 * LANES] = ks[g].astype(BF16)
        kwin_ref[g] = kw[g].astype(BF16)
        vselt_ref[g] = jnp.transpose(jnp.where(ones_row, 1.0, vs[g])).astype(BF16)
        vwt = jnp.transpose(jnp.where(ones_row, 1.0, vw[g])).astype(BF16)
        for qb in range(n_qb):
            vwint_ref[g, qb] = vwt[:, qb * Q_BLOCK:(qb + 1) * Q_BLOCK]


NSA_PREP_TILE = 512


def _nsa_prep(za, cos_t, sin_t, bsz, seq):
    tm = min(NSA_PREP_TILE, seq)
    per_seq = seq // tm
    n_qb = tm // Q_BLOCK
    n_q = seq // Q_BLOCK
    g_ = NSA_KV_GROUPS
    rows = GROUP_HEADS * Q_BLOCK
    qt_spec = pl.BlockSpec((None, g_, n_qb, LANES, rows), lambda b, i: (b, 0, i, 0, 0))
    qt_shape = jax.ShapeDtypeStruct((bsz, g_, n_q, LANES, rows), BF16)
    kv = lambda w: pl.BlockSpec((None, g_, tm, w), lambda b, i: (b, 0, i, 0))
    kvs = lambda w: jax.ShapeDtypeStruct((bsz, g_, seq, w), BF16)
    return pl.pallas_call(
        _nsa_prep_kernel,
        grid=(bsz, per_seq),
        in_specs=[pl.BlockSpec((tm, NSA_COLS), lambda b, i: (b * per_seq + i, 0)),
                  pl.BlockSpec((tm, LANES), lambda b, i: (i, 0)),
                  pl.BlockSpec((tm, LANES), lambda b, i: (i, 0))],
        out_specs=[qt_spec, qt_spec, kv(2 * LANES), kv(LANES),
                   pl.BlockSpec((None, g_, None, LANES, tm), lambda b, i: (b, 0, i, 0, 0)),
                   pl.BlockSpec((None, g_, n_qb, LANES, Q_BLOCK), lambda b, i: (b, 0, i, 0, 0))],
        out_shape=[qt_shape, qt_shape, kvs(2 * LANES), kvs(LANES),
                   jax.ShapeDtypeStruct((bsz, g_, per_seq, LANES, tm), BF16),
                   jax.ShapeDtypeStruct((bsz, g_, n_q, LANES, Q_BLOCK), BF16)],
        compiler_params=_params("parallel", "parallel"),
    )(za, cos_t, sin_t)


def _compress_kernel(z_ref, pe_ref, w1_ref, b1_ref, w2_ref, o_ref, ot_ref):
    n_half = z_ref.shape[0] // CMP_STRIDE
    y_lo = jnp.zeros((n_half, NSA_KV_GROUPS * CMP_HIDDEN), F32)
    y_hi = jnp.zeros((n_half, NSA_KV_GROUPS * CMP_HIDDEN), F32)
    for l in range(CMP_STRIDE):
        rows = z_ref[pl.ds(l, n_half, stride=CMP_STRIDE), :]
        y_lo = y_lo + _dot(rows + pe_ref[l:l + 1, :], w1_ref[l])
        y_hi = y_hi + _dot(rows + pe_ref[CMP_STRIDE + l:CMP_STRIDE + l + 1, :], w1_ref[CMP_STRIDE + l])
    pre = y_lo + pltpu.roll(y_hi, n_half - 1, 0) + b1_ref[...]
    h = 0.5 * pre * (1.0 + jnp.tanh(math.sqrt(2.0 / math.pi) * (pre + 0.044715 * (pre * pre * pre))))
    out = _dot(h, w2_ref[...])
    for g in range(NSA_KV_GROUPS):
        tile = out[:, g * LANES:(g + 1) * LANES]
        o_ref[g] = tile.astype(o_ref.dtype)
        ot_ref[g] = jnp.transpose(tile).astype(ot_ref.dtype)


def _compress(za, pe, w1, b1, w2, bsz, seq):
    n_half = seq // CMP_STRIDE
    hid = NSA_KV_GROUPS * CMP_HIDDEN
    return pl.pallas_call(
        _compress_kernel,
        grid=(2, bsz),
        in_specs=[pl.BlockSpec((seq, LANES), lambda s, b: (b, COL_KC // LANES + s)),
                  pl.BlockSpec((None, CMP_LEN, LANES), lambda s, b: (s, 0, 0)),
                  pl.BlockSpec((None, CMP_LEN, LANES, hid), lambda s, b: (s, 0, 0, 0)),
                  pl.BlockSpec((None, 1, hid), lambda s, b: (s, 0, 0)),
                  pl.BlockSpec((None, hid, NSA_KV_GROUPS * LANES), lambda s, b: (s, 0, 0))],
        out_specs=[pl.BlockSpec((None, None, NSA_KV_GROUPS, n_half, LANES), lambda s, b: (s, b, 0, 0, 0)),
                   pl.BlockSpec((None, None, NSA_KV_GROUPS, LANES, n_half), lambda s, b: (s, b, 0, 0, 0))],
        out_shape=[jax.ShapeDtypeStruct((2, bsz, NSA_KV_GROUPS, n_half, LANES), BF16),
                   jax.ShapeDtypeStruct((2, bsz, NSA_KV_GROUPS, LANES, n_half), BF16)],
        compiler_params=_params("parallel", "parallel"),
    )(za, pe, w1, b1, w2)


def _nsa_attn_kernel(qpt_ref, qrt_ref, kc_ref, vct_ref, kaug_ref, vst_ref, kw_ref, vwt_ref, gate_ref, ovt_ref,
                     o_ref, sa_ref, sb_ref, *, seq, key_tile, n_sel):
    cols = GROUP_HEADS * Q_BLOCK
    q0 = pl.program_id(2) * Q_BLOCK
    t_row = q0 + (lax.broadcasted_iota(jnp.int32, (1, cols), 1) & (Q_BLOCK - 1))

    n_cmp = kc_ref.shape[0]
    s_c = jnp.dot(kc_ref[...], qpt_ref[...], preferred_element_type=F32)
    cmp_end = lax.broadcasted_iota(jnp.int32, (n_cmp, 1), 0) * CMP_STRIDE + (CMP_LEN - 1)
    valid_c = cmp_end <= t_row
    s_c = jnp.where(valid_c, s_c, -BIG)
    e_c = jnp.exp2(s_c - jnp.max(s_c, axis=0, keepdims=True))
    p_c = jnp.where(valid_c, e_c * (1.0 / jnp.sum(e_c, axis=0, keepdims=True)), 0.0)
    o_c = jnp.dot(vct_ref[...], p_c.astype(BF16), preferred_element_type=F32)

    p_sum = p_c[:, 0:Q_BLOCK]
    for r in range(1, GROUP_HEADS):
        p_sum = p_sum + p_c[:, r * Q_BLOCK:(r + 1) * Q_BLOCK]
    p_hi = p_sum.astype(BF16)
    p_lo = (p_sum - p_hi.astype(F32)).astype(BF16)
    ovt = ovt_ref[...]
    imp = (jnp.dot(ovt, p_hi, preferred_element_type=F32)
           + jnp.dot(ovt, p_lo, preferred_element_type=F32))
    blk = lax.broadcasted_iota(jnp.int32, imp.shape, 0)
    cur = jnp.right_shift(q0 + lax.broadcasted_iota(jnp.int32, imp.shape, 1), int(math.log2(SLC_LEN)))
    imp = jnp.where(blk > cur, -BIG, imp)
    imp = jnp.where((blk == 0) | (blk == cur), BIG, imp)

    blk_f = blk.astype(F32)
    bias_t = jnp.full(imp.shape, UNSELECTED_BIAS, F32)
    for _ in range(n_sel):
        mx = jnp.max(imp, axis=0, keepdims=True)
        first = jnp.min(jnp.where(imp == mx, blk_f, float(MAX_SLC_BLOCKS)), axis=0, keepdims=True)
        hit = blk_f == first
        bias_t = jnp.where(hit, 0.0, bias_t)
        imp = jnp.where(hit, REMOVED, imp)
    qrt = qrt_ref[...]
    q_aug = jnp.concatenate([jnp.concatenate([bias_t.astype(BF16)] * GROUP_HEADS, axis=1), qrt], axis=0)
    k_iota = lax.broadcasted_iota(jnp.int32, (key_tile, 1), 0)

    def scores(kt):
        return jnp.dot(kaug_ref[pl.ds(pl.multiple_of(kt * key_tile, key_tile), key_tile), :], q_aug,
                       preferred_element_type=F32)

    def tile_update(kt, s, m, acc):
        m_new = jnp.maximum(m, jnp.max(s, axis=0, keepdims=True))
        p = jnp.exp2(s - m_new)
        acc = jnp.exp2(m - m_new) * acc + jnp.dot(vst_ref[kt], p.astype(BF16), preferred_element_type=F32)
        return m_new, acc

    def causal(kt, s):
        return jnp.where(kt * key_tile + k_iota <= t_row, s, -BIG)

    n_last = q0 // key_tile
    n_pairs = n_last // 2
    sa_ref[...] = scores(0)

    def pair(j, carry):
        m, acc = carry
        sb_ref[...] = scores(2 * j + 1)
        m, acc = tile_update(2 * j, sa_ref[...], m, acc)
        sa_ref[...] = scores(2 * j + 2)
        return tile_update(2 * j + 1, sb_ref[...], m, acc)

    m_s, acc_s = lax.fori_loop(0, n_pairs, pair,
                               (jnp.full((1, cols), -BIG, F32), jnp.zeros((LANES, cols), F32)))
    odd = n_last > 2 * n_pairs
    sb_ref[...] = scores(n_last)
    m_s, acc_s = tile_update(2 * n_pairs, causal(2 * n_pairs, sa_ref[...]), m_s, acc_s)
    _, acc_s = lax.cond(odd, lambda: tile_update(n_last, causal(n_last, sb_ref[...]), m_s, acc_s),
                        lambda: (m_s, acc_s))
    o_s = acc_s * (1.0 / acc_s[HEAD_DIM:HEAD_DIM + 1, :])

    n_chunks = WINDOW // Q_BLOCK + 1
    c0 = jnp.maximum(pl.program_id(2) - WINDOW // Q_BLOCK, 0)
    kstart = pl.multiple_of(c0 * Q_BLOCK, Q_BLOCK)
    s_w = jnp.dot(kw_ref[pl.ds(kstart, n_chunks * Q_BLOCK), :], qrt, preferred_element_type=F32)
    kpos = kstart + lax.broadcasted_iota(jnp.int32, (n_chunks * Q_BLOCK, 1), 0)
    s_w = jnp.where((kpos <= t_row) & (kpos > t_row - WINDOW), s_w, -BIG)
    p_w = jnp.exp2(s_w - jnp.max(s_w, axis=0, keepdims=True)).astype(BF16)
    acc_w = jnp.dot(vwt_ref[c0], p_w[0:Q_BLOCK], preferred_element_type=F32)
    for c in range(1, n_chunks):
        acc_w = acc_w + jnp.dot(vwt_ref[c0 + c], p_w[c * Q_BLOCK:(c + 1) * Q_BLOCK], preferred_element_type=F32)
    o_w = acc_w * (1.0 / acc_w[HEAD_DIM:HEAD_DIM + 1, :])

    gate_t = jnp.transpose(_sigmoid(gate_ref[...]))
    low = lax.broadcasted_iota(jnp.int32, (Q_BLOCK, LANES), 1) < HEAD_DIM
    heads = []
    for r in range(GROUP_HEADS):
        sl = slice(r * Q_BLOCK, (r + 1) * Q_BLOCK)
        heads.append(jnp.transpose(
            gate_t[r:r + 1, :] * o_c[:, sl]
            + gate_t[GROUP_HEADS + r:GROUP_HEADS + r + 1, :] * o_s[:, sl]
            + gate_t[2 * GROUP_HEADS + r:2 * GROUP_HEADS + r + 1, :] * o_w[:, sl]))
    for pair in range(GROUP_HEADS // 2):
        o_ref[:, pair * LANES:(pair + 1) * LANES] = jnp.where(
            low, heads[2 * pair], pltpu.roll(heads[2 * pair + 1], HEAD_DIM, 1))


def _nsa_attn(qpt, qrt, kcmp, vcmpt, kaug, vst, kw, vwt, za, ovt, bsz, seq):
    n_cmp = kcmp.shape[2]
    n_q = seq // Q_BLOCK
    key_tile = vst.shape[-1]
    n_sel = min(SLC_TOPK, seq // SLC_LEN)
    kern = functools.partial(_nsa_attn_kernel, seq=seq, key_tile=key_tile, n_sel=n_sel)
    full = lambda *dims: pl.BlockSpec((None, None) + dims, lambda b, g, i: (b, g) + (0,) * len(dims))
    qspec = pl.BlockSpec((None, None, None, LANES, GROUP_HEADS * Q_BLOCK), lambda b, g, i: (b, g, i, 0, 0))
    return pl.pallas_call(
        kern,
        grid=(bsz, NSA_KV_GROUPS, n_q),
        in_specs=[qspec, qspec, full(n_cmp, LANES), full(LANES, n_cmp),
                  full(seq, 2 * LANES), full(seq // key_tile, LANES, key_tile),
                  full(seq, LANES), full(n_q, LANES, Q_BLOCK),
                  pl.BlockSpec((Q_BLOCK, LANES), lambda b, g, i: (b * n_q + i, COL_GATE // LANES + g)),
                  pl.BlockSpec((MAX_SLC_BLOCKS, n_cmp), lambda b, g, i: (0, 0))],
        out_specs=pl.BlockSpec((Q_BLOCK, GROUP_HEADS * HEAD_DIM), lambda b, g, i: (b * n_q + i, g)),
        out_shape=jax.ShapeDtypeStruct((bsz * seq, NSA_WIDTH), F32),
        scratch_shapes=[pltpu.VMEM((key_tile, GROUP_HEADS * Q_BLOCK), F32)] * 2,
        compiler_params=_params("parallel", "parallel", "arbitrary"),
    )(qpt, qrt, kcmp, vcmpt, kaug, vst, kw, vwt, za, ovt)


def _rwkv_token_terms(z, prev_row, mu, w0, a0, k_k, k_a, ww2, wa2, wg2):
    row = lax.broadcasted_iota(jnp.int32, z.shape, 0)
    prev = jnp.where(row == 0, prev_row, pltpu.roll(z, 1, 0))
    zs = z + (prev - z) * mu
    w = RWKV_WIDTH
    r, k, v, lo = zs[:, 0:w], zs[:, w:2 * w], zs[:, 2 * w:3 * w], zs[:, 3 * w:]
    wl = w0 + _dot(jnp.tanh(lo), ww2)
    w_log = -(jnp.maximum(-wl, 0.0) + jnp.log1p(jnp.exp(-jnp.abs(wl)))) - 0.5
    a = _sigmoid(a0 + _dot(lo, wa2))
    lw = -jnp.exp(w_log)
    g = _dot(_sigmoid(lo), wg2)
    return r, lw, k * (1.0 + (a - 1.0) * k_a), v, k * k_k, a, g


def _split_heads(x, heads):
    n = x.shape[1] // heads
    return jnp.stack([x[:, h * n:(h + 1) * n] for h in range(heads)])


def _merge_heads(x):
    return jnp.concatenate([x[h] for h in range(x.shape[0])], axis=-1)


def _bdot(a, b, dims):
    return lax.dot_general(a.astype(BF16), b.astype(BF16), dims, preferred_element_type=F32)


_B_NN = (((2,), (1,)), ((0,), (0,)))
_B_NT = (((2,), (2,)), ((0,), (0,)))
_B_TN = (((1,), (1,)), ((0,), (0,)))


def _rwkv_chunk_terms(r, lw, k, v, kkraw, a):
    nb, c, _ = r.shape
    ri = lax.broadcasted_iota(jnp.int32, (nb, c, c), 1)
    ci = lax.broadcasted_iota(jnp.int32, (nb, c, c), 2)
    incl, strict, eye = ri >= ci, ri > ci, ri == ci
    kk = kkraw / jnp.maximum(jnp.sqrt(jnp.sum(kkraw * kkraw, axis=-1, keepdims=True)), 1e-12)
    cum = lax.dot_general(jnp.where(incl, 1.0, 0.0), lw, _B_NN, precision=lax.Precision.HIGHEST,
                          preferred_element_type=F32)
    tot = cum[:, c - 1:c, :]
    b = kk * a
    e_neg = jnp.exp(-cum)
    e_rem = jnp.exp(tot - cum)
    r_t = r * jnp.exp(cum)
    k_a = kk * jnp.exp(cum - lw)
    lhs = jnp.concatenate([r_t, k_a], axis=1)
    to_k = _bdot(lhs, k * e_neg, _B_NT)
    to_b = _bdot(lhs, b * e_neg, _B_NT)
    a_rk = jnp.where(incl, to_k[:, 0:c], 0.0)
    a_kk = jnp.where(strict, to_k[:, c:], 0.0)
    a_rb = jnp.where(incl, to_b[:, 0:c], 0.0)
    a_kb = jnp.where(strict, to_b[:, c:], 0.0)
    p = -a_kb
    t_inv = jnp.where(eye, 1.0, 0.0) + p
    for _ in range(int(math.log2(c)) - 1):
        p = _bdot(p, p, _B_NN)
        t_inv = t_inv + _bdot(t_inv, p, _B_NN)
    w_k = _bdot(t_inv, k_a, _B_NN)
    u_v = _bdot(t_inv, _bdot(a_kk, v, _B_NN), _B_NN)
    r_q = r_t - _bdot(a_rb, w_k, _B_NN)
    y_v = _bdot(a_rk, v, _B_NN) - _bdot(a_rb, u_v, _B_NN)
    b_h = b * e_rem
    m = jnp.where(eye, jnp.exp(tot), 0.0) - _bdot(b_h, w_k, _B_TN)
    g = _bdot(k * e_rem, v, _B_TN) - _bdot(b_h, u_v, _B_TN)
    return r_q, y_v, m, g


def _rwkv_kernel(zb_ref, mu_ref, w0_ref, a0_ref, kk_ref, ka_ref, ww2_ref, wa2_ref, wg2_ref,
                 rk_ref, lnw_ref, lnb_ref, o_ref, h_ref, prev_ref):
    @pl.when(pl.program_id(1) == 0)
    def _():
        h_ref[...] = jnp.zeros_like(h_ref)
        prev_ref[...] = jnp.zeros_like(prev_ref)

    z = zb_ref[...]
    step = z.shape[0]
    heads, _, n = h_ref.shape
    tok = _rwkv_token_terms(z, prev_ref[...], mu_ref[...], w0_ref[...], a0_ref[...], kk_ref[...],
                            ka_ref[...], ww2_ref[...], wa2_ref[...], wg2_ref[...])
    prev_ref[...] = z[step - 1:step, :]
    r, lw, k, v, kkraw, a = (_split_heads(x, heads) for x in tok[:6])
    c = RWKV_CHUNK
    n_chunks = step // c
    chunks = lambda x: x.reshape(heads * n_chunks, c, n)
    terms = _rwkv_chunk_terms(*(chunks(x) for x in (r, lw, k, v, kkraw, a)))
    r_q, y_v, m, g = (x.reshape(heads, n_chunks, c, n) for x in terms)
    h = h_ref[...]
    ys = []
    for j in range(n_chunks):
        ys.append(_bdot(r_q[:, j], h, _B_NN) + y_v[:, j])
        h = _bdot(m[:, j], h, _B_NN) + g[:, j]
    h_ref[...] = h
    y = jnp.concatenate(ys, axis=1)
    mean = jnp.mean(y, axis=-1, keepdims=True)
    var = jnp.mean(jnp.square(y - mean), axis=-1, keepdims=True)
    yn = (y - mean) * lax.rsqrt(var + LNX_EPS) * lnw_ref[...] + lnb_ref[...]
    bonus = jnp.sum(r * k * rk_ref[...], axis=-1, keepdims=True) * v
    o_ref[...] = _merge_heads(yn + bonus) * tok[6]


def _rwkv(zb, mu, w0, a0, k_k, k_a, ww2, wa2, wg2, r_k, lnx_w, lnx_b, bsz, seq):
    step = min(RWKV_STEP, seq)
    per_seq = seq // step
    row = lambda n: pl.BlockSpec((1, n), lambda b, i: (0, 0))
    mat = pl.BlockSpec((RWKV_LORA_PAD, RWKV_WIDTH), lambda b, i: (0, 0))
    par = pl.BlockSpec((RWKV_HEADS, 1, RWKV_HEAD), lambda b, i: (0, 0, 0))
    return pl.pallas_call(
        _rwkv_kernel,
        grid=(bsz, per_seq),
        in_specs=[pl.BlockSpec((step, RWKV_COLS), lambda b, i: (b * per_seq + i, 0)),
                  row(RWKV_COLS), row(RWKV_WIDTH), row(RWKV_WIDTH), row(RWKV_WIDTH), row(RWKV_WIDTH),
                  mat, mat, mat, par, par, par],
        out_specs=pl.BlockSpec((step, RWKV_WIDTH), lambda b, i: (b * per_seq + i, 0)),
        out_shape=jax.ShapeDtypeStruct((bsz * seq, RWKV_WIDTH), F32),
        scratch_shapes=[pltpu.VMEM((RWKV_HEADS, RWKV_HEAD, RWKV_HEAD), F32), pltpu.VMEM((1, RWKV_COLS), F32)],
        compiler_params=_params("parallel", "arbitrary"),
    )(zb, mu, w0, a0, k_k, k_a, ww2, wa2, wg2, r_k, lnx_w, lnx_b)


def _merge_kernel(x_ref, oa_ref, ob_ref, zc_ref, wa_ref, wb_ref, wo_ref, gn_ref, o_ref):
    gate_a = _sigmoid(zc_ref[:, 0:D_MODEL])
    gate_b = _sigmoid(zc_ref[:, D_MODEL:2 * D_MODEL])
    mixed = gate_a * _dot(oa_ref[...], wa_ref[...]) + gate_b * _dot(ob_ref[...], wb_ref[...])
    o_ref[...] = x_ref[...] + _rms_norm(_dot(mixed, wo_ref[...]), gn_ref[...])


def _merge(x2, oa, ob, zc, wa, wb, wo, gn, tm=256):
    t = x2.shape[0]
    tile = lambda n: pl.BlockSpec((tm, n), lambda i: (i, 0))
    const = lambda a, b: pl.BlockSpec((a, b), lambda i: (0, 0))
    return pl.pallas_call(
        _merge_kernel,
        grid=(t // tm,),
        in_specs=[tile(D_MODEL), tile(NSA_WIDTH), tile(RWKV_WIDTH), tile(MERGE_COLS),
                  const(NSA_WIDTH, D_MODEL), const(RWKV_WIDTH, D_MODEL), const(D_MODEL, D_MODEL),
                  const(1, D_MODEL)],
        out_specs=tile(D_MODEL),
        out_shape=jax.ShapeDtypeStruct((t, D_MODEL), F32),
        compiler_params=_params("parallel"),
    )(x2, oa, ob, zc, wa, wb, wo, gn)


def _ffn_kernel(x_ref, gpre_ref, gpost_ref, wg_ref, wu_ref, wd_ref, o_ref, *, ff_chunk):
    x = x_ref[...]
    h = _rms_norm(x, gpre_ref[...]).astype(BF16)
    acc = jnp.zeros(x.shape, F32)
    for c in range(0, D_FF, ff_chunk):
        gt = jnp.dot(h, wg_ref[:, c:c + ff_chunk], preferred_element_type=F32)
        up = jnp.dot(h, wu_ref[:, c:c + ff_chunk], preferred_element_type=F32)
        acc = acc + _dot(gt * _sigmoid(gt) * up, wd_ref[c:c + ff_chunk, :])
    o_ref[...] = x + _rms_norm(acc, gpost_ref[...])


def _ffn(x2, gpre, gpost, wg, wu, wd, tm=256, ff_chunk=256):
    t = x2.shape[0]
    const = lambda a, b: pl.BlockSpec((a, b), lambda i: (0, 0))
    return pl.pallas_call(
        functools.partial(_ffn_kernel, ff_chunk=ff_chunk),
        grid=(t // tm,),
        in_specs=[pl.BlockSpec((tm, D_MODEL), lambda i: (i, 0)), const(1, D_MODEL), const(1, D_MODEL),
                  const(D_MODEL, D_FF), const(D_MODEL, D_FF), const(D_FF, D_MODEL)],
        out_specs=pl.BlockSpec((tm, D_MODEL), lambda i: (i, 0)),
        out_shape=jax.ShapeDtypeStruct((t, D_MODEL), F32),
        compiler_params=_params("parallel"),
    )(x2, gpre, gpost, wg, wu, wd)


def _pad_cols(a, n):
    return jnp.pad(a, ((0, 0), (0, n - a.shape[1])))


def _pack_w_in(w_in):
    g0 = NSA_WIDTH + 6 * KV_WIDTH
    r0 = g0 + 3 * NSA_HEADS
    l0 = r0 + 3 * RWKV_WIDTH
    m0 = l0 + RWKV_LORA
    gates = w_in[:, g0:r0].reshape(-1, 3, NSA_KV_GROUPS, GROUP_HEADS).transpose(0, 2, 1, 3)
    gates = jnp.pad(gates.reshape(-1, NSA_KV_GROUPS, 3 * GROUP_HEADS),
                    ((0, 0), (0, 0), (0, LANES - 3 * GROUP_HEADS))).reshape(-1, NSA_KV_GROUPS * LANES)
    return jnp.concatenate([w_in[:, :g0], gates,
                            w_in[:, r0:l0], _pad_cols(w_in[:, l0:m0], RWKV_LORA_PAD),
                            w_in[:, m0:]], axis=1).astype(BF16)


def _rope_tables(seq):
    inv = 1.0 / (ROPE_THETA ** (jnp.arange(0, HEAD_DIM, 2, dtype=F32) / HEAD_DIM))
    ang = jnp.arange(seq, dtype=F32)[:, None] * inv[None, :]
    cos, sin = jnp.cos(ang), jnp.sin(ang)
    reps = LANES // HEAD_DIM
    return (jnp.concatenate([cos, cos] * reps, axis=1), jnp.concatenate([-sin, sin] * reps, axis=1))


def _layer(x, norm1_pre, norm1_post, w_in,
           cmp_pe_k, cmp_w1_k, cmp_b1_k, cmp_w2_k, cmp_pe_v, cmp_w1_v, cmp_b1_v, cmp_w2_v,
           mu_r, mu_k, mu_v, mu_w, mu_a, mu_g, w0, w_w2, a0, w_a2, w_g2,
           k_k, k_a, r_k, lnx_w, lnx_b, w_branch_a, w_branch_b, w_out,
           norm2_pre, norm2_post, w_gate, w_up, w_down):
    bsz, seq, _ = x.shape
    assert seq % RWKV_STEP == 0 or seq < RWKV_STEP
    assert seq // SLC_LEN <= MAX_SLC_BLOCKS and seq >= WINDOW + Q_BLOCK
    t = bsz * seq
    x2 = x.reshape(t, D_MODEL)
    row = lambda a: a.reshape(1, -1)

    za, zb, zc = _in_proj(x2, row(norm1_pre), _pack_w_in(w_in))

    cos_t, sin_t = _rope_tables(seq)
    qpt, qrt, kaug, kwin, vselt, vwint = _nsa_prep(za, cos_t, sin_t, bsz, seq)
    g_, hd = NSA_KV_GROUPS, HEAD_DIM

    def both_groups(w, cols):
        z = jnp.zeros(w.shape[:-2] + (g_ * w.shape[-2], g_ * cols), w.dtype)
        for g in range(g_):
            z = z.at[..., g * w.shape[-2]:(g + 1) * w.shape[-2], g * cols:g * cols + w.shape[-1]].set(w)
        return z

    w1 = jnp.stack([cmp_w1_k, cmp_w1_v]).reshape(2, CMP_LEN, hd, CMP_HIDDEN)
    cmp_out, cmp_out_t = _compress(za,
                        jnp.tile(jnp.stack([cmp_pe_k, cmp_pe_v]), (1, 1, g_)),
                        both_groups(w1, CMP_HIDDEN).astype(BF16),
                        jnp.tile(jnp.stack([cmp_b1_k, cmp_b1_v]).reshape(2, 1, CMP_HIDDEN), (1, 1, g_)),
                        both_groups(jnp.stack([cmp_w2_k, cmp_w2_v]), LANES).astype(BF16),
                        bsz, seq)

    n_half = seq // CMP_STRIDE
    n_slc = seq // SLC_LEN
    cmp_start = jnp.arange(n_half) * CMP_STRIDE
    slc_start = jnp.arange(MAX_SLC_BLOCKS) * SLC_LEN
    ovt = ((cmp_start[None, :] < slc_start[:, None] + SLC_LEN)
           & (cmp_start[None, :] + CMP_LEN - 1 >= slc_start[:, None])
           & (jnp.arange(MAX_SLC_BLOCKS)[:, None] < n_slc)).astype(BF16)
    o_a = _nsa_attn(qpt, qrt, cmp_out[0], cmp_out_t[1], kaug, vselt, kwin, vwint, za, ovt, bsz, seq)

    mu = _pad_cols(jnp.concatenate([mu_r, mu_k, mu_v, mu_w, mu_a, mu_g]).reshape(1, -1), RWKV_COLS)
    lora = jnp.zeros((3, RWKV_LORA_PAD, RWKV_WIDTH), F32)
    lora = lora.at[0, 0:DECAY_LORA].set(w_w2)
    lora = lora.at[1, DECAY_LORA:DECAY_LORA + AAA_LORA].set(w_a2)
    lora = lora.at[2, DECAY_LORA + AAA_LORA:RWKV_LORA].set(w_g2).astype(BF16)
    hp = lambda a: a.reshape(RWKV_HEADS, 1, RWKV_HEAD)
    o_b = _rwkv(zb, mu, row(w0), row(a0), row(k_k), row(k_a), lora[0], lora[1], lora[2],
                hp(r_k), hp(lnx_w), hp(lnx_b), bsz, seq)

    x1 = _merge(x2, o_a, o_b, zc, w_branch_a.astype(BF16), w_branch_b.astype(BF16),
                w_out.astype(BF16), row(norm1_post))
    out = _ffn(x1, row(norm2_pre), row(norm2_post), w_gate.astype(BF16), w_up.astype(BF16),
               w_down.astype(BF16))
    return out.reshape(bsz, seq, D_MODEL)


def kernel(x, norm1_pre, norm1_post, w_in, cmp_pe_k, cmp_w1_k, cmp_b1_k, cmp_w2_k, cmp_pe_v, cmp_w1_v, cmp_b1_v, cmp_w2_v, mu_r, mu_k, mu_v, mu_w, mu_a, mu_g, w0, w_w2, a0, w_a2, w_g2, k_k, k_a, r_k, lnx_w, lnx_b, w_branch_a, w_branch_b, w_out, norm2_pre, norm2_post, w_gate, w_up, w_down):
    params = (norm1_pre, norm1_post, w_in, cmp_pe_k, cmp_w1_k, cmp_b1_k, cmp_w2_k, cmp_pe_v, cmp_w1_v,
              cmp_b1_v, cmp_w2_v, mu_r, mu_k, mu_v, mu_w, mu_a, mu_g, w0, w_w2, a0, w_a2, w_g2,
              k_k, k_a, r_k, lnx_w, lnx_b, w_branch_a, w_branch_b, w_out,
              norm2_pre, norm2_post, w_gate, w_up, w_down)
    for layer in range(norm1_pre.shape[0]):
        x = _layer(x, *[p[layer] for p in params])
    return x
```

```python
import functools
import math

import jax
import jax.numpy as jnp
from jax import lax
from jax.experimental import pallas as pl
from jax.experimental.pallas import tpu as pltpu

F32 = jnp.float32
BF16 = jnp.bfloat16

D_MODEL = 1024
NSA_HEADS = 8
NSA_KV_GROUPS = 2
GROUP_HEADS = NSA_HEADS // NSA_KV_GROUPS
HEAD_DIM = 64
NSA_WIDTH = NSA_HEADS * HEAD_DIM
KV_WIDTH = NSA_KV_GROUPS * HEAD_DIM
CMP_LEN = 32
CMP_STRIDE = 16
CMP_HIDDEN = 256
SLC_LEN = 64
SLC_TOPK = 16
WINDOW = 512
Q_BLOCK = 128
ROPE_THETA = 10000.0
RWKV_HEADS = 8
RWKV_HEAD = 64
RWKV_WIDTH = RWKV_HEADS * RWKV_HEAD
DECAY_LORA = 32
AAA_LORA = 32
GATE_LORA = 96
LNX_EPS = 64e-5
D_FF = 2816
NORM_EPS = 1e-6
BIG = 1e30

LANES = 128
MAX_SLC_BLOCKS = 128
UNSELECTED_BIAS = -30000.0
REMOVED = -3.0e38

NSA_COLS = NSA_WIDTH + 6 * KV_WIDTH + NSA_KV_GROUPS * LANES
RWKV_LORA = DECAY_LORA + AAA_LORA + GATE_LORA
RWKV_LORA_PAD = 256
RWKV_COLS = 3 * RWKV_WIDTH + RWKV_LORA_PAD
MERGE_COLS = 2 * D_MODEL
COL_KC, COL_VC, COL_KS, COL_VS, COL_KW, COL_VW = (NSA_WIDTH + i * KV_WIDTH for i in range(6))
COL_GATE = NSA_WIDTH + 6 * KV_WIDTH

RWKV_CHUNK = 64
RWKV_STEP = 256
VMEM_LIMIT = 56 * 1024 * 1024
DENSE_TILE = 512

_NT = (((1,), (1,)), ((), ()))
_TN = (((0,), (0,)), ((), ()))


def _params(*sem):
    return pltpu.CompilerParams(dimension_semantics=sem, vmem_limit_bytes=VMEM_LIMIT)


def _sigmoid(x):
    return 1.0 / (1.0 + jnp.exp(-x))


def _rms_norm(x, g):
    return x * lax.rsqrt(jnp.mean(x * x, axis=-1, keepdims=True) + NORM_EPS) * g


def _dot(a, b):
    return jnp.dot(a.astype(BF16), b.astype(BF16), preferred_element_type=F32)


def _dot_nt(a, b):
    return lax.dot_general(a.astype(BF16), b.astype(BF16), _NT, preferred_element_type=F32)


def _dot_tn(a, b):
    return lax.dot_general(a.astype(BF16), b.astype(BF16), _TN, preferred_element_type=F32)


def _resident(*shape):
    return pl.BlockSpec(shape, lambda i: (0,) * len(shape), pipeline_mode=pl.Buffered(1))


def _col_chunks(width, step=512):
    return [(c, min(step, width - c)) for c in range(0, width, step)]


def _in_proj_kernel(x_ref, g_ref, w_ref, za_ref, zb_ref, zc_ref):
    h = _rms_norm(x_ref[...], g_ref[...]).astype(BF16)
    base = 0
    for o_ref in (za_ref, zb_ref, zc_ref):
        for c, n in _col_chunks(o_ref.shape[1]):
            o_ref[:, c:c + n] = jnp.dot(h, w_ref[:, base + c:base + c + n], preferred_element_type=F32)
        base += o_ref.shape[1]


def _in_proj(x2, g, w, tm=DENSE_TILE):
    t = x2.shape[0]
    ncols = w.shape[1]
    return pl.pallas_call(
        _in_proj_kernel,
        grid=(t // tm,),
        in_specs=[pl.BlockSpec((tm, D_MODEL), lambda i: (i, 0)),
                  _resident(1, D_MODEL), _resident(D_MODEL, ncols)],
        out_specs=[pl.BlockSpec((tm, NSA_COLS), lambda i: (i, 0)),
                   pl.BlockSpec((tm, RWKV_COLS), lambda i: (i, 0)),
                   pl.BlockSpec((tm, MERGE_COLS), lambda i: (i, 0))],
        out_shape=[jax.ShapeDtypeStruct((t, NSA_COLS), F32),
                   jax.ShapeDtypeStruct((t, RWKV_COLS), F32),
                   jax.ShapeDtypeStruct((t, MERGE_COLS), F32)],
        compiler_params=_params("parallel"),
    )(x2, g, w)


def _rope(x, cos, sin_signed):
    w = x.shape[1]
    lane = lax.broadcasted_iota(jnp.int32, x.shape, 1)
    rot = jnp.where((lane & (HEAD_DIM - 1)) < HEAD_DIM // 2,
                    pltpu.roll(x, w - HEAD_DIM // 2, 1), pltpu.roll(x, HEAD_DIM // 2, 1))
    return x * cos + rot * sin_signed


def _pad_heads(x):
    low = lax.broadcasted_iota(jnp.int32, x.shape, 1) < HEAD_DIM
    return jnp.where(low, x, 0.0), jnp.where(low, pltpu.roll(x, HEAD_DIM, 1), 0.0)


def _nsa_prep_kernel(za_ref, cos_ref, sin_ref, qpt_ref, qrt_ref, kaug_ref, kwin_ref, vselt_ref, vwint_ref):
    qscale = HEAD_DIM ** -0.5 * math.log2(math.e)
    cos, sin = cos_ref[...], sin_ref[...]
    tm = cos.shape[0]
    n_qb = tm // Q_BLOCK
    for pair in range(NSA_HEADS // 2):
        q = za_ref[:, pair * LANES:(pair + 1) * LANES]
        for o_ref, val in ((qpt_ref, q * qscale), (qrt_ref, _rope(q, cos, sin) * qscale)):
            for head, padded in zip((2 * pair, 2 * pair + 1), _pad_heads(val)):
                g, r = divmod(head, GROUP_HEADS)
                for qb in range(n_qb):
                    o_ref[g, qb, :, r * Q_BLOCK:(r + 1) * Q_BLOCK] = jnp.transpose(
                        padded[qb * Q_BLOCK:(qb + 1) * Q_BLOCK]).astype(BF16)
    lane = lax.broadcasted_iota(jnp.int32, (tm, LANES), 1)
    pos = pl.program_id(1) * tm + lax.broadcasted_iota(jnp.int32, (tm, LANES), 0)
    onehot = jnp.where(jnp.right_shift(pos, int(math.log2(SLC_LEN))) == lane, 1.0, 0.0).astype(BF16)
    ks = _pad_heads(_rope(za_ref[:, COL_KS:COL_KS + LANES], cos, sin))
    kw = _pad_heads(_rope(za_ref[:, COL_KW:COL_KW + LANES], cos, sin))
    vs = _pad_heads(za_ref[:, COL_VS:COL_VS + LANES])
    vw = _pad_heads(za_ref[:, COL_VW:COL_VW + LANES])
    ones_row = lane == HEAD_DIM
    for g in range(NSA_KV_GROUPS):
        kaug_ref[g, :, 0:LANES] = onehot
        kaug_ref[g, :, LANES:2 * LANES] = ks[g].astype(BF16)
        kwin_ref[g] = kw[g].astype(BF16)
        vselt_ref[g] = jnp.transpose(jnp.where(ones_row, 1.0, vs[g])).astype(BF16)
        vwt = jnp.transpose(jnp.where(ones_row, 1.0, vw[g])).astype(BF16)
        for qb in range(n_qb):
            vwint_ref[g, qb] = vwt[:, qb * Q_BLOCK:(qb + 1) * Q_BLOCK]


NSA_PREP_TILE = 512


def _nsa_prep(za, cos_t, sin_t, bsz, seq):
    tm = min(NSA_PREP_TILE, seq)
    per_seq = seq // tm
    n_qb = tm // Q_BLOCK
    n_q = seq // Q_BLOCK
    g_ = NSA_KV_GROUPS
    rows = GROUP_HEADS * Q_BLOCK
    qt_spec = pl.BlockSpec((None, g_, n_qb, LANES, rows), lambda b, i: (b, 0, i, 0, 0))
    qt_shape = jax.ShapeDtypeStruct((bsz, g_, n_q, LANES, rows), BF16)
    kv = lambda w: pl.BlockSpec((None, g_, tm, w), lambda b, i: (b, 0, i, 0))
    kvs = lambda w: jax.ShapeDtypeStruct((bsz, g_, seq, w), BF16)
    return pl.pallas_call(
        _nsa_prep_kernel,
        grid=(bsz, per_seq),
        in_specs=[pl.BlockSpec((tm, NSA_COLS), lambda b, i: (b * per_seq + i, 0)),
                  pl.BlockSpec((tm, LANES), lambda b, i: (i, 0)),
                  pl.BlockSpec((tm, LANES), lambda b, i: (i, 0))],
        out_specs=[qt_spec, qt_spec, kv(2 * LANES), kv(LANES),
                   pl.BlockSpec((None, g_, None, LANES, tm), lambda b, i: (b, 0, i, 0, 0)),
                   pl.BlockSpec((None, g_, n_qb, LANES, Q_BLOCK), lambda b, i: (b, 0, i, 0, 0))],
        out_shape=[qt_shape, qt_shape, kvs(2 * LANES), kvs(LANES),
                   jax.ShapeDtypeStruct((bsz, g_, per_seq, LANES, tm), BF16),
                   jax.ShapeDtypeStruct((bsz, g_, n_q, LANES, Q_BLOCK), BF16)],
        compiler_params=_params("parallel", "parallel"),
    )(za, cos_t, sin_t)


def _compress_kernel(z_ref, pe_ref, w1_ref, b1_ref, w2_ref, o_ref, ot_ref):
    n_half = z_ref.shape[0] // CMP_STRIDE
    y_lo = jnp.zeros((n_half, NSA_KV_GROUPS * CMP_HIDDEN), F32)
    y_hi = jnp.zeros((n_half, NSA_KV_GROUPS * CMP_HIDDEN), F32)
    for l in range(CMP_STRIDE):
        rows = z_ref[pl.ds(l, n_half, stride=CMP_STRIDE), :]
        y_lo = y_lo + _dot(rows + pe_ref[l:l + 1, :], w1_ref[l])
        y_hi = y_hi + _dot(rows + pe_ref[CMP_STRIDE + l:CMP_STRIDE + l + 1, :], w1_ref[CMP_STRIDE + l])
    pre = y_lo + pltpu.roll(y_hi, n_half - 1, 0) + b1_ref[...]
    h = 0.5 * pre * (1.0 + jnp.tanh(math.sqrt(2.0 / math.pi) * (pre + 0.044715 * (pre * pre * pre))))
    out = _dot(h, w2_ref[...])
    for g in range(NSA_KV_GROUPS):
        tile = out[:, g * LANES:(g + 1) * LANES]
        o_ref[g] = tile.astype(o_ref.dtype)
        ot_ref[g] = jnp.transpose(tile).astype(ot_ref.dtype)


def _compress(za, pe, w1, b1, w2, bsz, seq):
    n_half = seq // CMP_STRIDE
    hid = NSA_KV_GROUPS * CMP_HIDDEN
    return pl.pallas_call(
        _compress_kernel,
        grid=(2, bsz),
        in_specs=[pl.BlockSpec((seq, LANES), lambda s, b: (b, COL_KC // LANES + s)),
                  pl.BlockSpec((None, CMP_LEN, LANES), lambda s, b: (s, 0, 0)),
                  pl.BlockSpec((None, CMP_LEN, LANES, hid), lambda s, b: (s, 0, 0, 0)),
                  pl.BlockSpec((None, 1, hid), lambda s, b: (s, 0, 0)),
                  pl.BlockSpec((None, hid, NSA_KV_GROUPS * LANES), lambda s, b: (s, 0, 0))],
        out_specs=[pl.BlockSpec((None, None, NSA_KV_GROUPS, n_half, LANES), lambda s, b: (s, b, 0, 0, 0)),
                   pl.BlockSpec((None, None, NSA_KV_GROUPS, LANES, n_half), lambda s, b: (s, b, 0, 0, 0))],
        out_shape=[jax.ShapeDtypeStruct((2, bsz, NSA_KV_GROUPS, n_half, LANES), BF16),
                   jax.ShapeDtypeStruct((2, bsz, NSA_KV_GROUPS, LANES, n_half), BF16)],
        compiler_params=_params("parallel", "parallel"),
    )(za, pe, w1, b1, w2)


def _nsa_attn_kernel(qpt_ref, qrt_ref, kc_ref, vct_ref, kaug_ref, vst_ref, kw_ref, vwt_ref, gate_ref, ovt_ref,
                     o_ref, sa_ref, sb_ref, *, seq, key_tile, n_sel):
    cols = GROUP_HEADS * Q_BLOCK
    q0 = pl.program_id(2) * Q_BLOCK
    t_row = q0 + (lax.broadcasted_iota(jnp.int32, (1, cols), 1) & (Q_BLOCK - 1))
    t_q = q0 + lax.broadcasted_iota(jnp.int32, (1, Q_BLOCK), 1)

    def all_heads(mask_bias):
        return jnp.concatenate([mask_bias] * GROUP_HEADS, axis=1)

    n_cmp = kc_ref.shape[0]
    qrt = qrt_ref[...]
    s_c = jnp.dot(kc_ref[...], qpt_ref[...], preferred_element_type=F32)
    n_chunks = WINDOW // Q_BLOCK + 1
    c0 = jnp.maximum(pl.program_id(2) - WINDOW // Q_BLOCK, 0)
    kstart = pl.multiple_of(c0 * Q_BLOCK, Q_BLOCK)
    s_w = jnp.dot(kw_ref[pl.ds(kstart, n_chunks * Q_BLOCK), :], qrt, preferred_element_type=F32)

    cmp_end = lax.broadcasted_iota(jnp.int32, (n_cmp, 1), 0) * CMP_STRIDE + (CMP_LEN - 1)
    s_c = s_c + all_heads(jnp.where(cmp_end <= t_q, 0.0, -BIG))
    e_c = jnp.exp2(s_c - jnp.max(s_c, axis=0, keepdims=True))
    p_c = e_c * jnp.where(t_row >= CMP_LEN - 1, 1.0 / jnp.sum(e_c, axis=0, keepdims=True), 0.0)
    o_c = jnp.dot(vct_ref[...], p_c.astype(BF16), preferred_element_type=F32)

    p_sum = p_c[:, 0:Q_BLOCK]
    for r in range(1, GROUP_HEADS):
        p_sum = p_sum + p_c[:, r * Q_BLOCK:(r + 1) * Q_BLOCK]
    p_hi = p_sum.astype(BF16)
    p_lo = (p_sum - p_hi.astype(F32)).astype(BF16)
    ovt = ovt_ref[...]
    imp = (jnp.dot(ovt, p_hi, preferred_element_type=F32)
           + jnp.dot(ovt, p_lo, preferred_element_type=F32))
    blk = lax.broadcasted_iota(jnp.int32, imp.shape, 0)
    cur = jnp.right_shift(q0 + lax.broadcasted_iota(jnp.int32, imp.shape, 1), int(math.log2(SLC_LEN)))
    imp = jnp.where(blk > cur, -BIG, imp)
    imp = jnp.where((blk == 0) | (blk == cur), BIG, imp)

    age = t_q - (kstart + lax.broadcasted_iota(jnp.int32, (n_chunks * Q_BLOCK, 1), 0))
    s_w = s_w + all_heads(jnp.where((age >= 0) & (age < WINDOW), 0.0, -BIG))
    p_w = jnp.exp2(s_w - jnp.max(s_w, axis=0, keepdims=True)).astype(BF16)
    acc_w = jnp.dot(vwt_ref[c0], p_w[0:Q_BLOCK], preferred_element_type=F32)
    for c in range(1, n_chunks):
        acc_w = acc_w + jnp.dot(vwt_ref[c0 + c], p_w[c * Q_BLOCK:(c + 1) * Q_BLOCK], preferred_element_type=F32)
    o_w = acc_w * (1.0 / acc_w[HEAD_DIM:HEAD_DIM + 1, :])

    blk_f = blk.astype(F32)
    bias_t = jnp.full(imp.shape, UNSELECTED_BIAS, F32)
    for _ in range(n_sel):
        mx = jnp.max(imp, axis=0, keepdims=True)
        first = jnp.min(jnp.where(imp == mx, blk_f, float(MAX_SLC_BLOCKS)), axis=0, keepdims=True)
        hit = blk_f == first
        bias_t = jnp.where(hit, 0.0, bias_t)
        imp = jnp.where(hit, REMOVED, imp)
    q_aug = jnp.concatenate([jnp.concatenate([bias_t.astype(BF16)] * GROUP_HEADS, axis=1), qrt], axis=0)
    k_iota = lax.broadcasted_iota(jnp.int32, (key_tile, 1), 0)

    def scores(kt):
        return jnp.dot(kaug_ref[pl.ds(pl.multiple_of(kt * key_tile, key_tile), key_tile), :], q_aug,
                       preferred_element_type=F32)

    def tile_update(kt, s, m, acc):
        m_new = jnp.maximum(m, jnp.max(s, axis=0, keepdims=True))
        p = jnp.exp2(s - m_new)
        acc = jnp.exp2(m - m_new) * acc + jnp.dot(vst_ref[kt], p.astype(BF16), preferred_element_type=F32)
        return m_new, acc

    def causal(kt, s):
        return s + all_heads(jnp.where(kt * key_tile + k_iota <= t_q, 0.0, -BIG))

    n_last = q0 // key_tile
    n_pairs = n_last // 2
    sa_ref[...] = scores(0)

    def pair(j, carry):
        m, acc = carry
        sb_ref[...] = scores(2 * j + 1)
        m, acc = tile_update(2 * j, sa_ref[...], m, acc)
        sa_ref[...] = scores(2 * j + 2)
        return tile_update(2 * j + 1, sb_ref[...], m, acc)

    m_s, acc_s = lax.fori_loop(0, n_pairs, pair,
                               (jnp.full((1, cols), -BIG, F32), jnp.zeros((LANES, cols), F32)))
    odd = n_last > 2 * n_pairs
    sb_ref[...] = scores(n_last)
    m_s, acc_s = tile_update(2 * n_pairs, causal(2 * n_pairs, sa_ref[...]), m_s, acc_s)
    _, acc_s = lax.cond(odd, lambda: tile_update(n_last, causal(n_last, sb_ref[...]), m_s, acc_s),
                        lambda: (m_s, acc_s))
    o_s = acc_s * (1.0 / acc_s[HEAD_DIM:HEAD_DIM + 1, :])

    gate_t = jnp.transpose(_sigmoid(gate_ref[...]))
    low = lax.broadcasted_iota(jnp.int32, (Q_BLOCK, LANES), 1) < HEAD_DIM
    heads = []
    for r in range(GROUP_HEADS):
        sl = slice(r * Q_BLOCK, (r + 1) * Q_BLOCK)
        heads.append(jnp.transpose(
            gate_t[r:r + 1, :] * o_c[:, sl]
            + gate_t[GROUP_HEADS + r:GROUP_HEADS + r + 1, :] * o_s[:, sl]
            + gate_t[2 * GROUP_HEADS + r:2 * GROUP_HEADS + r + 1, :] * o_w[:, sl]))
    for pair in range(GROUP_HEADS // 2):
        o_ref[:, pair * LANES:(pair + 1) * LANES] = jnp.where(
            low, heads[2 * pair], pltpu.roll(heads[2 * pair + 1], HEAD_DIM, 1))


def _nsa_attn(qpt, qrt, kcmp, vcmpt, kaug, vst, kw, vwt, za, ovt, bsz, seq):
    n_cmp = kcmp.shape[2]
    n_q = seq // Q_BLOCK
    key_tile = vst.shape[-1]
    n_sel = min(SLC_TOPK, seq // SLC_LEN)
    kern = functools.partial(_nsa_attn_kernel, seq=seq, key_tile=key_tile, n_sel=n_sel)
    full = lambda *dims: pl.BlockSpec((None, None) + dims, lambda b, g, i: (b, g) + (0,) * len(dims))
    qspec = pl.BlockSpec((None, None, None, LANES, GROUP_HEADS * Q_BLOCK), lambda b, g, i: (b, g, i, 0, 0))
    return pl.pallas_call(
        kern,
        grid=(bsz, NSA_KV_GROUPS, n_q),
        in_specs=[qspec, qspec, full(n_cmp, LANES), full(LANES, n_cmp),
                  full(seq, 2 * LANES), full(seq // key_tile, LANES, key_tile),
                  full(seq, LANES), full(n_q, LANES, Q_BLOCK),
                  pl.BlockSpec((Q_BLOCK, LANES), lambda b, g, i: (b * n_q + i, COL_GATE // LANES + g)),
                  pl.BlockSpec((MAX_SLC_BLOCKS, n_cmp), lambda b, g, i: (0, 0))],
        out_specs=pl.BlockSpec((Q_BLOCK, GROUP_HEADS * HEAD_DIM), lambda b, g, i: (b * n_q + i, g)),
        out_shape=jax.ShapeDtypeStruct((bsz * seq, NSA_WIDTH), F32),
        scratch_shapes=[pltpu.VMEM((key_tile, GROUP_HEADS * Q_BLOCK), F32)] * 2,
        compiler_params=_params("parallel", "parallel", "arbitrary"),
    )(qpt, qrt, kcmp, vcmpt, kaug, vst, kw, vwt, za, ovt)


def _rwkv_token_terms(z, prev_row, mu, w0, a0, k_k, k_a, ww2, wa2, wg2):
    row = lax.broadcasted_iota(jnp.int32, z.shape, 0)
    prev = jnp.where(row == 0, prev_row, pltpu.roll(z, 1, 0))
    zs = z + (prev - z) * mu
    w = RWKV_WIDTH
    r, k, v, lo = zs[:, 0:w], zs[:, w:2 * w], zs[:, 2 * w:3 * w], zs[:, 3 * w:]
    wl = w0 + _dot(jnp.tanh(lo), ww2)
    w_log = -(jnp.maximum(-wl, 0.0) + jnp.log1p(jnp.exp(-jnp.abs(wl)))) - 0.5
    a = _sigmoid(a0 + _dot(lo, wa2))
    lw = -jnp.exp(w_log)
    g = _dot(_sigmoid(lo), wg2)
    return r, lw, k * (1.0 + (a - 1.0) * k_a), v, k * k_k, a, g


def _split_heads(x, heads):
    n = x.shape[1] // heads
    return jnp.stack([x[:, h * n:(h + 1) * n] for h in range(heads)])


def _merge_heads(x):
    return jnp.concatenate([x[h] for h in range(x.shape[0])], axis=-1)


def _bdot(a, b, dims):
    return lax.dot_general(a.astype(BF16), b.astype(BF16), dims, preferred_element_type=F32)


_B_NN = (((2,), (1,)), ((0,), (0,)))
_B_NT = (((2,), (2,)), ((0,), (0,)))
_B_TN = (((1,), (1,)), ((0,), (0,)))


def _rwkv_chunk_terms(r, lw, k, v, kkraw, a):
    nb, c, _ = r.shape
    ri = lax.broadcasted_iota(jnp.int32, (nb, c, c), 1)
    ci = lax.broadcasted_iota(jnp.int32, (nb, c, c), 2)
    incl, strict, eye = ri >= ci, ri > ci, ri == ci
    kk = kkraw * lax.rsqrt(jnp.maximum(jnp.sum(kkraw * kkraw, axis=-1, keepdims=True), 1e-24))
    cum = lax.dot_general(jnp.where(incl, 1.0, 0.0), lw, _B_NN, precision=lax.Precision.HIGHEST,
                          preferred_element_type=F32)
    tot = cum[:, c - 1:c, :]
    b = kk * a
    e_neg = jnp.exp(-cum)
    e_rem = jnp.exp(tot - cum)
    r_t = r * jnp.exp(cum)
    k_a = kk * jnp.exp(cum - lw)
    lhs = jnp.concatenate([r_t, k_a], axis=1)
    to_k = _bdot(lhs, k * e_neg, _B_NT)
    to_b = _bdot(lhs, b * e_neg, _B_NT)
    a_rk = jnp.where(incl, to_k[:, 0:c], 0.0)
    a_kk = jnp.where(strict, to_k[:, c:], 0.0)
    a_rb = jnp.where(incl, to_b[:, 0:c], 0.0)
    a_kb = jnp.where(strict, to_b[:, c:], 0.0)
    p = -a_kb
    t_inv = jnp.where(eye, 1.0, 0.0) + p
    for _ in range(int(math.log2(c)) - 1):
        pb = p.astype(BF16)
        p = _bdot(pb, pb, _B_NN)
        t_inv = t_inv + _bdot(t_inv, p, _B_NN)
    t_inv, v, a_rb = t_inv.astype(BF16), v.astype(BF16), a_rb.astype(BF16)
    w_k = _bdot(t_inv, k_a, _B_NN).astype(BF16)
    u_v = _bdot(t_inv, _bdot(a_kk, v, _B_NN), _B_NN).astype(BF16)
    r_q = r_t - _bdot(a_rb, w_k, _B_NN)
    y_v = _bdot(a_rk, v, _B_NN) - _bdot(a_rb, u_v, _B_NN)
    b_h = (b * e_rem).astype(BF16)
    m =jnp.where(eye, jnp.exp(tot), 0.0) - _bdot(b_h, w_k, _B_TN)
    g = _bdot(k * e_rem, v, _B_TN) - _bdot(b_h, u_v, _B_TN)
    return r_q, y_v, m, g


def _rwkv_kernel(zb_ref, mu_ref, w0_ref, a0_ref, kk_ref, ka_ref, ww2_ref, wa2_ref, wg2_ref,
                 rk_ref, lnw_ref, lnb_ref, o_ref, h_ref, prev_ref):
    @pl.when(pl.program_id(1) == 0)
    def _():
        h_ref[...] = jnp.zeros_like(h_ref)
        prev_ref[...] = jnp.zeros_like(prev_ref)

    z = zb_ref[...]
    step = z.shape[0]
    heads, _, n = h_ref.shape
    tok = _rwkv_token_terms(z, prev_ref[...], mu_ref[...], w0_ref[...], a0_ref[...], kk_ref[...],
                            ka_ref[...], ww2_ref[...], wa2_ref[...], wg2_ref[...])
    prev_ref[...] = z[step - 1:step, :]
    r, lw, k, v, kkraw, a = (_split_heads(x, heads) for x in tok[:6])
    c = RWKV_CHUNK
    n_chunks = step // c
    chunks = lambda x: x.reshape(heads * n_chunks, c, n)
    terms = _rwkv_chunk_terms(*(chunks(x) for x in (r, lw, k, v, kkraw, a)))
    r_q, y_v, m, g = (x.reshape(heads, n_chunks, c, n) for x in terms)
    h = h_ref[...]
    ys = []
    for j in range(n_chunks):
        ys.append(_bdot(r_q[:, j], h, _B_NN) + y_v[:, j])
        h = _bdot(m[:, j], h, _B_NN) + g[:, j]
    h_ref[...] = h
    y = jnp.concatenate(ys, axis=1)
    mean = jnp.mean(y, axis=-1, keepdims=True)
    var = jnp.mean(jnp.square(y - mean), axis=-1, keepdims=True)
    yn = (y - mean) * lax.rsqrt(var + LNX_EPS) * lnw_ref[...] + lnb_ref[...]
    bonus = jnp.sum(r * k * rk_ref[...], axis=-1, keepdims=True) * v
    o_ref[...] = _merge_heads(yn + bonus) * tok[6]


def _rwkv(zb, mu, w0, a0, k_k, k_a, ww2, wa2, wg2, r_k, lnx_w, lnx_b, bsz, seq):
    step = min(RWKV_STEP, seq)
    per_seq = seq // step
    row = lambda n: pl.BlockSpec((1, n), lambda b, i: (0, 0))
    mat = pl.BlockSpec((RWKV_LORA_PAD, RWKV_WIDTH), lambda b, i: (0, 0))
    par = pl.BlockSpec((RWKV_HEADS, 1, RWKV_HEAD), lambda b, i: (0, 0, 0))
    return pl.pallas_call(
        _rwkv_kernel,
        grid=(bsz, per_seq),
        in_specs=[pl.BlockSpec((step, RWKV_COLS), lambda b, i: (b * per_seq + i, 0)),
                  row(RWKV_COLS), row(RWKV_WIDTH), row(RWKV_WIDTH), row(RWKV_WIDTH), row(RWKV_WIDTH),
                  mat, mat, mat, par, par, par],
        out_specs=pl.BlockSpec((step, RWKV_WIDTH), lambda b, i: (b * per_seq + i, 0)),
        out_shape=jax.ShapeDtypeStruct((bsz * seq, RWKV_WIDTH), F32),
        scratch_shapes=[pltpu.VMEM((RWKV_HEADS, RWKV_HEAD, RWKV_HEAD), F32), pltpu.VMEM((1, RWKV_COLS), F32)],
        compiler_params=_params("parallel", "arbitrary"),
    )(zb, mu, w0, a0, k_k, k_a, ww2, wa2, wg2, r_k, lnx_w, lnx_b)


def _merge_kernel(x_ref, oa_ref, ob_ref, zc_ref, wa_ref, wb_ref, wo_ref, gn_ref, o_ref):
    gate_a = _sigmoid(zc_ref[:, 0:D_MODEL])
    gate_b = _sigmoid(zc_ref[:, D_MODEL:2 * D_MODEL])
    mixed = gate_a * _dot(oa_ref[...], wa_ref[...]) + gate_b * _dot(ob_ref[...], wb_ref[...])
    o_ref[...] = x_ref[...] + _rms_norm(_dot(mixed, wo_ref[...]), gn_ref[...])


def _merge(x2, oa, ob, zc, wa, wb, wo, gn, tm=DENSE_TILE):
    t = x2.shape[0]
    tile = lambda n: pl.BlockSpec((tm, n), lambda i: (i, 0))
    const = _resident
    return pl.pallas_call(
        _merge_kernel,
        grid=(t // tm,),
        in_specs=[tile(D_MODEL), tile(NSA_WIDTH), tile(RWKV_WIDTH), tile(MERGE_COLS),
                  const(NSA_WIDTH, D_MODEL), const(RWKV_WIDTH, D_MODEL), const(D_MODEL, D_MODEL),
                  const(1, D_MODEL)],
        out_specs=tile(D_MODEL),
        out_shape=jax.ShapeDtypeStruct((t, D_MODEL), F32),
        compiler_params=_params("parallel"),
    )(x2, oa, ob, zc, wa, wb, wo, gn)


def _ffn_kernel(x_ref, gpre_ref, gpost_ref, wg_ref, wu_ref, wd_ref, o_ref, *, ff_chunk):
    x = x_ref[...]
    h = _rms_norm(x, gpre_ref[...]).astype(BF16)
    acc = jnp.zeros(x.shape, F32)
    for c in range(0, D_FF, ff_chunk):
        gt = jnp.dot(h, wg_ref[:, c:c + ff_chunk], preferred_element_type=F32)
        up = jnp.dot(h, wu_ref[:, c:c + ff_chunk], preferred_element_type=F32)
        acc = acc + _dot(gt * _sigmoid(gt) * up, wd_ref[c:c + ff_chunk, :])
    o_ref[...] = x + _rms_norm(acc, gpost_ref[...])


def _ffn(x2, gpre, gpost, wg, wu, wd, tm=DENSE_TILE, ff_chunk=256):
    t = x2.shape[0]
    const = _resident
    return pl.pallas_call(
        functools.partial(_ffn_kernel, ff_chunk=ff_chunk),
        grid=(t // tm,),
        in_specs=[pl.BlockSpec((tm, D_MODEL), lambda i: (i, 0)), const(1, D_MODEL), const(1, D_MODEL),
                  const(D_MODEL, D_FF), const(D_MODEL, D_FF), const(D_FF, D_MODEL)],
        out_specs=pl.BlockSpec((tm, D_MODEL), lambda i: (i, 0)),
        out_shape=jax.ShapeDtypeStruct((t, D_MODEL), F32),
        compiler_params=_params("parallel"),
    )(x2, gpre, gpost, wg, wu, wd)


def _pad_cols(a, n):
    return jnp.pad(a, ((0, 0), (0, n - a.shape[1])))


def _pack_w_in(w_in):
    g0 = NSA_WIDTH + 6 * KV_WIDTH
    r0 = g0 + 3 * NSA_HEADS
    l0 = r0 + 3 * RWKV_WIDTH
    m0 = l0 + RWKV_LORA
    gates = w_in[:, g0:r0].reshape(-1, 3, NSA_KV_GROUPS, GROUP_HEADS).transpose(0, 2, 1, 3)
    gates = jnp.pad(gates.reshape(-1, NSA_KV_GROUPS, 3 * GROUP_HEADS),
                    ((0, 0), (0, 0), (0, LANES - 3 * GROUP_HEADS))).reshape(-1, NSA_KV_GROUPS * LANES)
    return jnp.concatenate([w_in[:, :g0], gates,
                            w_in[:, r0:l0], _pad_cols(w_in[:, l0:m0], RWKV_LORA_PAD),
                            w_in[:, m0:]], axis=1).astype(BF16)


def _rope_tables(seq):
    inv = 1.0 / (ROPE_THETA ** (jnp.arange(0, HEAD_DIM, 2, dtype=F32) / HEAD_DIM))
    ang = jnp.arange(seq, dtype=F32)[:, None] * inv[None, :]
    cos, sin = jnp.cos(ang), jnp.sin(ang)
    reps = LANES // HEAD_DIM
    return (jnp.concatenate([cos, cos] * reps, axis=1), jnp.concatenate([-sin, sin] * reps, axis=1))


def _layer(x, norm1_pre, norm1_post, w_in,
           cmp_pe_k, cmp_w1_k, cmp_b1_k, cmp_w2_k, cmp_pe_v, cmp_w1_v, cmp_b1_v, cmp_w2_v,
           mu_r, mu_k, mu_v, mu_w, mu_a, mu_g, w0, w_w2, a0, w_a2, w_g2,
           k_k, k_a, r_k, lnx_w, lnx_b, w_branch_a, w_branch_b, w_out,
           norm2_pre, norm2_post, w_gate, w_up, w_down):
    bsz, seq, _ = x.shape
    assert seq % RWKV_STEP == 0 or seq < RWKV_STEP
    assert seq // SLC_LEN <= MAX_SLC_BLOCKS and seq >= WINDOW + Q_BLOCK
    t = bsz * seq
    x2 = x.reshape(t, D_MODEL)
    row = lambda a: a.reshape(1, -1)

    za, zb, zc = _in_proj(x2, row(norm1_pre), _pack_w_in(w_in))

    cos_t, sin_t = _rope_tables(seq)
    qpt, qrt, kaug, kwin, vselt, vwint = _nsa_prep(za, cos_t, sin_t, bsz, seq)
    g_, hd = NSA_KV_GROUPS, HEAD_DIM

    def both_groups(w, cols):
        z = jnp.zeros(w.shape[:-2] + (g_ * w.shape[-2], g_ * cols), w.dtype)
        for g in range(g_):
            z = z.at[..., g * w.shape[-2]:(g + 1) * w.shape[-2], g * cols:g * cols + w.shape[-1]].set(w)
        return z

    w1 = jnp.stack([cmp_w1_k, cmp_w1_v]).reshape(2, CMP_LEN, hd, CMP_HIDDEN)
    cmp_out, cmp_out_t = _compress(za,
                        jnp.tile(jnp.stack([cmp_pe_k, cmp_pe_v]), (1, 1, g_)),
                        both_groups(w1, CMP_HIDDEN).astype(BF16),
                        jnp.tile(jnp.stack([cmp_b1_k, cmp_b1_v]).reshape(2, 1, CMP_HIDDEN), (1, 1, g_)),
                        both_groups(jnp.stack([cmp_w2_k, cmp_w2_v]), LANES).astype(BF16),
                        bsz, seq)

    n_half = seq // CMP_STRIDE
    n_slc = seq // SLC_LEN
    cmp_start = jnp.arange(n_half) * CMP_STRIDE
    slc_start = jnp.arange(MAX_SLC_BLOCKS) * SLC_LEN
    ovt = ((cmp_start[None, :] < slc_start[:, None] + SLC_LEN)
           & (cmp_start[None, :] + CMP_LEN - 1 >= slc_start[:, None])
           & (jnp.arange(MAX_SLC_BLOCKS)[:, None] < n_slc)).astype(BF16)
    o_a = _nsa_attn(qpt, qrt, cmp_out[0], cmp_out_t[1], kaug, vselt, kwin, vwint, za, ovt, bsz, seq)

    mu = _pad_cols(jnp.concatenate([mu_r, mu_k, mu_v, mu_w, mu_a, mu_g]).reshape(1, -1), RWKV_COLS)
    lora = jnp.zeros((3, RWKV_LORA_PAD, RWKV_WIDTH), F32)
    lora = lora.at[0, 0:DECAY_LORA].set(w_w2)
    lora = lora.at[1, DECAY_LORA:DECAY_LORA + AAA_LORA].set(w_a2)
    lora = lora.at[2, DECAY_LORA + AAA_LORA:RWKV_LORA].set(w_g2).astype(BF16)
    hp = lambda a: a.reshape(RWKV_HEADS, 1, RWKV_HEAD)
    o_b = _rwkv(zb, mu, row(w0), row(a0), row(k_k), row(k_a), lora[0], lora[1], lora[2],
                hp(r_k), hp(lnx_w), hp(lnx_b), bsz, seq)

    x1 = _merge(x2, o_a, o_b, zc, w_branch_a.astype(BF16), w_branch_b.astype(BF16),
                w_out.astype(BF16), row(norm1_post))
    out = _ffn(x1, row(norm2_pre), row(norm2_post), w_gate.astype(BF16), w_up.astype(BF16),
               w_down.astype(BF16))
    return out.reshape(bsz, seq, D_MODEL)


def kernel(x, norm1_pre, norm1_post, w_in, cmp_pe_k, cmp_w1_k, cmp_b1_k, cmp_w2_k, cmp_pe_v, cmp_w1_v, cmp_b1_v, cmp_w2_v, mu_r, mu_k, mu_v, mu_w, mu_a, mu_g, w0, w_w2, a0, w_a2, w_g2, k_k, k_a, r_k, lnx_w, lnx_b, w_branch_a, w_branch_b, w_out, norm2_pre, norm2_post, w_gate, w_up, w_down):
    params = (norm1_pre, norm1_post, w_in, cmp_pe_k, cmp_w1_k, cmp_b1_k, cmp_w2_k, cmp_pe_v, cmp_w1_v,
              cmp_b1_v, cmp_w2_v, mu_r, mu_k, mu_v, mu_w, mu_a, mu_g, w0, w_w2, a0, w_a2, w_g2,
              k_k, k_a, r_k, lnx_w, lnx_b, w_branch_a, w_branch_b, w_out,
              norm2_pre, norm2_post, w_gate, w_up, w_down)
    for layer in range(norm1_pre.shape[0]):
        x = _layer(x, *[p[layer] for p in params])
    return x
```

```python
import functools
import math

import jax
import jax.numpy as jnp
from jax import lax
from jax.experimental import pallas as pl
from jax.experimental.pallas import tpu as pltpu

F32 = jnp.float32
BF16 = jnp.bfloat16

D_MODEL = 1024
NSA_HEADS = 8
NSA_KV_GROUPS = 2
GROUP_HEADS = NSA_HEADS // NSA_KV_GROUPS
HEAD_DIM = 64
NSA_WIDTH = NSA_HEADS * HEAD_DIM
KV_WIDTH = NSA_KV_GROUPS * HEAD_DIM
CMP_LEN = 32
CMP_STRIDE = 16
CMP_HIDDEN = 256
SLC_LEN = 64
SLC_TOPK = 16
WINDOW = 512
Q_BLOCK = 128
ROPE_THETA = 10000.0
RWKV_HEADS = 8
RWKV_HEAD = 64
RWKV_WIDTH = RWKV_HEADS * RWKV_HEAD
DECAY_LORA = 32
AAA_LORA = 32
GATE_LORA = 96
LNX_EPS = 64e-5
D_FF = 2816
NORM_EPS = 1e-6
BIG = 1e30

LANES = 128
MAX_SLC_BLOCKS = 128
UNSELECTED_BIAS = -30000.0
REMOVED = -3.0e38

NSA_COLS = NSA_WIDTH + 6 * KV_WIDTH + NSA_KV_GROUPS * LANES
RWKV_LORA = DECAY_LORA + AAA_LORA + GATE_LORA
RWKV_LORA_PAD = 256
RWKV_COLS = 3 * RWKV_WIDTH + RWKV_LORA_PAD
MERGE_COLS = 2 * D_MODEL
COL_KC, COL_VC, COL_KS, COL_VS, COL_KW, COL_VW = (NSA_WIDTH + i * KV_WIDTH for i in range(6))
COL_GATE = NSA_WIDTH + 6 * KV_WIDTH

RWKV_CHUNK = 64
RWKV_STEP = 256
VMEM_LIMIT = 56 * 1024 * 1024
DENSE_TILE = 512

_NT = (((1,), (1,)), ((), ()))
_TN = (((0,), (0,)), ((), ()))


def _params(*sem):
    return pltpu.CompilerParams(dimension_semantics=sem, vmem_limit_bytes=VMEM_LIMIT)


def _sigmoid(x):
    return 1.0 / (1.0 + jnp.exp(-x))


def _rms_norm(x, g):
    return x * lax.rsqrt(jnp.mean(x * x, axis=-1, keepdims=True) + NORM_EPS) * g


def _dot(a, b):
    return jnp.dot(a.astype(BF16), b.astype(BF16), preferred_element_type=F32)


def _dot_nt(a, b):
    return lax.dot_general(a.astype(BF16), b.astype(BF16), _NT, preferred_element_type=F32)


def _dot_tn(a, b):
    return lax.dot_general(a.astype(BF16), b.astype(BF16), _TN, preferred_element_type=F32)


def _resident(*shape):
    return pl.BlockSpec(shape, lambda i: (0,) * len(shape), pipeline_mode=pl.Buffered(1))


def _col_chunks(width, step=512):
    return [(c, min(step, width - c)) for c in range(0, width, step)]


def _in_proj_kernel(x_ref, g_ref, w_ref, za_ref, zb_ref, zc_ref):
    h = _rms_norm(x_ref[...], g_ref[...]).astype(BF16)
    base = 0
    for o_ref in (za_ref, zb_ref, zc_ref):
        for c, n in _col_chunks(o_ref.shape[1]):
            o_ref[:, c:c + n] = jnp.dot(h, w_ref[:, base + c:base + c + n], preferred_element_type=F32)
        base += o_ref.shape[1]


def _in_proj(x2, g, w, tm=DENSE_TILE):
    t = x2.shape[0]
    ncols = w.shape[1]
    return pl.pallas_call(
        _in_proj_kernel,
        grid=(t // tm,),
        in_specs=[pl.BlockSpec((tm, D_MODEL), lambda i: (i, 0)),
                  _resident(1, D_MODEL), _resident(D_MODEL, ncols)],
        out_specs=[pl.BlockSpec((tm, NSA_COLS), lambda i: (i, 0)),
                   pl.BlockSpec((tm, RWKV_COLS), lambda i: (i, 0)),
                   pl.BlockSpec((tm, MERGE_COLS), lambda i: (i, 0))],
        out_shape=[jax.ShapeDtypeStruct((t, NSA_COLS), F32),
                   jax.ShapeDtypeStruct((t, RWKV_COLS), F32),
                   jax.ShapeDtypeStruct((t, MERGE_COLS), F32)],
        compiler_params=_params("parallel"),
    )(x2, g, w)


def _rope(x, cos, sin_signed):
    w = x.shape[1]
    lane = lax.broadcasted_iota(jnp.int32, x.shape, 1)
    rot = jnp.where((lane & (HEAD_DIM - 1)) < HEAD_DIM // 2,
                    pltpu.roll(x, w - HEAD_DIM // 2, 1), pltpu.roll(x, HEAD_DIM // 2, 1))
    return x * cos + rot * sin_signed


def _pad_heads(x):
    low = lax.broadcasted_iota(jnp.int32, x.shape, 1) < HEAD_DIM
    return jnp.where(low, x, 0.0), jnp.where(low, pltpu.roll(x, HEAD_DIM, 1), 0.0)


def _nsa_prep_kernel(za_ref, cos_ref, sin_ref, qpt_ref, qrt_ref, kaug_ref, kwin_ref, vselt_ref, vwint_ref):
    qscale = HEAD_DIM ** -0.5 * math.log2(math.e)
    cos, sin = cos_ref[...], sin_ref[...]
    tm = cos.shape[0]
    n_qb = tm // Q_BLOCK
    for pair in range(NSA_HEADS // 2):
        q = za_ref[:, pair * LANES:(pair + 1) * LANES]
        for o_ref, val in ((qpt_ref, q * qscale), (qrt_ref, _rope(q, cos, sin) * qscale)):
            for head, padded in zip((2 * pair, 2 * pair + 1), _pad_heads(val)):
                g, r = divmod(head, GROUP_HEADS)
                for qb in range(n_qb):
                    o_ref[g, qb, :, r * Q_BLOCK:(r + 1) * Q_BLOCK] = jnp.transpose(
                        padded[qb * Q_BLOCK:(qb + 1) * Q_BLOCK]).astype(BF16)
    lane = lax.broadcasted_iota(jnp.int32, (tm, LANES), 1)
    pos = pl.program_id(1) * tm + lax.broadcasted_iota(jnp.int32, (tm, LANES), 0)
    onehot = jnp.where(jnp.right_shift(pos, int(math.log2(SLC_LEN))) == lane, 1.0, 0.0).astype(BF16)
    ks = _pad_heads(_rope(za_ref[:, COL_KS:COL_KS + LANES], cos, sin))
    kw = _pad_heads(_rope(za_ref[:, COL_KW:COL_KW + LANES], cos, sin))
    vs = _pad_heads(za_ref[:, COL_VS:COL_VS + LANES])
    vw = _pad_heads(za_ref[:, COL_VW:COL_VW + LANES])
    ones_row = lane == HEAD_DIM
    for g in range(NSA_KV_GROUPS):
        kaug_ref[g, :, 0:LANES] = onehot
        kaug_ref[g, :, LANES:2 * LANES] = ks[g].astype(BF16)
        kwin_ref[g] = kw[g].astype(BF16)
        vselt_ref[g] = jnp.transpose(jnp.where(ones_row, 1.0, vs[g])).astype(BF16)
        vwt = jnp.transpose(jnp.where(ones_row, 1.0, vw[g])).astype(BF16)
        for qb in range(n_qb):
            vwint_ref[g, qb] = vwt[:, qb * Q_BLOCK:(qb + 1) * Q_BLOCK]


NSA_PREP_TILE = 512


def _nsa_prep(za, cos_t, sin_t, bsz, seq):
    tm = min(NSA_PREP_TILE, seq)
    per_seq = seq // tm
    n_qb = tm // Q_BLOCK
    n_q = seq // Q_BLOCK
    g_ = NSA_KV_GROUPS
    rows = GROUP_HEADS * Q_BLOCK
    qt_spec = pl.BlockSpec((None, g_, n_qb, LANES, rows), lambda b, i: (b, 0, i, 0, 0))
    qt_shape = jax.ShapeDtypeStruct((bsz, g_, n_q, LANES, rows), BF16)
    kv = lambda w: pl.BlockSpec((None, g_, tm, w), lambda b, i: (b, 0, i, 0))
    kvs = lambda w: jax.ShapeDtypeStruct((bsz, g_, seq, w), BF16)
    return pl.pallas_call(
        _nsa_prep_kernel,
        grid=(bsz, per_seq),
        in_specs=[pl.BlockSpec((tm, NSA_COLS), lambda b, i: (b * per_seq + i, 0)),
                  pl.BlockSpec((tm, LANES), lambda b, i: (i, 0)),
                  pl.BlockSpec((tm, LANES), lambda b, i: (i, 0))],
        out_specs=[qt_spec, qt_spec, kv(2 * LANES), kv(LANES),
                   pl.BlockSpec((None, g_, None, LANES, tm), lambda b, i: (b, 0, i, 0, 0)),
                   pl.BlockSpec((None, g_, n_qb, LANES, Q_BLOCK), lambda b, i: (b, 0, i, 0, 0))],
        out_shape=[qt_shape, qt_shape, kvs(2 * LANES), kvs(LANES),
                   jax.ShapeDtypeStruct((bsz, g_, per_seq, LANES, tm), BF16),
                   jax.ShapeDtypeStruct((bsz, g_, n_q, LANES, Q_BLOCK), BF16)],
        compiler_params=_params("parallel", "parallel"),
    )(za, cos_t, sin_t)


def _compress_kernel(z_ref, pe_ref, w1_ref, b1_ref, w2_ref, o_ref, ot_ref):
    n_half = z_ref.shape[0] // CMP_STRIDE
    y_lo = jnp.zeros((n_half, NSA_KV_GROUPS * CMP_HIDDEN), F32)
    y_hi = jnp.zeros((n_half, NSA_KV_GROUPS * CMP_HIDDEN), F32)
    for l in range(CMP_STRIDE):
        rows = z_ref[pl.ds(l, n_half, stride=CMP_STRIDE), :]
        y_lo = y_lo + _dot(rows + pe_ref[l:l + 1, :], w1_ref[l])
        y_hi = y_hi + _dot(rows + pe_ref[CMP_STRIDE + l:CMP_STRIDE + l + 1, :], w1_ref[CMP_STRIDE + l])
    pre = y_lo + pltpu.roll(y_hi, n_half - 1, 0) + b1_ref[...]
    h = 0.5 * pre * (1.0 + jnp.tanh(math.sqrt(2.0 / math.pi) * (pre + 0.044715 * (pre * pre * pre))))
    out = _dot(h, w2_ref[...])
    for g in range(NSA_KV_GROUPS):
        tile = out[:, g * LANES:(g + 1) * LANES]
        o_ref[g] = tile.astype(o_ref.dtype)
        ot_ref[g] = jnp.transpose(tile).astype(ot_ref.dtype)


def _compress(za, pe, w1, b1, w2, bsz, seq):
    n_half = seq // CMP_STRIDE
    hid = NSA_KV_GROUPS * CMP_HIDDEN
    return pl.pallas_call(
        _compress_kernel,
        grid=(2, bsz),
        in_specs=[pl.BlockSpec((seq, LANES), lambda s, b: (b, COL_KC // LANES + s)),
                  pl.BlockSpec((None, CMP_LEN, LANES), lambda s, b: (s, 0, 0)),
                  pl.BlockSpec((None, CMP_LEN, LANES, hid), lambda s, b: (s, 0, 0, 0)),
                  pl.BlockSpec((None, 1, hid), lambda s, b: (s, 0, 0)),
                  pl.BlockSpec((None, hid, NSA_KV_GROUPS * LANES), lambda s, b: (s, 0, 0))],
        out_specs=[pl.BlockSpec((None, None, NSA_KV_GROUPS, n_half, LANES), lambda s, b: (s, b, 0, 0, 0)),
                   pl.BlockSpec((None, None, NSA_KV_GROUPS, LANES, n_half), lambda s, b: (s, b, 0, 0, 0))],
        out_shape=[jax.ShapeDtypeStruct((2, bsz, NSA_KV_GROUPS, n_half, LANES), BF16),
                   jax.ShapeDtypeStruct((2, bsz, NSA_KV_GROUPS, LANES, n_half), BF16)],
        compiler_params=_params("parallel", "parallel"),
    )(za, pe, w1, b1, w2)


def _nsa_attn_kernel(qpt_ref, qrt_ref, kc_ref, vct_ref, kaug_ref, vst_ref, kw_ref, vwt_ref, gate_ref, ovt_ref,
                     o_ref, sa_ref, sb_ref, *, seq, key_tile, n_sel):
    cols = GROUP_HEADS * Q_BLOCK
    q0 = pl.program_id(2) * Q_BLOCK
    t_row = q0 + (lax.broadcasted_iota(jnp.int32, (1, cols), 1) & (Q_BLOCK - 1))
    t_q = q0 + lax.broadcasted_iota(jnp.int32, (1, Q_BLOCK), 1)

    def all_heads(mask_bias):
        return jnp.concatenate([mask_bias] * GROUP_HEADS, axis=1)

    n_cmp = kc_ref.shape[0]
    qrt = qrt_ref[...]
    s_c = jnp.dot(kc_ref[...], qpt_ref[...], preferred_element_type=F32)
    n_chunks = WINDOW // Q_BLOCK + 1
    c0 = jnp.maximum(pl.program_id(2) - WINDOW // Q_BLOCK, 0)
    kstart = pl.multiple_of(c0 * Q_BLOCK, Q_BLOCK)
    s_w = jnp.dot(kw_ref[pl.ds(kstart, n_chunks * Q_BLOCK), :], qrt, preferred_element_type=F32)

    cmp_end = lax.broadcasted_iota(jnp.int32, (n_cmp, 1), 0) * CMP_STRIDE + (CMP_LEN - 1)
    s_c = s_c + all_heads(jnp.where(cmp_end <= t_q, 0.0, -BIG))
    e_c = jnp.exp2(s_c - jnp.max(s_c, axis=0, keepdims=True))
    p_c = e_c * jnp.where(t_row >= CMP_LEN - 1, 1.0 / jnp.sum(e_c, axis=0, keepdims=True), 0.0)
    o_c = jnp.dot(vct_ref[...], p_c.astype(BF16), preferred_element_type=F32)

    p_sum = p_c[:, 0:Q_BLOCK]
    for r in range(1, GROUP_HEADS):
        p_sum = p_sum + p_c[:, r * Q_BLOCK:(r + 1) * Q_BLOCK]
    p_hi = p_sum.astype(BF16)
    p_lo = (p_sum - p_hi.astype(F32)).astype(BF16)
    ovt = ovt_ref[...]
    imp = (jnp.dot(ovt, p_hi, preferred_element_type=F32)
           + jnp.dot(ovt, p_lo, preferred_element_type=F32))
    blk = lax.broadcasted_iota(jnp.int32, imp.shape, 0)
    cur = jnp.right_shift(q0 + lax.broadcasted_iota(jnp.int32, imp.shape, 1), int(math.log2(SLC_LEN)))
    imp = jnp.where(blk > cur, -BIG, imp)
    imp = jnp.where((blk == 0) | (blk == cur), BIG, imp)

    age = t_q - (kstart + lax.broadcasted_iota(jnp.int32, (n_chunks * Q_BLOCK, 1), 0))
    s_w = s_w + all_heads(jnp.where((age >= 0) & (age < WINDOW), 0.0, -BIG))
    p_w = jnp.exp2(s_w - jnp.max(s_w, axis=0, keepdims=True)).astype(BF16)
    acc_w = jnp.dot(vwt_ref[c0], p_w[0:Q_BLOCK], preferred_element_type=F32)
    for c in range(1, n_chunks):
        acc_w = acc_w + jnp.dot(vwt_ref[c0 + c], p_w[c * Q_BLOCK:(c + 1) * Q_BLOCK], preferred_element_type=F32)
    o_w = acc_w * (1.0 / acc_w[HEAD_DIM:HEAD_DIM + 1, :])

    blk_f = blk.astype(F32)
    bias_t = jnp.full(imp.shape, UNSELECTED_BIAS, F32)
    for _ in range(n_sel):
        mx = jnp.max(imp, axis=0, keepdims=True)
        first = jnp.min(jnp.where(imp == mx, blk_f, float(MAX_SLC_BLOCKS)), axis=0, keepdims=True)
        hit = blk_f == first
        bias_t = jnp.where(hit, 0.0, bias_t)
        imp = jnp.where(hit, REMOVED, imp)
    q_aug = jnp.concatenate([jnp.concatenate([bias_t.astype(BF16)] * GROUP_HEADS, axis=1), qrt], axis=0)
    k_iota = lax.broadcasted_iota(jnp.int32, (key_tile, 1), 0)

    def scores(kt):
        return jnp.dot(kaug_ref[pl.ds(pl.multiple_of(kt * key_tile, key_tile), key_tile), :], q_aug,
                       preferred_element_type=F32)

    def tile_update(kt, s, m, acc):
        m_new = jnp.maximum(m, jnp.max(s, axis=0, keepdims=True))
        p = jnp.exp2(s - m_new)
        acc = jnp.exp2(m - m_new) * acc + jnp.dot(vst_ref[kt], p.astype(BF16), preferred_element_type=F32)
        return m_new, acc

    def causal(kt, s):
        return s + all_heads(jnp.where(kt * key_tile + k_iota <= t_q, 0.0, -BIG))

    n_last = q0 // key_tile
    n_pairs = n_last // 2
    sa_ref[...] = scores(0)

    def pair(j, carry):
        m, acc = carry
        sb_ref[...] = scores(2 * j + 1)
        m, acc = tile_update(2 * j, sa_ref[...], m, acc)
        sa_ref[...] = scores(2 * j + 2)
        return tile_update(2 * j + 1, sb_ref[...], m, acc)

    m_s, acc_s = lax.fori_loop(0, n_pairs, pair,
                               (jnp.full((1, cols), -BIG, F32), jnp.zeros((LANES, cols), F32)))
    odd = n_last > 2 * n_pairs
    sb_ref[...] = scores(n_last)
    m_s, acc_s = tile_update(2 * n_pairs, causal(2 * n_pairs, sa_ref[...]), m_s, acc_s)
    _, acc_s = lax.cond(odd, lambda: tile_update(n_last, causal(n_last, sb_ref[...]), m_s, acc_s),
                        lambda: (m_s, acc_s))
    o_s = acc_s * (1.0 / acc_s[HEAD_DIM:HEAD_DIM + 1, :])

    gate_t = jnp.transpose(_sigmoid(gate_ref[...]))
    low = lax.broadcasted_iota(jnp.int32, (Q_BLOCK, LANES), 1) < HEAD_DIM
    heads = []
    for r in range(GROUP_HEADS):
        sl = slice(r * Q_BLOCK, (r + 1) * Q_BLOCK)
        heads.append(jnp.transpose(
            gate_t[r:r + 1, :] * o_c[:, sl]
            + gate_t[GROUP_HEADS + r:GROUP_HEADS + r + 1, :] * o_s[:, sl]
            + gate_t[2 * GROUP_HEADS + r:2 * GROUP_HEADS + r + 1, :] * o_w[:, sl]))
    for pair in range(GROUP_HEADS // 2):
        o_ref[:, pair * LANES:(pair + 1) * LANES] = jnp.where(
            low, heads[2 * pair], pltpu.roll(heads[2 * pair + 1], HEAD_DIM, 1))


def _nsa_attn(qpt, qrt, kcmp, vcmpt, kaug, vst, kw, vwt, za, ovt, bsz, seq):
    n_cmp = kcmp.shape[2]
    n_q = seq // Q_BLOCK
    key_tile = vst.shape[-1]
    n_sel = min(SLC_TOPK, seq // SLC_LEN)
    kern = functools.partial(_nsa_attn_kernel, seq=seq, key_tile=key_tile, n_sel=n_sel)
    full = lambda *dims: pl.BlockSpec((None, None) + dims, lambda b, g, i: (b, g) + (0,) * len(dims))
    qspec = pl.BlockSpec((None, None, None, LANES, GROUP_HEADS * Q_BLOCK), lambda b, g, i: (b, g, i, 0, 0))
    return pl.pallas_call(
        kern,
        grid=(bsz, NSA_KV_GROUPS, n_q),
        in_specs=[qspec, qspec, full(n_cmp, LANES), full(LANES, n_cmp),
                  full(seq, 2 * LANES), full(seq // key_tile, LANES, key_tile),
                  full(seq, LANES), full(n_q, LANES, Q_BLOCK),
                  pl.BlockSpec((Q_BLOCK, LANES), lambda b, g, i: (b * n_q + i, COL_GATE // LANES + g)),
                  pl.BlockSpec((MAX_SLC_BLOCKS, n_cmp), lambda b, g, i: (0, 0))],
        out_specs=pl.BlockSpec((Q_BLOCK, GROUP_HEADS * HEAD_DIM), lambda b, g, i: (b * n_q + i, g)),
        out_shape=jax.ShapeDtypeStruct((bsz * seq, NSA_WIDTH), F32),
        scratch_shapes=[pltpu.VMEM((key_tile, GROUP_HEADS * Q_BLOCK), F32)] * 2,
        compiler_params=_params("parallel", "parallel", "arbitrary"),
    )(qpt, qrt, kcmp, vcmpt, kaug, vst, kw, vwt, za, ovt)


def _split3(x):
    hi = x.astype(BF16)
    r1 = x - hi.astype(F32)
    mid = r1.astype(BF16)
    return hi, mid, (r1 - mid.astype(F32)).astype(BF16)


def _rwkv_token_terms(z, prev_row, mu, w0, a0, k_k, k_a, r_k, ww2, wa2, wg2, seg):
    row = lax.broadcasted_iota(jnp.int32, z.shape, 0)
    prev = jnp.where(row == 0, prev_row, pltpu.roll(z, 1, 0))
    zs = z + (prev - z) * mu
    w = RWKV_WIDTH
    r, k, v, lo = zs[:, 0:w], zs[:, w:2 * w], zs[:, 2 * w:3 * w], zs[:, 3 * w:]
    wl = w0 + _dot(jnp.tanh(lo), ww2)
    w_log = -(jnp.maximum(-wl, 0.0) + jnp.log1p(jnp.exp(-jnp.abs(wl)))) - 0.5
    a = _sigmoid(a0 + _dot(lo, wa2))
    lw = -jnp.exp(w_log)
    g = _dot(_sigmoid(lo), wg2)
    k2 = k * (1.0 + (a - 1.0) * k_a)
    kkraw = k * k_k
    kk = kkraw * lax.rsqrt(jnp.maximum(_dot(kkraw * kkraw, seg), 1e-24))
    b = kk * a
    bonus = _dot(r * k2 * r_k, seg) * v
    step, c = z.shape[0], RWKV_CHUNK
    shift = int(math.log2(c))
    ri = lax.broadcasted_iota(jnp.int32, (step, step), 0)
    ci = lax.broadcasted_iota(jnp.int32, (step, step), 1)
    tri = jnp.where((ri >= ci) & (jnp.right_shift(ri, shift) == jnp.right_shift(ci, shift)), 1.0, 0.0).astype(BF16)
    cum = sum(jnp.dot(tri, piece, preferred_element_type=F32) for piece in _split3(lw))
    tot = jnp.concatenate([jnp.broadcast_to(cum[j * c + c - 1:(j + 1) * c, :], (c, w))
                           for j in range(step // c)], axis=0)
    e_neg = jnp.exp(-cum)
    e_rem = jnp.exp(tot - cum)
    scaled = (r * jnp.exp(cum), k2 * e_neg, b * e_neg, kk * jnp.exp(cum - lw), k2 * e_rem, b * e_rem,
              v, jnp.exp(tot))
    return scaled, bonus, g


RWKV_GROUP = LANES * 2 // RWKV_HEAD


def _group_tiles(x):
    c, gw = RWKV_CHUNK, RWKV_GROUP * RWKV_HEAD
    x3 = x.reshape(x.shape[0] // c, c, x.shape[1])
    return jnp.concatenate([x3[:, :, g * gw:(g + 1) * gw] for g in range(x.shape[1] // gw)], axis=0)


def _ungroup_tiles(t, n_chunks):
    c, gw = RWKV_CHUNK, t.shape[2]
    return jnp.concatenate([t[g * n_chunks:(g + 1) * n_chunks].reshape(n_chunks * c, gw)
                            for g in range(t.shape[0] // n_chunks)], axis=1)


def _block_diag(y):
    y = y.astype(BF16)
    t = jnp.concatenate([y] * (y.shape[2] // y.shape[1]), axis=1)
    shift = int(math.log2(y.shape[1]))
    same = (jnp.right_shift(lax.broadcasted_iota(jnp.int32, t.shape, 1), shift)
            == jnp.right_shift(lax.broadcasted_iota(jnp.int32, t.shape, 2), shift))
    return jnp.where(same, t, jnp.zeros_like(t))


def _head_diag(p):
    n = RWKV_HEAD
    head = jnp.right_shift(lax.broadcasted_iota(jnp.int32, (p.shape[0], n, p.shape[2]), 2), int(math.log2(n)))
    out = p[:, 0:n]
    for h in range(1, p.shape[1] // n):
        out = jnp.where(head == h, p[:, h * n:(h + 1) * n], out)
    return out


def _bdot(a, b, dims):
    return lax.dot_general(a.astype(BF16), b.astype(BF16), dims, preferred_element_type=F32)


_B_NN = (((2,), (1,)), ((0,), (0,)))
_B_NT = (((2,), (2,)), ((0,), (0,)))
_B_TN = (((1,), (1,)), ((0,), (0,)))


def _rwkv_chunk_terms(r_t, k_t, b_t, k_a, k_h, b_h, v, gamma):
    nb, c, gw = r_t.shape
    ri = lax.broadcasted_iota(jnp.int32, (nb, c, gw), 1)
    ci = lax.broadcasted_iota(jnp.int32, (nb, c, gw), 2) & (c - 1)
    incl, strict, eye = ri >= ci, ri > ci, ri == ci
    lhs = jnp.concatenate([r_t, k_a], axis=1).astype(BF16)
    to_k = _bdot(lhs, _block_diag(k_t), _B_NT)
    to_b = _bdot(lhs, _block_diag(b_t), _B_NT)
    a_rk = jnp.where(incl, to_k[:, 0:c], 0.0)
    a_kk = jnp.where(strict, to_k[:, c:], 0.0)
    a_rb = jnp.where(incl, to_b[:, 0:c], 0.0).astype(BF16)
    a_kb = jnp.where(strict, to_b[:, c:], 0.0)
    p = -a_kb
    t_inv = jnp.where(eye, 1.0, 0.0) + p
    p_bd = _block_diag(p)
    for _ in range(int(math.log2(c)) - 1):
        p = _bdot(p, p_bd, _B_NN)
        p_bd = _block_diag(p)
        t_inv = t_inv + _bdot(t_inv, p_bd, _B_NN)
    t_inv, v_bd = t_inv.astype(BF16), _block_diag(v)
    w_k = _bdot(t_inv, _block_diag(k_a), _B_NN).astype(BF16)
    u_v = _bdot(t_inv, _block_diag(_bdot(a_kk, v_bd, _B_NN)), _B_NN).astype(BF16)
    r_q = r_t - _bdot(a_rb, _block_diag(w_k), _B_NN)
    y_v = _bdot(a_rk, v_bd, _B_NN) - _bdot(a_rb, _block_diag(u_v), _B_NN)
    v = v.astype(BF16)
    rhs = jnp.concatenate([jnp.concatenate([v, jnp.zeros_like(v)], axis=2),
                           jnp.concatenate([-u_v, w_k], axis=2)], axis=1)
    both = _bdot(jnp.concatenate([k_h, b_h], axis=1), rhs, _B_TN)
    g = _head_diag(both[:, :, 0:gw])
    m = jnp.where(eye, gamma, 0.0) - _head_diag(both[:, :, gw:])
    return r_q, y_v, m, g


def _rwkv_kernel(zb_ref, mu_ref, w0_ref, a0_ref, kk_ref, ka_ref, rk_ref, ww2_ref, wa2_ref, wg2_ref, seg_ref,
                 lnw_ref, lnb_ref, o_ref, h_ref, prev_ref):
    @pl.when(pl.program_id(1) == 0)
    def _():
        h_ref[...] = jnp.zeros_like(h_ref)
        prev_ref[...] = jnp.zeros_like(prev_ref)

    z = zb_ref[...]
    step = z.shape[0]
    seg = seg_ref[...]
    scaled, bonus, gate = _rwkv_token_terms(
        z, prev_ref[...], mu_ref[...], w0_ref[...], a0_ref[...], kk_ref[...], ka_ref[...], rk_ref[...],
        ww2_ref[...], wa2_ref[...], wg2_ref[...], seg)
    prev_ref[...] = z[step - 1:step, :]
    c = RWKV_CHUNK
    n_chunks = step // c
    groups = h_ref.shape[0]
    terms = _rwkv_chunk_terms(*(_group_tiles(x) for x in scaled))
    r_q, y_v, m, g = (x.reshape(groups, n_chunks, c, x.shape[2]) for x in terms)
    h = h_ref[...]
    ys = []
    for j in range(n_chunks):
        h_bd = _block_diag(h)
        ys.append(_bdot(r_q[:, j], h_bd, _B_NN) + y_v[:, j])
        h = _bdot(m[:, j], h_bd, _B_NN) + g[:, j]
    h_ref[...] = h
    y = _ungroup_tiles(jnp.stack(ys, axis=1).reshape(groups * n_chunks, c, -1), n_chunks)
    inv_n = 1.0 / RWKV_HEAD
    mean = _dot(y, seg) * inv_n
    d = y - mean
    var = _dot(d * d, seg) * inv_n
    yn = d * lax.rsqrt(var + LNX_EPS) * lnw_ref[...] + lnb_ref[...]
    o_ref[...] = (yn + bonus) * gate


def _rwkv(zb, mu, w0, a0, k_k, k_a, r_k, ww2, wa2, wg2, seg, lnx_w, lnx_b, bsz, seq):
    step = min(RWKV_STEP, seq)
    per_seq = seq // step
    const = lambda *shape: pl.BlockSpec(shape, lambda b, i: (0,) * len(shape))
    row = lambda n: const(1, n)
    mat = const(RWKV_LORA_PAD, RWKV_WIDTH)
    return pl.pallas_call(
        _rwkv_kernel,
        grid=(bsz, per_seq),
        in_specs=[pl.BlockSpec((step, RWKV_COLS), lambda b, i: (b * per_seq + i, 0)),
                  row(RWKV_COLS)] + [row(RWKV_WIDTH)] * 5 + [mat, mat, mat, const(RWKV_WIDTH, RWKV_WIDTH),
                                                              row(RWKV_WIDTH), row(RWKV_WIDTH)],
        out_specs=pl.BlockSpec((step, RWKV_WIDTH), lambda b, i: (b * per_seq + i, 0)),
        out_shape=jax.ShapeDtypeStruct((bsz * seq, RWKV_WIDTH), F32),
        scratch_shapes=[pltpu.VMEM((RWKV_HEADS // RWKV_GROUP, RWKV_HEAD, RWKV_GROUP * RWKV_HEAD), F32),
                        pltpu.VMEM((1, RWKV_COLS), F32)],
        compiler_params=_params("parallel", "arbitrary"),
    )(zb, mu, w0, a0, k_k, k_a, r_k, ww2, wa2, wg2, seg, lnx_w, lnx_b)


def _merge_ffn_kernel(x_ref, oa_ref, ob_ref, zc_ref, wa_ref, wb_ref, wo_ref, g1_ref,
                      g2pre_ref, g2post_ref, wg_ref, wu_ref, wd_ref, o_ref, *, ff_chunk):
    gate_a = _sigmoid(zc_ref[:, 0:D_MODEL])
    gate_b = _sigmoid(zc_ref[:, D_MODEL:2 * D_MODEL])
    mixed = gate_a * _dot(oa_ref[...], wa_ref[...]) + gate_b * _dot(ob_ref[...], wb_ref[...])
    x = x_ref[...] + _rms_norm(_dot(mixed, wo_ref[...]), g1_ref[...])
    h = _rms_norm(x, g2pre_ref[...]).astype(BF16)
    acc = jnp.zeros(x.shape, F32)
    for c in range(0, D_FF, ff_chunk):
        gt = jnp.dot(h, wg_ref[:, c:c + ff_chunk], preferred_element_type=F32)
        up = jnp.dot(h, wu_ref[:, c:c + ff_chunk], preferred_element_type=F32)
        acc = acc + _dot(gt * _sigmoid(gt) * up, wd_ref[c:c + ff_chunk, :])
    o_ref[...] = x + _rms_norm(acc, g2post_ref[...])


def _merge_ffn(x2, oa, ob, zc, wa, wb, wo, g1, g2pre, g2post, wg, wu, wd, tm=DENSE_TILE, ff_chunk=256):
    t = x2.shape[0]
    tile = lambda n: pl.BlockSpec((tm, n), lambda i: (i, 0))
    const = _resident
    return pl.pallas_call(
        functools.partial(_merge_ffn_kernel, ff_chunk=ff_chunk),
        grid=(t // tm,),
        in_specs=[tile(D_MODEL), tile(NSA_WIDTH), tile(RWKV_WIDTH), tile(MERGE_COLS),
                  const(NSA_WIDTH, D_MODEL), const(RWKV_WIDTH, D_MODEL), const(D_MODEL, D_MODEL),
                  const(1, D_MODEL), const(1, D_MODEL), const(1, D_MODEL),
                  const(D_MODEL, D_FF), const(D_MODEL, D_FF), const(D_FF, D_MODEL)],
        out_specs=tile(D_MODEL),
        out_shape=jax.ShapeDtypeStruct((t, D_MODEL), F32),
        compiler_params=_params("parallel"),
    )(x2, oa, ob, zc, wa, wb, wo, g1, g2pre, g2post, wg, wu, wd)


def _pad_cols(a, n):
    return jnp.pad(a, ((0, 0), (0, n - a.shape[1])))


def _pack_w_in(w_in):
    g0 = NSA_WIDTH + 6 * KV_WIDTH
    r0 = g0 + 3 * NSA_HEADS
    l0 = r0 + 3 * RWKV_WIDTH
    m0 = l0 + RWKV_LORA
    gates = w_in[:, g0:r0].reshape(-1, 3, NSA_KV_GROUPS, GROUP_HEADS).transpose(0, 2, 1, 3)
    gates = jnp.pad(gates.reshape(-1, NSA_KV_GROUPS, 3 * GROUP_HEADS),
                    ((0, 0), (0, 0), (0, LANES - 3 * GROUP_HEADS))).reshape(-1, NSA_KV_GROUPS * LANES)
    return jnp.concatenate([w_in[:, :g0], gates,
                            w_in[:, r0:l0], _pad_cols(w_in[:, l0:m0], RWKV_LORA_PAD),
                            w_in[:, m0:]], axis=1).astype(BF16)


def _rope_tables(seq):
    inv = 1.0 / (ROPE_THETA ** (jnp.arange(0, HEAD_DIM, 2, dtype=F32) / HEAD_DIM))
    ang = jnp.arange(seq, dtype=F32)[:, None] * inv[None, :]
    cos, sin = jnp.cos(ang), jnp.sin(ang)
    reps = LANES // HEAD_DIM
    return (jnp.concatenate([cos, cos] * reps, axis=1), jnp.concatenate([-sin, sin] * reps, axis=1))


def _layer(x, norm1_pre, norm1_post, w_in,
           cmp_pe_k, cmp_w1_k, cmp_b1_k, cmp_w2_k, cmp_pe_v, cmp_w1_v, cmp_b1_v, cmp_w2_v,
           mu_r, mu_k, mu_v, mu_w, mu_a, mu_g, w0, w_w2, a0, w_a2, w_g2,
           k_k, k_a, r_k, lnx_w, lnx_b, w_branch_a, w_branch_b, w_out,
           norm2_pre, norm2_post, w_gate, w_up, w_down):
    bsz, seq, _ = x.shape
    assert seq % RWKV_STEP == 0 or seq < RWKV_STEP
    assert seq // SLC_LEN <= MAX_SLC_BLOCKS and seq >= WINDOW + Q_BLOCK
    t = bsz * seq
    x2 = x.reshape(t, D_MODEL)
    row = lambda a: a.reshape(1, -1)

    za, zb, zc = _in_proj(x2, row(norm1_pre), _pack_w_in(w_in))

    cos_t, sin_t = _rope_tables(seq)
    qpt, qrt, kaug, kwin, vselt, vwint = _nsa_prep(za, cos_t, sin_t, bsz, seq)
    g_, hd = NSA_KV_GROUPS, HEAD_DIM

    def both_groups(w, cols):
        z = jnp.zeros(w.shape[:-2] + (g_ * w.shape[-2], g_ * cols), w.dtype)
        for g in range(g_):
            z = z.at[..., g * w.shape[-2]:(g + 1) * w.shape[-2], g * cols:g * cols + w.shape[-1]].set(w)
        return z

    w1 = jnp.stack([cmp_w1_k, cmp_w1_v]).reshape(2, CMP_LEN, hd, CMP_HIDDEN)
    cmp_out, cmp_out_t = _compress(za,
                        jnp.tile(jnp.stack([cmp_pe_k, cmp_pe_v]), (1, 1, g_)),
                        both_groups(w1, CMP_HIDDEN).astype(BF16),
                        jnp.tile(jnp.stack([cmp_b1_k, cmp_b1_v]).reshape(2, 1, CMP_HIDDEN), (1, 1, g_)),
                        both_groups(jnp.stack([cmp_w2_k, cmp_w2_v]), LANES).astype(BF16),
                        bsz, seq)

    n_half = seq // CMP_STRIDE
    n_slc = seq // SLC_LEN
    cmp_start = jnp.arange(n_half) * CMP_STRIDE
    slc_start = jnp.arange(MAX_SLC_BLOCKS) * SLC_LEN
    ovt = ((cmp_start[None, :] < slc_start[:, None] + SLC_LEN)
           & (cmp_start[None, :] + CMP_LEN - 1 >= slc_start[:, None])
           & (jnp.arange(MAX_SLC_BLOCKS)[:, None] < n_slc)).astype(BF16)
    o_a = _nsa_attn(qpt, qrt, cmp_out[0], cmp_out_t[1], kaug, vselt, kwin, vwint, za, ovt, bsz, seq)

    mu = _pad_cols(jnp.concatenate([mu_r, mu_k, mu_v, mu_w, mu_a, mu_g]).reshape(1, -1), RWKV_COLS)
    lora = jnp.zeros((3, RWKV_LORA_PAD, RWKV_WIDTH), F32)
    lora = lora.at[0, 0:DECAY_LORA].set(w_w2)
    lora = lora.at[1, DECAY_LORA:DECAY_LORA + AAA_LORA].set(w_a2)
    lora = lora.at[2, DECAY_LORA + AAA_LORA:RWKV_LORA].set(w_g2).astype(BF16)
    lanes = jnp.arange(RWKV_WIDTH) // RWKV_HEAD
    seg = (lanes[:, None] == lanes[None, :]).astype(BF16)
    o_b = _rwkv(zb, mu, row(w0), row(a0), row(k_k), row(k_a), row(r_k), lora[0], lora[1], lora[2], seg,
                row(lnx_w), row(lnx_b), bsz, seq)

    out = _merge_ffn(x2, o_a, o_b, zc, w_branch_a.astype(BF16), w_branch_b.astype(BF16), w_out.astype(BF16),
                     row(norm1_post), row(norm2_pre), row(norm2_post),
                     w_gate.astype(BF16), w_up.astype(BF16), w_down.astype(BF16))
    return out.reshape(bsz, seq, D_MODEL)


def kernel(x, norm1_pre, norm1_post, w_in, cmp_pe_k, cmp_w1_k, cmp_b1_k, cmp_w2_k, cmp_pe_v, cmp_w1_v, cmp_b1_v, cmp_w2_v, mu_r, mu_k, mu_v, mu_w, mu_a, mu_g, w0, w_w2, a0, w_a2, w_g2, k_k, k_a, r_k, lnx_w, lnx_b, w_branch_a, w_branch_b, w_out, norm2_pre, norm2_post, w_gate, w_up, w_down):
    params = (norm1_pre, norm1_post, w_in, cmp_pe_k, cmp_w1_k, cmp_b1_k, cmp_w2_k, cmp_pe_v, cmp_w1_v,
              cmp_b1_v, cmp_w2_v, mu_r, mu_k, mu_v, mu_w, mu_a, mu_g, w0, w_w2, a0, w_a2, w_g2,
              k_k, k_a, r_k, lnx_w, lnx_b, w_branch_a, w_branch_b, w_out,
              norm2_pre, norm2_post, w_gate, w_up, w_down)
    for layer in range(norm1_pre.shape[0]):
        x = _layer(x, *[p[layer] for p in params])
    return x
```

```python
import functools
import math

import jax
import jax.numpy as jnp
from jax import lax
from jax.experimental import pallas as pl
from jax.experimental.pallas import tpu as pltpu

F32 = jnp.float32
BF16 = jnp.bfloat16

D_MODEL = 1024
NSA_HEADS = 8
NSA_KV_GROUPS = 2
GROUP_HEADS = NSA_HEADS // NSA_KV_GROUPS
HEAD_DIM = 64
NSA_WIDTH = NSA_HEADS * HEAD_DIM
KV_WIDTH = NSA_KV_GROUPS * HEAD_DIM
CMP_LEN = 32
CMP_STRIDE = 16
CMP_HIDDEN = 256
SLC_LEN = 64
SLC_TOPK = 16
WINDOW = 512
Q_BLOCK = 128
ROPE_THETA = 10000.0
RWKV_HEADS = 8
RWKV_HEAD = 64
RWKV_WIDTH = RWKV_HEADS * RWKV_HEAD
DECAY_LORA = 32
AAA_LORA = 32
GATE_LORA = 96
LNX_EPS = 64e-5
D_FF = 2816
NORM_EPS = 1e-6
BIG = 1e30

LANES = 128
MAX_SLC_BLOCKS = 128
UNSELECTED_BIAS = -30000.0
REMOVED = -3.0e38

NSA_COLS = NSA_WIDTH + 6 * KV_WIDTH + NSA_KV_GROUPS * LANES
RWKV_LORA = DECAY_LORA + AAA_LORA + GATE_LORA
RWKV_LORA_PAD = 256
RWKV_COLS = 3 * RWKV_WIDTH + RWKV_LORA_PAD
MERGE_COLS = 2 * D_MODEL
COL_KC, COL_VC, COL_KS, COL_VS, COL_KW, COL_VW = (NSA_WIDTH + i * KV_WIDTH for i in range(6))
COL_GATE = NSA_WIDTH + 6 * KV_WIDTH

RWKV_CHUNK = 64
RWKV_STEP = 256
VMEM_LIMIT = 56 * 1024 * 1024
DENSE_TILE = 512

_NT = (((1,), (1,)), ((), ()))
_TN = (((0,), (0,)), ((), ()))


def _params(*sem):
    return pltpu.CompilerParams(dimension_semantics=sem, vmem_limit_bytes=VMEM_LIMIT)


def _sigmoid(x):
    return 1.0 / (1.0 + jnp.exp(-x))


def _rms_norm(x, g):
    return x * lax.rsqrt(jnp.mean(x * x, axis=-1, keepdims=True) + NORM_EPS) * g


def _dot(a, b):
    return jnp.dot(a.astype(BF16), b.astype(BF16), preferred_element_type=F32)


def _dot_nt(a, b):
    return lax.dot_general(a.astype(BF16), b.astype(BF16), _NT, preferred_element_type=F32)


def _dot_tn(a, b):
    return lax.dot_general(a.astype(BF16), b.astype(BF16), _TN, preferred_element_type=F32)


def _resident(*shape):
    return pl.BlockSpec(shape, lambda i: (0,) * len(shape), pipeline_mode=pl.Buffered(1))


def _col_chunks(width, step=512):
    return [(c, min(step, width - c)) for c in range(0, width, step)]


def _in_proj_kernel(x_ref, g_ref, w_ref, cos_ref, sin_ref, za_ref, zb_ref, zc_ref,
                    qpt_ref, qrt_ref, kaug_ref, kwin_ref, vselt_ref, vwint_ref, *, tiles_per_seq):
    h = _rms_norm(x_ref[...], g_ref[...]).astype(BF16)
    base = 0
    for o_ref in (za_ref, zb_ref, zc_ref):
        for c, n in _col_chunks(o_ref.shape[1]):
            o_ref[:, c:c + n] = jnp.dot(h, w_ref[:, base + c:base + c + n], preferred_element_type=F32)
        base += o_ref.shape[1]
    _nsa_prep_tile(za_ref, cos_ref[...], sin_ref[...], pl.program_id(0) % tiles_per_seq,
                   qpt_ref, qrt_ref, kaug_ref, kwin_ref, vselt_ref, vwint_ref)


def _in_proj(x2, g, w, cos_t, sin_t, bsz, seq):
    tm = min(DENSE_TILE, seq)
    t = x2.shape[0]
    ncols = w.shape[1]
    per_seq = seq // tm
    n_qb = tm // Q_BLOCK
    n_q = seq // Q_BLOCK
    g_ = NSA_KV_GROUPS
    cols = GROUP_HEADS * Q_BLOCK
    rows = lambda n: pl.BlockSpec((tm, n), lambda i: (i, 0))
    table = pl.BlockSpec((tm, LANES), lambda i: (i % per_seq, 0))
    qt_spec = pl.BlockSpec((None, g_, n_qb, LANES, cols), lambda i: (i // per_seq, 0, i % per_seq, 0, 0))
    qt_shape = jax.ShapeDtypeStruct((bsz, g_, n_q, LANES, cols), BF16)
    kv = lambda width: pl.BlockSpec((None, g_, tm, width), lambda i: (i // per_seq, 0, i % per_seq, 0))
    kvs = lambda width: jax.ShapeDtypeStruct((bsz, g_, seq, width), BF16)
    return pl.pallas_call(
        functools.partial(_in_proj_kernel, tiles_per_seq=per_seq),
        grid=(t // tm,),
        in_specs=[rows(D_MODEL), _resident(1, D_MODEL), _resident(D_MODEL, ncols), table, table],
        out_specs=[rows(NSA_COLS), rows(RWKV_COLS), rows(MERGE_COLS), qt_spec, qt_spec, kv(2 * LANES), kv(LANES),
                   pl.BlockSpec((None, g_, None, LANES, tm), lambda i: (i // per_seq, 0, i % per_seq, 0, 0)),
                   pl.BlockSpec((None, g_, n_qb, LANES, Q_BLOCK), lambda i: (i // per_seq, 0, i % per_seq, 0, 0))],
        out_shape=[jax.ShapeDtypeStruct((t, NSA_COLS), F32),
                   jax.ShapeDtypeStruct((t, RWKV_COLS), F32),
                   jax.ShapeDtypeStruct((t, MERGE_COLS), F32),
                   qt_shape, qt_shape, kvs(2 * LANES), kvs(LANES),
                   jax.ShapeDtypeStruct((bsz, g_, per_seq, LANES, tm), BF16),
                   jax.ShapeDtypeStruct((bsz, g_, n_q, LANES, Q_BLOCK), BF16)],
        compiler_params=_params("parallel"),
    )(x2, g, w, cos_t, sin_t)


def _rope(x, cos, sin_signed):
    w = x.shape[1]
    lane = lax.broadcasted_iota(jnp.int32, x.shape, 1)
    rot = jnp.where((lane & (HEAD_DIM - 1)) < HEAD_DIM // 2,
                    pltpu.roll(x, w - HEAD_DIM // 2, 1), pltpu.roll(x, HEAD_DIM // 2, 1))
    return x * cos + rot * sin_signed


def _pad_heads(x):
    low = lax.broadcasted_iota(jnp.int32, x.shape, 1) < HEAD_DIM
    return jnp.where(low, x, 0.0), jnp.where(low, pltpu.roll(x, HEAD_DIM, 1), 0.0)


def _nsa_prep_tile(za_ref, cos, sin, tile_in_seq, qpt_ref, qrt_ref, kaug_ref, kwin_ref, vselt_ref, vwint_ref):
    qscale = HEAD_DIM ** -0.5 * math.log2(math.e)
    tm = cos.shape[0]
    n_qb = tm // Q_BLOCK
    for pair in range(NSA_HEADS // 2):
        q = za_ref[:, pair * LANES:(pair + 1) * LANES]
        for o_ref, val in ((qpt_ref, q * qscale), (qrt_ref, _rope(q, cos, sin) * qscale)):
            for head, padded in zip((2 * pair, 2 * pair + 1), _pad_heads(val)):
                g, r = divmod(head, GROUP_HEADS)
                for qb in range(n_qb):
                    o_ref[g, qb, :, r * Q_BLOCK:(r + 1) * Q_BLOCK] = jnp.transpose(
                        padded[qb * Q_BLOCK:(qb + 1) * Q_BLOCK]).astype(BF16)
    lane = lax.broadcasted_iota(jnp.int32, (tm, LANES), 1)
    pos = tile_in_seq * tm + lax.broadcasted_iota(jnp.int32, (tm, LANES), 0)
    onehot = jnp.where(jnp.right_shift(pos, int(math.log2(SLC_LEN))) == lane, 1.0, 0.0).astype(BF16)
    ks = _pad_heads(_rope(za_ref[:, COL_KS:COL_KS + LANES], cos, sin))
    kw = _pad_heads(_rope(za_ref[:, COL_KW:COL_KW + LANES], cos, sin))
    vs = _pad_heads(za_ref[:, COL_VS:COL_VS + LANES])
    vw = _pad_heads(za_ref[:, COL_VW:COL_VW + LANES])
    ones_row = lane == HEAD_DIM
    for g in range(NSA_KV_GROUPS):
        kaug_ref[g, :, 0:LANES] = onehot
        kaug_ref[g, :, LANES:2 * LANES] = ks[g].astype(BF16)
        kwin_ref[g] = kw[g].astype(BF16)
        vselt_ref[g] = jnp.transpose(jnp.where(ones_row, 1.0, vs[g])).astype(BF16)
        vwt = jnp.transpose(jnp.where(ones_row, 1.0, vw[g])).astype(BF16)
        for qb in range(n_qb):
            vwint_ref[g, qb] = vwt[:, qb * Q_BLOCK:(qb + 1) * Q_BLOCK]


def _compress_kernel(z_ref, pe_ref, w1_ref, b1_ref, w2_ref, o_ref, ot_ref):
    n_half = z_ref.shape[0] // CMP_STRIDE
    y_lo = jnp.zeros((n_half, NSA_KV_GROUPS * CMP_HIDDEN), F32)
    y_hi = jnp.zeros((n_half, NSA_KV_GROUPS * CMP_HIDDEN), F32)
    for l in range(CMP_STRIDE):
        rows = z_ref[pl.ds(l, n_half, stride=CMP_STRIDE), :]
        y_lo = y_lo + _dot(rows + pe_ref[l:l + 1, :], w1_ref[l])
        y_hi = y_hi + _dot(rows + pe_ref[CMP_STRIDE + l:CMP_STRIDE + l + 1, :], w1_ref[CMP_STRIDE + l])
    pre = y_lo + pltpu.roll(y_hi, n_half - 1, 0) + b1_ref[...]
    h = 0.5 * pre * (1.0 + jnp.tanh(math.sqrt(2.0 / math.pi) * (pre + 0.044715 * (pre * pre * pre))))
    out = _dot(h, w2_ref[...])
    for g in range(NSA_KV_GROUPS):
        tile = out[:, g * LANES:(g + 1) * LANES]
        o_ref[g] = tile.astype(o_ref.dtype)
        ot_ref[g] = jnp.transpose(tile).astype(ot_ref.dtype)


def _compress(za, pe, w1, b1, w2, bsz, seq):
    n_half = seq // CMP_STRIDE
    hid = NSA_KV_GROUPS * CMP_HIDDEN
    return pl.pallas_call(
        _compress_kernel,
        grid=(2, bsz),
        in_specs=[pl.BlockSpec((seq, LANES), lambda s, b: (b, COL_KC // LANES + s)),
                  pl.BlockSpec((None, CMP_LEN, LANES), lambda s, b: (s, 0, 0)),
                  pl.BlockSpec((None, CMP_LEN, LANES, hid), lambda s, b: (s, 0, 0, 0)),
                  pl.BlockSpec((None, 1, hid), lambda s, b: (s, 0, 0)),
                  pl.BlockSpec((None, hid, NSA_KV_GROUPS * LANES), lambda s, b: (s, 0, 0))],
        out_specs=[pl.BlockSpec((None, None, NSA_KV_GROUPS, n_half, LANES), lambda s, b: (s, b, 0, 0, 0)),
                   pl.BlockSpec((None, None, NSA_KV_GROUPS, LANES, n_half), lambda s, b: (s, b, 0, 0, 0))],
        out_shape=[jax.ShapeDtypeStruct((2, bsz, NSA_KV_GROUPS, n_half, LANES), BF16),
                   jax.ShapeDtypeStruct((2, bsz, NSA_KV_GROUPS, LANES, n_half), BF16)],
        compiler_params=_params("parallel", "parallel"),
    )(za, pe, w1, b1, w2)


def _nsa_attn_kernel(qpt_ref, qrt_ref, kc_ref, vct_ref, kaug_ref, vst_ref, kw_ref, vwt_ref, gate_ref, ovt_ref,
                     o_ref, sa_ref, sb_ref, *, seq, key_tile, n_sel, variants):
    cols = GROUP_HEADS * Q_BLOCK
    q0 = pl.program_id(2) * Q_BLOCK
    t_row = q0 + (lax.broadcasted_iota(jnp.int32, (1, cols), 1) & (Q_BLOCK - 1))
    t_q = q0 + lax.broadcasted_iota(jnp.int32, (1, Q_BLOCK), 1)

    def all_heads(mask_bias):
        return jnp.concatenate([mask_bias] * GROUP_HEADS, axis=1)

    qrt = qrt_ref[...]
    n_chunks = WINDOW // Q_BLOCK + 1
    c0 = jnp.maximum(pl.program_id(2) - WINDOW // Q_BLOCK, 0)
    kstart = pl.multiple_of(c0 * Q_BLOCK, Q_BLOCK)

    def front(rows_c, rows_k):
        s_c = jnp.dot(kc_ref[0:rows_c, :], qpt_ref[...], preferred_element_type=F32)
        s_w = jnp.dot(kw_ref[pl.ds(kstart, n_chunks * Q_BLOCK), :], qrt, preferred_element_type=F32)

        cmp_end = lax.broadcasted_iota(jnp.int32, (rows_c, 1), 0) * CMP_STRIDE + (CMP_LEN - 1)
        s_c = s_c + all_heads(jnp.where(cmp_end <= t_q, 0.0, -BIG))
        e_c = jnp.exp2(s_c - jnp.max(s_c, axis=0, keepdims=True))
        p_c = e_c * jnp.where(t_row >= CMP_LEN - 1, 1.0 / jnp.sum(e_c, axis=0, keepdims=True), 0.0)
        o_c = jnp.dot(vct_ref[:, 0:rows_c], p_c.astype(BF16), preferred_element_type=F32)

        p_sum = p_c[:, 0:Q_BLOCK]
        for r in range(1, GROUP_HEADS):
            p_sum = p_sum + p_c[:, r * Q_BLOCK:(r + 1) * Q_BLOCK]
        p_hi = p_sum.astype(BF16)
        p_lo = (p_sum - p_hi.astype(F32)).astype(BF16)
        ovt = ovt_ref[0:rows_k, 0:rows_c]
        imp = (jnp.dot(ovt, p_hi, preferred_element_type=F32)
               + jnp.dot(ovt, p_lo, preferred_element_type=F32))
        blk = lax.broadcasted_iota(jnp.int32, imp.shape, 0)
        cur = jnp.right_shift(q0 + lax.broadcasted_iota(jnp.int32, imp.shape, 1), int(math.log2(SLC_LEN)))
        imp = jnp.where(blk > cur, -BIG, imp)
        imp = jnp.where((blk == 0) | (blk == cur), BIG, imp)

        age = t_q - (kstart + lax.broadcasted_iota(jnp.int32, (n_chunks * Q_BLOCK, 1), 0))
        s_w = s_w + all_heads(jnp.where((age >= 0) & (age < WINDOW), 0.0, -BIG))
        p_w = jnp.exp2(s_w - jnp.max(s_w, axis=0, keepdims=True)).astype(BF16)
        acc_w = jnp.dot(vwt_ref[c0], p_w[0:Q_BLOCK], preferred_element_type=F32)
        for c in range(1, n_chunks):
            acc_w = acc_w + jnp.dot(vwt_ref[c0 + c], p_w[c * Q_BLOCK:(c + 1) * Q_BLOCK],
                                    preferred_element_type=F32)
        o_w = acc_w * (1.0 / acc_w[HEAD_DIM:HEAD_DIM + 1, :])

        blk_f = blk.astype(F32)
        bias_t = jnp.full(imp.shape, UNSELECTED_BIAS, F32)
        for _ in range(n_sel):
            mx = jnp.max(imp, axis=0, keepdims=True)
            first = jnp.min(jnp.where(imp == mx, blk_f, float(MAX_SLC_BLOCKS)), axis=0, keepdims=True)
            hit = blk_f == first
            bias_t = jnp.where(hit, 0.0, bias_t)
            imp = jnp.where(hit, REMOVED, imp)
        if rows_k < MAX_SLC_BLOCKS:
            bias_t = jnp.concatenate(
                [bias_t, jnp.full((MAX_SLC_BLOCKS - rows_k, Q_BLOCK), UNSELECTED_BIAS, F32)], axis=0)
        return o_c, o_w, bias_t

    o_c, o_w, bias_t = lax.switch(
        sum((pl.program_id(2) >= first).astype(jnp.int32) for first, _, _ in variants[1:]),
        [functools.partial(front, rows_c, rows_k) for _, rows_c, rows_k in variants])

    q_aug = jnp.concatenate([jnp.concatenate([bias_t.astype(BF16)] * GROUP_HEADS, axis=1), qrt], axis=0)
    k_iota = lax.broadcasted_iota(jnp.int32, (key_tile, 1), 0)

    def scores(kt):
        return jnp.dot(kaug_ref[pl.ds(pl.multiple_of(kt * key_tile, key_tile), key_tile), :], q_aug,
                       preferred_element_type=F32)

    def tile_update(kt, s, m, acc):
        m_new = jnp.maximum(m, jnp.max(s, axis=0, keepdims=True))
        p = jnp.exp2(s - m_new)
        acc = jnp.exp2(m - m_new) * acc + jnp.dot(vst_ref[kt], p.astype(BF16), preferred_element_type=F32)
        return m_new, acc

    def causal(kt, s):
        return s + all_heads(jnp.where(kt * key_tile + k_iota <= t_q, 0.0, -BIG))

    n_last = q0 // key_tile
    n_pairs = n_last // 2
    sa_ref[...] = scores(0)

    def pair(j, carry):
        m, acc = carry
        sb_ref[...] = scores(2 * j + 1)
        m, acc = tile_update(2 * j, sa_ref[...], m, acc)
        sa_ref[...] = scores(2 * j + 2)
        return tile_update(2 * j + 1, sb_ref[...], m, acc)

    m_s, acc_s = lax.fori_loop(0, n_pairs, pair,
                               (jnp.full((1, cols), -BIG, F32), jnp.zeros((LANES, cols), F32)))
    odd = n_last > 2 * n_pairs
    sb_ref[...] = scores(n_last)
    m_s, acc_s = tile_update(2 * n_pairs, causal(2 * n_pairs, sa_ref[...]), m_s, acc_s)
    _, acc_s = lax.cond(odd, lambda: tile_update(n_last, causal(n_last, sb_ref[...]), m_s, acc_s),
                        lambda: (m_s, acc_s))
    o_s = acc_s * (1.0 / acc_s[HEAD_DIM:HEAD_DIM + 1, :])

    gate_t = jnp.transpose(_sigmoid(gate_ref[...]))
    low = lax.broadcasted_iota(jnp.int32, (Q_BLOCK, LANES), 1) < HEAD_DIM
    heads = []
    for r in range(GROUP_HEADS):
        sl = slice(r * Q_BLOCK, (r + 1) * Q_BLOCK)
        heads.append(jnp.transpose(
            gate_t[r:r + 1, :] * o_c[:, sl]
            + gate_t[GROUP_HEADS + r:GROUP_HEADS + r + 1, :] * o_s[:, sl]
            + gate_t[2 * GROUP_HEADS + r:2 * GROUP_HEADS + r + 1, :] * o_w[:, sl]))
    for pair in range(GROUP_HEADS // 2):
        o_ref[:, pair * LANES:(pair + 1) * LANES] = jnp.where(
            low, heads[2 * pair], pltpu.roll(heads[2 * pair + 1], HEAD_DIM, 1))


def _nsa_attn(qpt, qrt, kcmp, vcmpt, kaug, vst, kw, vwt, za, ovt, bsz, seq):
    n_cmp = kcmp.shape[2]
    n_q = seq // Q_BLOCK
    key_tile = vst.shape[-1]
    n_sel = min(SLC_TOPK, seq // SLC_LEN)
    variants, first = [], 0
    while first < n_q:
        rows_k = min(max(pl.next_power_of_2(2 * (first + 1)), n_sel), MAX_SLC_BLOCKS)
        last = rows_k // 2 if rows_k < MAX_SLC_BLOCKS else n_q
        step_c = LANES * CMP_STRIDE // Q_BLOCK
        sub = first
        while sub < min(last, n_q):
            nxt = min((sub // step_c + 1) * step_c, last, n_q)
            rows_c = min(-(-nxt * Q_BLOCK // CMP_STRIDE // LANES) * LANES, n_cmp)
            variants.append((sub, rows_c, rows_k))
            sub = nxt
        first = min(last, n_q)
    kern = functools.partial(_nsa_attn_kernel, seq=seq, key_tile=key_tile, n_sel=n_sel, variants=tuple(variants))
    full = lambda *dims: pl.BlockSpec((None, None) + dims, lambda b, g, i: (b, g) + (0,) * len(dims))
    qspec = pl.BlockSpec((None, None, None, LANES, GROUP_HEADS * Q_BLOCK), lambda b, g, i: (b, g, i, 0, 0))
    return pl.pallas_call(
        kern,
        grid=(bsz, NSA_KV_GROUPS, n_q),
        in_specs=[qspec, qspec, full(n_cmp, LANES), full(LANES, n_cmp),
                  full(seq, 2 * LANES), full(seq // key_tile, LANES, key_tile),
                  full(seq, LANES), full(n_q, LANES, Q_BLOCK),
                  pl.BlockSpec((Q_BLOCK, LANES), lambda b, g, i: (b * n_q + i, COL_GATE // LANES + g)),
                  pl.BlockSpec((MAX_SLC_BLOCKS, n_cmp), lambda b, g, i: (0, 0))],
        out_specs=pl.BlockSpec((Q_BLOCK, GROUP_HEADS * HEAD_DIM), lambda b, g, i: (b * n_q + i, g)),
        out_shape=jax.ShapeDtypeStruct((bsz * seq, NSA_WIDTH), F32),
        scratch_shapes=[pltpu.VMEM((key_tile, GROUP_HEADS * Q_BLOCK), F32)] * 2,
        compiler_params=_params("parallel", "parallel", "arbitrary"),
    )(qpt, qrt, kcmp, vcmpt, kaug, vst, kw, vwt, za, ovt)


def _split3(x):
    hi = x.astype(BF16)
    r1 = x - hi.astype(F32)
    mid = r1.astype(BF16)
    return hi, mid, (r1 - mid.astype(F32)).astype(BF16)


def _rwkv_token_terms(z, prev_row, mu, w0, a0, k_k, k_a, r_k, ww2, wa2, wg2, seg):
    row = lax.broadcasted_iota(jnp.int32, z.shape, 0)
    prev = jnp.where(row == 0, prev_row, pltpu.roll(z, 1, 0))
    zs = z + (prev - z) * mu
    w = RWKV_WIDTH
    r, k, v, lo = zs[:, 0:w], zs[:, w:2 * w], zs[:, 2 * w:3 * w], zs[:, 3 * w:]
    wl = w0 + _dot(jnp.tanh(lo), ww2)
    w_log = -(jnp.maximum(-wl, 0.0) + jnp.log1p(jnp.exp(-jnp.abs(wl)))) - 0.5
    a = _sigmoid(a0 + _dot(lo, wa2))
    lw = -jnp.exp(w_log)
    g = _dot(_sigmoid(lo), wg2)
    k2 = k * (1.0 + (a - 1.0) * k_a)
    kkraw = k * k_k
    kk = kkraw * lax.rsqrt(jnp.maximum(_dot(kkraw * kkraw, seg), 1e-24))
    b = kk * a
    bonus = _dot(r * k2 * r_k, seg) * v
    step, c = z.shape[0], RWKV_CHUNK
    shift = int(math.log2(c))
    ri = lax.broadcasted_iota(jnp.int32, (step, step), 0)
    ci = lax.broadcasted_iota(jnp.int32, (step, step), 1)
    tri = jnp.where((ri >= ci) & (jnp.right_shift(ri, shift) == jnp.right_shift(ci, shift)), 1.0, 0.0).astype(BF16)
    cum = sum(jnp.dot(tri, piece, preferred_element_type=F32) for piece in _split3(lw))
    tot = jnp.concatenate([jnp.broadcast_to(cum[j * c + c - 1:(j + 1) * c, :], (c, w))
                           for j in range(step // c)], axis=0)
    e_neg = jnp.exp(-cum)
    e_rem = jnp.exp(tot - cum)
    scaled = (r * jnp.exp(cum), k2 * e_neg, b * e_neg, kk * jnp.exp(cum - lw), k2 * e_rem, b * e_rem,
              v, jnp.exp(tot))
    return scaled, bonus, g


RWKV_GROUP = LANES * 2 // RWKV_HEAD


def _group_tiles(x):
    c, gw = RWKV_CHUNK, RWKV_GROUP * RWKV_HEAD
    x3 = x.reshape(x.shape[0] // c, c, x.shape[1])
    return jnp.concatenate([x3[:, :, g * gw:(g + 1) * gw] for g in range(x.shape[1] // gw)], axis=0)


def _ungroup_tiles(t, n_chunks):
    c, gw = RWKV_CHUNK, t.shape[2]
    return jnp.concatenate([t[g * n_chunks:(g + 1) * n_chunks].reshape(n_chunks * c, gw)
                            for g in range(t.shape[0] // n_chunks)], axis=1)


def _block_diag(y):
    y = y.astype(BF16)
    t = jnp.concatenate([y] * (y.shape[2] // y.shape[1]), axis=1)
    shift = int(math.log2(y.shape[1]))
    same = (jnp.right_shift(lax.broadcasted_iota(jnp.int32, t.shape, 1), shift)
            == jnp.right_shift(lax.broadcasted_iota(jnp.int32, t.shape, 2), shift))
    return jnp.where(same, t, jnp.zeros_like(t))


def _head_diag(p):
    n = RWKV_HEAD
    head = jnp.right_shift(lax.broadcasted_iota(jnp.int32, (p.shape[0], n, p.shape[2]), 2), int(math.log2(n)))
    out = p[:, 0:n]
    for h in range(1, p.shape[1] // n):
        out = jnp.where(head == h, p[:, h * n:(h + 1) * n], out)
    return out


def _bdot(a, b, dims):
    return lax.dot_general(a.astype(BF16), b.astype(BF16), dims, preferred_element_type=F32)


_B_NN = (((2,), (1,)), ((0,), (0,)))
_B_NT = (((2,), (2,)), ((0,), (0,)))
_B_TN = (((1,), (1,)), ((0,), (0,)))


def _rwkv_chunk_terms(r_t, k_t, b_t, k_a, k_h, b_h, v, gamma):
    nb, c, gw = r_t.shape
    ri = lax.broadcasted_iota(jnp.int32, (nb, c, gw), 1)
    ci = lax.broadcasted_iota(jnp.int32, (nb, c, gw), 2) & (c - 1)
    incl, strict, eye = ri >= ci, ri > ci, ri == ci
    lhs = jnp.concatenate([r_t, k_a], axis=1).astype(BF16)
    to_k = _bdot(lhs, _block_diag(k_t), _B_NT)
    to_b = _bdot(lhs, _block_diag(b_t), _B_NT)
    a_rk = jnp.where(incl, to_k[:, 0:c], 0.0)
    a_kk = jnp.where(strict, to_k[:, c:], 0.0)
    a_rb = jnp.where(incl, to_b[:, 0:c], 0.0).astype(BF16)
    a_kb = jnp.where(strict, to_b[:, c:], 0.0)
    p = -a_kb
    t_inv = jnp.where(eye, 1.0, 0.0) + p
    p_bd = _block_diag(p)
    for _ in range(int(math.log2(c)) - 1):
        p = _bdot(p, p_bd, _B_NN)
        p_bd = _block_diag(p)
        t_inv = t_inv + _bdot(t_inv, p_bd, _B_NN)
    t_inv, v_bd = t_inv.astype(BF16), _block_diag(v)
    w_k = _bdot(t_inv, _block_diag(k_a), _B_NN).astype(BF16)
    u_v = _bdot(t_inv, _block_diag(_bdot(a_kk, v_bd, _B_NN)), _B_NN).astype(BF16)
    r_q = r_t - _bdot(a_rb, _block_diag(w_k), _B_NN)
    y_v = _bdot(a_rk, v_bd, _B_NN) - _bdot(a_rb, _block_diag(u_v), _B_NN)
    v = v.astype(BF16)
    rhs = jnp.concatenate([jnp.concatenate([v, jnp.zeros_like(v)], axis=2),
                           jnp.concatenate([-u_v, w_k], axis=2)], axis=1)
    both = _bdot(jnp.concatenate([k_h, b_h], axis=1), rhs, _B_TN)
    g = _head_diag(both[:, :, 0:gw])
    m = jnp.where(eye, gamma, 0.0) - _head_diag(both[:, :, gw:])
    return r_q, y_v, m, g


def _rwkv_kernel(zb_ref, mu_ref, w0_ref, a0_ref, kk_ref, ka_ref, rk_ref, ww2_ref, wa2_ref, wg2_ref, seg_ref,
                 lnw_ref, lnb_ref, o_ref, h_ref, prev_ref):
    @pl.when(pl.program_id(1) == 0)
    def _():
        h_ref[...] = jnp.zeros_like(h_ref)
        prev_ref[...] = jnp.zeros_like(prev_ref)

    z = zb_ref[...]
    step = z.shape[0]
    seg = seg_ref[...]
    scaled, bonus, gate = _rwkv_token_terms(
        z, prev_ref[...], mu_ref[...], w0_ref[...], a0_ref[...], kk_ref[...], ka_ref[...], rk_ref[...],
        ww2_ref[...], wa2_ref[...], wg2_ref[...], seg)
    prev_ref[...] = z[step - 1:step, :]
    c = RWKV_CHUNK
    n_chunks = step // c
    groups = h_ref.shape[0]
    terms = _rwkv_chunk_terms(*(_group_tiles(x) for x in scaled))
    r_q, y_v, m, g = (x.reshape(groups, n_chunks, c, x.shape[2]) for x in terms)
    h = h_ref[...]
    ys = []
    for j in range(n_chunks):
        h_bd = _block_diag(h)
        ys.append(_bdot(r_q[:, j], h_bd, _B_NN) + y_v[:, j])
        h = _bdot(m[:, j], h_bd, _B_NN) + g[:, j]
    h_ref[...] = h
    y = _ungroup_tiles(jnp.stack(ys, axis=1).reshape(groups * n_chunks, c, -1), n_chunks)
    inv_n = 1.0 / RWKV_HEAD
    mean = _dot(y, seg) * inv_n
    d = y - mean
    var = _dot(d * d, seg) * inv_n
    yn = d * lax.rsqrt(var + LNX_EPS) * lnw_ref[...] + lnb_ref[...]
    o_ref[...] = (yn + bonus) * gate


def _rwkv(zb, mu, w0, a0, k_k, k_a, r_k, ww2, wa2, wg2, seg, lnx_w, lnx_b, bsz, seq):
    step = min(RWKV_STEP, seq)
    per_seq = seq // step
    const = lambda *shape: pl.BlockSpec(shape, lambda b, i: (0,) * len(shape))
    row = lambda n: const(1, n)
    mat = const(RWKV_LORA_PAD, RWKV_WIDTH)
    return pl.pallas_call(
        _rwkv_kernel,
        grid=(bsz, per_seq),
        in_specs=[pl.BlockSpec((step, RWKV_COLS), lambda b, i: (b * per_seq + i, 0)),
                  row(RWKV_COLS)] + [row(RWKV_WIDTH)] * 5 + [mat, mat, mat, const(RWKV_WIDTH, RWKV_WIDTH),
                                                              row(RWKV_WIDTH), row(RWKV_WIDTH)],
        out_specs=pl.BlockSpec((step, RWKV_WIDTH), lambda b, i: (b * per_seq + i, 0)),
        out_shape=jax.ShapeDtypeStruct((bsz * seq, RWKV_WIDTH), F32),
        scratch_shapes=[pltpu.VMEM((RWKV_HEADS // RWKV_GROUP, RWKV_HEAD, RWKV_GROUP * RWKV_HEAD), F32),
                        pltpu.VMEM((1, RWKV_COLS), F32)],
        compiler_params=_params("parallel", "arbitrary"),
    )(zb, mu, w0, a0, k_k, k_a, r_k, ww2, wa2, wg2, seg, lnx_w, lnx_b)


def _merge_ffn_kernel(x_ref, oa_ref, ob_ref, zc_ref, wa_ref, wb_ref, wo_ref, g1_ref,
                      g2pre_ref, g2post_ref, wg_ref, wu_ref, wd_ref, o_ref, *, ff_chunk):
    gate_a = _sigmoid(zc_ref[:, 0:D_MODEL])
    gate_b = _sigmoid(zc_ref[:, D_MODEL:2 * D_MODEL])
    mixed = gate_a * _dot(oa_ref[...], wa_ref[...]) + gate_b * _dot(ob_ref[...], wb_ref[...])
    x = x_ref[...] + _rms_norm(_dot(mixed, wo_ref[...]), g1_ref[...])
    h = _rms_norm(x, g2pre_ref[...]).astype(BF16)
    acc = jnp.zeros(x.shape, F32)
    for c in range(0, D_FF, ff_chunk):
        gt = jnp.dot(h, wg_ref[:, c:c + ff_chunk], preferred_element_type=F32)
        up = jnp.dot(h, wu_ref[:, c:c + ff_chunk], preferred_element_type=F32)
        acc = acc + _dot(gt * _sigmoid(gt) * up, wd_ref[c:c + ff_chunk, :])
    o_ref[...] = x + _rms_norm(acc, g2post_ref[...])


def _merge_ffn(x2, oa, ob, zc, wa, wb, wo, g1, g2pre, g2post, wg, wu, wd, tm=DENSE_TILE, ff_chunk=256):
    t = x2.shape[0]
    tile = lambda n: pl.BlockSpec((tm, n), lambda i: (i, 0))
    const = _resident
    return pl.pallas_call(
        functools.partial(_merge_ffn_kernel, ff_chunk=ff_chunk),
        grid=(t // tm,),
        in_specs=[tile(D_MODEL), tile(NSA_WIDTH), tile(RWKV_WIDTH), tile(MERGE_COLS),
                  const(NSA_WIDTH, D_MODEL), const(RWKV_WIDTH, D_MODEL), const(D_MODEL, D_MODEL),
                  const(1, D_MODEL), const(1, D_MODEL), const(1, D_MODEL),
                  const(D_MODEL, D_FF), const(D_MODEL, D_FF), const(D_FF, D_MODEL)],
        out_specs=tile(D_MODEL),
        out_shape=jax.ShapeDtypeStruct((t, D_MODEL), F32),
        compiler_params=_params("parallel"),
    )(x2, oa, ob, zc, wa, wb, wo, g1, g2pre, g2post, wg, wu, wd)


def _pad_cols(a, n):
    return jnp.pad(a, ((0, 0), (0, n - a.shape[1])))


def _pack_w_in(w_in):
    g0 = NSA_WIDTH + 6 * KV_WIDTH
    r0 = g0 + 3 * NSA_HEADS
    l0 = r0 + 3 * RWKV_WIDTH
    m0 = l0 + RWKV_LORA
    gates = w_in[:, g0:r0].reshape(-1, 3, NSA_KV_GROUPS, GROUP_HEADS).transpose(0, 2, 1, 3)
    gates = jnp.pad(gates.reshape(-1, NSA_KV_GROUPS, 3 * GROUP_HEADS),
                    ((0, 0), (0, 0), (0, LANES - 3 * GROUP_HEADS))).reshape(-1, NSA_KV_GROUPS * LANES)
    return jnp.concatenate([w_in[:, :g0], gates,
                            w_in[:, r0:l0], _pad_cols(w_in[:, l0:m0], RWKV_LORA_PAD),
                            w_in[:, m0:]], axis=1).astype(BF16)


def _rope_tables(seq):
    inv = 1.0 / (ROPE_THETA ** (jnp.arange(0, HEAD_DIM, 2, dtype=F32) / HEAD_DIM))
    ang = jnp.arange(seq, dtype=F32)[:, None] * inv[None, :]
    cos, sin = jnp.cos(ang), jnp.sin(ang)
    reps = LANES // HEAD_DIM
    return (jnp.concatenate([cos, cos] * reps, axis=1), jnp.concatenate([-sin, sin] * reps, axis=1))


def _layer(x, norm1_pre, norm1_post, w_in,
           cmp_pe_k, cmp_w1_k, cmp_b1_k, cmp_w2_k, cmp_pe_v, cmp_w1_v, cmp_b1_v, cmp_w2_v,
           mu_r, mu_k, mu_v, mu_w, mu_a, mu_g, w0, w_w2, a0, w_a2, w_g2,
           k_k, k_a, r_k, lnx_w, lnx_b, w_branch_a, w_branch_b, w_out,
           norm2_pre, norm2_post, w_gate, w_up, w_down):
    bsz, seq, _ = x.shape
    assert seq % RWKV_STEP == 0 or seq < RWKV_STEP
    assert seq // SLC_LEN <= MAX_SLC_BLOCKS and seq >= WINDOW + Q_BLOCK
    t = bsz * seq
    x2 = x.reshape(t, D_MODEL)
    row = lambda a: a.reshape(1, -1)


    cos_t, sin_t = _rope_tables(seq)
    za, zb, zc, qpt, qrt, kaug, kwin, vselt, vwint = _in_proj(
        x2, row(norm1_pre), _pack_w_in(w_in), cos_t, sin_t, bsz, seq)
    g_, hd = NSA_KV_GROUPS, HEAD_DIM

    def both_groups(w, cols):
        z = jnp.zeros(w.shape[:-2] + (g_ * w.shape[-2], g_ * cols), w.dtype)
        for g in range(g_):
            z = z.at[..., g * w.shape[-2]:(g + 1) * w.shape[-2], g * cols:g * cols + w.shape[-1]].set(w)
        return z

    w1 = jnp.stack([cmp_w1_k, cmp_w1_v]).reshape(2, CMP_LEN, hd, CMP_HIDDEN)
    cmp_out, cmp_out_t = _compress(za,
                        jnp.tile(jnp.stack([cmp_pe_k, cmp_pe_v]), (1, 1, g_)),
                        both_groups(w1, CMP_HIDDEN).astype(BF16),
                        jnp.tile(jnp.stack([cmp_b1_k, cmp_b1_v]).reshape(2, 1, CMP_HIDDEN), (1, 1, g_)),
                        both_groups(jnp.stack([cmp_w2_k, cmp_w2_v]), LANES).astype(BF16),
                        bsz, seq)

    n_half = seq // CMP_STRIDE
    n_slc = seq // SLC_LEN
    cmp_start = jnp.arange(n_half) * CMP_STRIDE
    slc_start = jnp.arange(MAX_SLC_BLOCKS) * SLC_LEN
    ovt = ((cmp_start[None, :] < slc_start[:, None] + SLC_LEN)
           & (cmp_start[None, :] + CMP_LEN - 1 >= slc_start[:, None])
           & (jnp.arange(MAX_SLC_BLOCKS)[:, None] < n_slc)).astype(BF16)
    o_a = _nsa_attn(qpt, qrt, cmp_out[0], cmp_out_t[1], kaug, vselt, kwin, vwint, za, ovt, bsz, seq)

    mu = _pad_cols(jnp.concatenate([mu_r, mu_k, mu_v, mu_w, mu_a, mu_g]).reshape(1, -1), RWKV_COLS)
    lora = jnp.zeros((3, RWKV_LORA_PAD, RWKV_WIDTH), F32)
    lora = lora.at[0, 0:DECAY_LORA].set(w_w2)
    lora = lora.at[1, DECAY_LORA:DECAY_LORA + AAA_LORA].set(w_a2)
    lora = lora.at[2, DECAY_LORA + AAA_LORA:RWKV_LORA].set(w_g2).astype(BF16)
    lanes = jnp.arange(RWKV_WIDTH) // RWKV_HEAD
    seg = (lanes[:, None] == lanes[None, :]).astype(BF16)
    o_b = _rwkv(zb, mu, row(w0), row(a0), row(k_k), row(k_a), row(r_k), lora[0], lora[1], lora[2], seg,
                row(lnx_w), row(lnx_b), bsz, seq)

    out = _merge_ffn(x2, o_a, o_b, zc, w_branch_a.astype(BF16), w_branch_b.astype(BF16), w_out.astype(BF16),
                     row(norm1_post), row(norm2_pre), row(norm2_post),
                     w_gate.astype(BF16), w_up.astype(BF16), w_down.astype(BF16))
    return out.reshape(bsz, seq, D_MODEL)


def kernel(x, norm1_pre, norm1_post, w_in, cmp_pe_k, cmp_w1_k, cmp_b1_k, cmp_w2_k, cmp_pe_v, cmp_w1_v, cmp_b1_v, cmp_w2_v, mu_r, mu_k, mu_v, mu_w, mu_a, mu_g, w0, w_w2, a0, w_a2, w_g2, k_k, k_a, r_k, lnx_w, lnx_b, w_branch_a, w_branch_b, w_out, norm2_pre, norm2_post, w_gate, w_up, w_down):
    params = (norm1_pre, norm1_post, w_in, cmp_pe_k, cmp_w1_k, cmp_b1_k, cmp_w2_k, cmp_pe_v, cmp_w1_v,
              cmp_b1_v, cmp_w2_v, mu_r, mu_k, mu_v, mu_w, mu_a, mu_g, w0, w_w2, a0, w_a2, w_g2,
              k_k, k_a, r_k, lnx_w, lnx_b, w_branch_a, w_branch_b, w_out,
              norm2_pre, norm2_post, w_gate, w_up, w_down)
    for layer in range(norm1_pre.shape[0]):
        x = _layer(x, *[p[layer] for p in params])
    return x
```

```python
import functools
import math

import jax
import jax.numpy as jnp
from jax import lax
from jax.experimental import pallas as pl
from jax.experimental.pallas import tpu as pltpu

F32 = jnp.float32
BF16 = jnp.bfloat16

D_MODEL = 1024
NSA_HEADS = 8
NSA_KV_GROUPS = 2
GROUP_HEADS = NSA_HEADS // NSA_KV_GROUPS
HEAD_DIM = 64
NSA_WIDTH = NSA_HEADS * HEAD_DIM
KV_WIDTH = NSA_KV_GROUPS * HEAD_DIM
CMP_LEN = 32
CMP_STRIDE = 16
CMP_HIDDEN = 256
SLC_LEN = 64
SLC_TOPK = 16
WINDOW = 512
Q_BLOCK = 128
ROPE_THETA = 10000.0
RWKV_HEADS = 8
RWKV_HEAD = 64
RWKV_WIDTH = RWKV_HEADS * RWKV_HEAD
DECAY_LORA = 32
AAA_LORA = 32
GATE_LORA = 96
LNX_EPS = 64e-5
D_FF = 2816
NORM_EPS = 1e-6
BIG = 1e30

LANES = 128
MAX_SLC_BLOCKS = 128
UNSELECTED_BIAS = -30000.0
REMOVED = -3.0e38

NSA_COLS = NSA_WIDTH + 6 * KV_WIDTH + NSA_KV_GROUPS * LANES
RWKV_LORA = DECAY_LORA + AAA_LORA + GATE_LORA
RWKV_LORA_PAD = 256
RWKV_COLS = 3 * RWKV_WIDTH + RWKV_LORA_PAD
MERGE_COLS = 2 * D_MODEL
COL_KC, COL_VC, COL_KS, COL_VS, COL_KW, COL_VW = (NSA_WIDTH + i * KV_WIDTH for i in range(6))
COL_GATE = NSA_WIDTH + 6 * KV_WIDTH

RWKV_CHUNK = 64
RWKV_STEP = 256
VMEM_LIMIT = 56 * 1024 * 1024
DENSE_TILE = 512

_NT = (((1,), (1,)), ((), ()))
_TN = (((0,), (0,)), ((), ()))


def _params(*sem):
    return pltpu.CompilerParams(dimension_semantics=sem, vmem_limit_bytes=VMEM_LIMIT)


def _sigmoid(x):
    return 1.0 / (1.0 + jnp.exp(-x))


def _rms_norm(x, g):
    return x * lax.rsqrt(jnp.mean(x * x, axis=-1, keepdims=True) + NORM_EPS) * g


def _dot(a, b):
    return jnp.dot(a.astype(BF16), b.astype(BF16), preferred_element_type=F32)


def _dot_nt(a, b):
    return lax.dot_general(a.astype(BF16), b.astype(BF16), _NT, preferred_element_type=F32)


def _dot_tn(a, b):
    return lax.dot_general(a.astype(BF16), b.astype(BF16), _TN, preferred_element_type=F32)


def _resident(*shape):
    return pl.BlockSpec(shape, lambda i: (0,) * len(shape), pipeline_mode=pl.Buffered(1))


def _col_chunks(width, step=512):
    return [(c, min(step, width - c)) for c in range(0, width, step)]


def _in_proj_kernel(x_ref, g_ref, w_ref, cos_ref, sin_ref, za_ref, zb_ref, zc_ref,
                    qpt_ref, qrt_ref, kaug_ref, kwin_ref, vselt_ref, vwint_ref, *, tiles_per_seq):
    h = _rms_norm(x_ref[...], g_ref[...]).astype(BF16)
    base = 0
    for o_ref in (za_ref, zb_ref, zc_ref):
        for c, n in _col_chunks(o_ref.shape[1]):
            o_ref[:, c:c + n] = jnp.dot(h, w_ref[:, base + c:base + c + n], preferred_element_type=F32)
        base += o_ref.shape[1]
    _nsa_prep_tile(za_ref, cos_ref[...], sin_ref[...], pl.program_id(0) % tiles_per_seq,
                   qpt_ref, qrt_ref, kaug_ref, kwin_ref, vselt_ref, vwint_ref)


def _in_proj(x2, g, w, cos_t, sin_t, bsz, seq):
    tm = min(DENSE_TILE, seq)
    t = x2.shape[0]
    ncols = w.shape[1]
    per_seq = seq // tm
    n_qb = tm // Q_BLOCK
    n_q = seq // Q_BLOCK
    g_ = NSA_KV_GROUPS
    cols = GROUP_HEADS * Q_BLOCK
    rows = lambda n: pl.BlockSpec((tm, n), lambda i: (i, 0))
    table = pl.BlockSpec((tm, LANES), lambda i: (i % per_seq, 0))
    qt_spec = pl.BlockSpec((None, g_, n_qb, LANES, cols), lambda i: (i // per_seq, 0, i % per_seq, 0, 0))
    qt_shape = jax.ShapeDtypeStruct((bsz, g_, n_q, LANES, cols), BF16)
    kv = lambda width: pl.BlockSpec((None, g_, tm, width), lambda i: (i // per_seq, 0, i % per_seq, 0))
    kvs = lambda width: jax.ShapeDtypeStruct((bsz, g_, seq, width), BF16)
    return pl.pallas_call(
        functools.partial(_in_proj_kernel, tiles_per_seq=per_seq),
        grid=(t // tm,),
        in_specs=[rows(D_MODEL), _resident(1, D_MODEL), _resident(D_MODEL, ncols), table, table],
        out_specs=[rows(NSA_COLS), rows(RWKV_COLS), rows(MERGE_COLS), qt_spec, qt_spec, kv(2 * LANES), kv(LANES),
                   pl.BlockSpec((None, g_, None, LANES, tm), lambda i: (i // per_seq, 0, i % per_seq, 0, 0)),
                   pl.BlockSpec((None, g_, n_qb, LANES, Q_BLOCK), lambda i: (i // per_seq, 0, i % per_seq, 0, 0))],
        out_shape=[jax.ShapeDtypeStruct((t, NSA_COLS), F32),
                   jax.ShapeDtypeStruct((t, RWKV_COLS), F32),
                   jax.ShapeDtypeStruct((t, MERGE_COLS), F32),
                   qt_shape, qt_shape, kvs(2 * LANES), kvs(LANES),
                   jax.ShapeDtypeStruct((bsz, g_, per_seq, LANES, tm), BF16),
                   jax.ShapeDtypeStruct((bsz, g_, n_q, LANES, Q_BLOCK), BF16)],
        compiler_params=_params("parallel"),
    )(x2, g, w, cos_t, sin_t)


def _rope(x, cos, sin_signed):
    w = x.shape[1]
    lane = lax.broadcasted_iota(jnp.int32, x.shape, 1)
    rot = jnp.where((lane & (HEAD_DIM - 1)) < HEAD_DIM // 2,
                    pltpu.roll(x, w - HEAD_DIM // 2, 1), pltpu.roll(x, HEAD_DIM // 2, 1))
    return x * cos + rot * sin_signed


def _pad_heads(x):
    low = lax.broadcasted_iota(jnp.int32, x.shape, 1) < HEAD_DIM
    return jnp.where(low, x, 0.0), jnp.where(low, pltpu.roll(x, HEAD_DIM, 1), 0.0)


def _nsa_prep_tile(za_ref, cos, sin, tile_in_seq, qpt_ref, qrt_ref, kaug_ref, kwin_ref, vselt_ref, vwint_ref):
    qscale = HEAD_DIM ** -0.5 * math.log2(math.e)
    tm = cos.shape[0]
    n_qb = tm // Q_BLOCK
    for pair in range(NSA_HEADS // 2):
        q = za_ref[:, pair * LANES:(pair + 1) * LANES]
        for o_ref, val in ((qpt_ref, q * qscale), (qrt_ref, _rope(q, cos, sin) * qscale)):
            for head, padded in zip((2 * pair, 2 * pair + 1), _pad_heads(val)):
                g, r = divmod(head, GROUP_HEADS)
                for qb in range(n_qb):
                    o_ref[g, qb, :, r * Q_BLOCK:(r + 1) * Q_BLOCK] = jnp.transpose(
                        padded[qb * Q_BLOCK:(qb + 1) * Q_BLOCK]).astype(BF16)
    lane = lax.broadcasted_iota(jnp.int32, (tm, LANES), 1)
    pos = tile_in_seq * tm + lax.broadcasted_iota(jnp.int32, (tm, LANES), 0)
    onehot = jnp.where(jnp.right_shift(pos, int(math.log2(SLC_LEN))) == lane, 1.0, 0.0).astype(BF16)
    ks = _pad_heads(_rope(za_ref[:, COL_KS:COL_KS + LANES], cos, sin))
    kw = _pad_heads(_rope(za_ref[:, COL_KW:COL_KW + LANES], cos, sin))
    vs = _pad_heads(za_ref[:, COL_VS:COL_VS + LANES])
    vw = _pad_heads(za_ref[:, COL_VW:COL_VW + LANES])
    ones_row = lane == HEAD_DIM
    for g in range(NSA_KV_GROUPS):
        kaug_ref[g, :, 0:LANES] = onehot
        kaug_ref[g, :, LANES:2 * LANES] = ks[g].astype(BF16)
        kwin_ref[g] = kw[g].astype(BF16)
        vselt_ref[g] = jnp.transpose(jnp.where(ones_row, 1.0, vs[g])).astype(BF16)
        vwt = jnp.transpose(jnp.where(ones_row, 1.0, vw[g])).astype(BF16)
        for qb in range(n_qb):
            vwint_ref[g, qb] = vwt[:, qb * Q_BLOCK:(qb + 1) * Q_BLOCK]


def _compress_kernel(z_ref, pe_ref, w1_ref, b1_ref, w2_ref, o_ref, ot_ref):
    n_half = z_ref.shape[0] // CMP_STRIDE
    y_lo = jnp.zeros((n_half, NSA_KV_GROUPS * CMP_HIDDEN), F32)
    y_hi = jnp.zeros((n_half, NSA_KV_GROUPS * CMP_HIDDEN), F32)
    for l in range(CMP_STRIDE):
        rows = z_ref[pl.ds(l, n_half, stride=CMP_STRIDE), :]
        y_lo = y_lo + _dot(rows + pe_ref[l:l + 1, :], w1_ref[l])
        y_hi = y_hi + _dot(rows + pe_ref[CMP_STRIDE + l:CMP_STRIDE + l + 1, :], w1_ref[CMP_STRIDE + l])
    pre = y_lo + pltpu.roll(y_hi, n_half - 1, 0) + b1_ref[...]
    h = 0.5 * pre * (1.0 + jnp.tanh(math.sqrt(2.0 / math.pi) * (pre + 0.044715 * (pre * pre * pre))))
    out = _dot(h, w2_ref[...])
    for g in range(NSA_KV_GROUPS):
        tile = out[:, g * LANES:(g + 1) * LANES]
        o_ref[g] = tile.astype(o_ref.dtype)
        ot_ref[g] = jnp.transpose(tile).astype(ot_ref.dtype)


def _compress(za, pe, w1, b1, w2, bsz, seq):
    n_half = seq // CMP_STRIDE
    hid = NSA_KV_GROUPS * CMP_HIDDEN
    return pl.pallas_call(
        _compress_kernel,
        grid=(2, bsz),
        in_specs=[pl.BlockSpec((seq, LANES), lambda s, b: (b, COL_KC // LANES + s)),
                  pl.BlockSpec((None, CMP_LEN, LANES), lambda s, b: (s, 0, 0)),
                  pl.BlockSpec((None, CMP_LEN, LANES, hid), lambda s, b: (s, 0, 0, 0)),
                  pl.BlockSpec((None, 1, hid), lambda s, b: (s, 0, 0)),
                  pl.BlockSpec((None, hid, NSA_KV_GROUPS * LANES), lambda s, b: (s, 0, 0))],
        out_specs=[pl.BlockSpec((None, None, NSA_KV_GROUPS, n_half, LANES), lambda s, b: (s, b, 0, 0, 0)),
                   pl.BlockSpec((None, None, NSA_KV_GROUPS, LANES, n_half), lambda s, b: (s, b, 0, 0, 0))],
        out_shape=[jax.ShapeDtypeStruct((2, bsz, NSA_KV_GROUPS, n_half, LANES), BF16),
                   jax.ShapeDtypeStruct((2, bsz, NSA_KV_GROUPS, LANES, n_half), BF16)],
        compiler_params=_params("parallel", "parallel"),
    )(za, pe, w1, b1, w2)


def _nsa_attn_kernel(qpt_ref, qrt_ref, kc_ref, vct_ref, kaug_ref, vst_ref, kw_ref, vwt_ref, gate_ref, ovt_ref,
                     o_ref, sa_ref, sb_ref, qa_ref, *, seq, key_tile, n_sel, variants):
    cols = GROUP_HEADS * Q_BLOCK
    q0 = pl.program_id(2) * Q_BLOCK
    t_row = q0 + (lax.broadcasted_iota(jnp.int32, (1, cols), 1) & (Q_BLOCK - 1))
    t_q = q0 + lax.broadcasted_iota(jnp.int32, (1, Q_BLOCK), 1)

    def all_heads(mask_bias):
        return jnp.concatenate([mask_bias] * GROUP_HEADS, axis=1)

    qrt = qrt_ref[...]
    n_chunks = WINDOW // Q_BLOCK + 1
    c0 = jnp.maximum(pl.program_id(2) - WINDOW // Q_BLOCK, 0)
    kstart = pl.multiple_of(c0 * Q_BLOCK, Q_BLOCK)

    def scores(kt):
        return jnp.dot(kaug_ref[pl.ds(pl.multiple_of(kt * key_tile, key_tile), key_tile), :], qa_ref[...],
                       preferred_element_type=F32)

    def front(rows_c, rows_k):
        s_c = jnp.dot(kc_ref[0:rows_c, :], qpt_ref[...], preferred_element_type=F32)
        s_w = jnp.dot(kw_ref[pl.ds(kstart, n_chunks * Q_BLOCK), :], qrt, preferred_element_type=F32)

        cmp_end = lax.broadcasted_iota(jnp.int32, (rows_c, 1), 0) * CMP_STRIDE + (CMP_LEN - 1)
        s_c = s_c + all_heads(jnp.where(cmp_end <= t_q, 0.0, -BIG))
        e_c = jnp.exp2(s_c - jnp.max(s_c, axis=0, keepdims=True))
        p_c = e_c * jnp.where(t_row >= CMP_LEN - 1, 1.0 / jnp.sum(e_c, axis=0, keepdims=True), 0.0)
        o_c = jnp.dot(vct_ref[:, 0:rows_c], p_c.astype(BF16), preferred_element_type=F32)

        p_sum = p_c[:, 0:Q_BLOCK]
        for r in range(1, GROUP_HEADS):
            p_sum = p_sum + p_c[:, r * Q_BLOCK:(r + 1) * Q_BLOCK]
        p_hi = p_sum.astype(BF16)
        p_lo = (p_sum - p_hi.astype(F32)).astype(BF16)
        ovt = ovt_ref[0:rows_k, 0:rows_c]
        imp = (jnp.dot(ovt, p_hi, preferred_element_type=F32)
               + jnp.dot(ovt, p_lo, preferred_element_type=F32))
        blk = lax.broadcasted_iota(jnp.int32, imp.shape, 0)
        cur = jnp.right_shift(q0 + lax.broadcasted_iota(jnp.int32, imp.shape, 1), int(math.log2(SLC_LEN)))
        imp = jnp.where(blk > cur, -BIG, imp)
        imp = jnp.where((blk == 0) | (blk == cur), BIG, imp)

        age = t_q - (kstart + lax.broadcasted_iota(jnp.int32, (n_chunks * Q_BLOCK, 1), 0))
        s_w = s_w + all_heads(jnp.where((age >= 0) & (age < WINDOW), 0.0, -BIG))
        p_w = jnp.exp2(s_w - jnp.max(s_w, axis=0, keepdims=True)).astype(BF16)
        acc_w = jnp.dot(vwt_ref[c0], p_w[0:Q_BLOCK], preferred_element_type=F32)
        for c in range(1, n_chunks):
            acc_w = acc_w + jnp.dot(vwt_ref[c0 + c], p_w[c * Q_BLOCK:(c + 1) * Q_BLOCK],
                                    preferred_element_type=F32)
        o_w = acc_w * (1.0 / acc_w[HEAD_DIM:HEAD_DIM + 1, :])

        blk_f = blk.astype(F32)
        bias_t = jnp.full(imp.shape, UNSELECTED_BIAS, F32)
        for _ in range(n_sel):
            mx = jnp.max(imp, axis=0, keepdims=True)
            first = jnp.min(jnp.where(imp == mx, blk_f, float(MAX_SLC_BLOCKS)), axis=0, keepdims=True)
            hit = blk_f == first
            bias_t = jnp.where(hit, 0.0, bias_t)
            imp = jnp.where(hit, REMOVED, imp)
        if rows_k < MAX_SLC_BLOCKS:
            bias_t = jnp.concatenate(
                [bias_t, jnp.full((MAX_SLC_BLOCKS - rows_k, Q_BLOCK), UNSELECTED_BIAS, F32)], axis=0)
        qa_ref[...] = jnp.concatenate([jnp.concatenate([bias_t.astype(BF16)] * GROUP_HEADS, axis=1), qrt], axis=0)
        sa_ref[...] = scores(0)
        return o_c, o_w

    o_c, o_w = lax.switch(
        sum((pl.program_id(2) >= first).astype(jnp.int32) for first, _, _ in variants[1:]),
        [functools.partial(front, rows_c, rows_k) for _, rows_c, rows_k in variants])

    k_iota = lax.broadcasted_iota(jnp.int32, (key_tile, 1), 0)

    def tile_update(kt, s, m, acc):
        m_new = jnp.maximum(m, jnp.max(s, axis=0, keepdims=True))
        p = jnp.exp2(s - m_new)
        acc = jnp.exp2(m - m_new) * acc + jnp.dot(vst_ref[kt], p.astype(BF16), preferred_element_type=F32)
        return m_new, acc

    def causal(kt, s):
        return s + all_heads(jnp.where(kt * key_tile + k_iota <= t_q, 0.0, -BIG))

    n_last = q0 // key_tile
    n_pairs = n_last // 2

    def pair(j, carry):
        m, acc = carry
        sb_ref[...] = scores(2 * j + 1)
        m, acc = tile_update(2 * j, sa_ref[...], m, acc)
        sa_ref[...] = scores(2 * j + 2)
        return tile_update(2 * j + 1, sb_ref[...], m, acc)

    m_s, acc_s = lax.fori_loop(0, n_pairs, pair,
                               (jnp.full((1, cols), -BIG, F32), jnp.zeros((LANES, cols), F32)))
    odd = n_last > 2 * n_pairs
    sb_ref[...] = scores(n_last)
    m_s, acc_s = tile_update(2 * n_pairs, causal(2 * n_pairs, sa_ref[...]), m_s, acc_s)
    _, acc_s = lax.cond(odd, lambda: tile_update(n_last, causal(n_last, sb_ref[...]), m_s, acc_s),
                        lambda: (m_s, acc_s))
    o_s = acc_s * (1.0 / acc_s[HEAD_DIM:HEAD_DIM + 1, :])

    gate_t = jnp.transpose(_sigmoid(gate_ref[...]))
    low = lax.broadcasted_iota(jnp.int32, (Q_BLOCK, LANES), 1) < HEAD_DIM
    heads = []
    for r in range(GROUP_HEADS):
        sl = slice(r * Q_BLOCK, (r + 1) * Q_BLOCK)
        heads.append(jnp.transpose(
            gate_t[r:r + 1, :] * o_c[:, sl]
            + gate_t[GROUP_HEADS + r:GROUP_HEADS + r + 1, :] * o_s[:, sl]
            + gate_t[2 * GROUP_HEADS + r:2 * GROUP_HEADS + r + 1, :] * o_w[:, sl]))
    for pair in range(GROUP_HEADS // 2):
        o_ref[:, pair * LANES:(pair + 1) * LANES] = jnp.where(
            low, heads[2 * pair], pltpu.roll(heads[2 * pair + 1], HEAD_DIM, 1))


def _nsa_attn(qpt, qrt, kcmp, vcmpt, kaug, vst, kw, vwt, za, ovt, bsz, seq):
    n_cmp = kcmp.shape[2]
    n_q = seq // Q_BLOCK
    key_tile = vst.shape[-1]
    n_sel = min(SLC_TOPK, seq // SLC_LEN)
    variants, first = [], 0
    while first < n_q:
        rows_k = min(max(pl.next_power_of_2(2 * (first + 1)), n_sel), MAX_SLC_BLOCKS)
        last = rows_k // 2 if rows_k < MAX_SLC_BLOCKS else n_q
        step_c = LANES * CMP_STRIDE // Q_BLOCK
        sub = first
        while sub < min(last, n_q):
            nxt = min((sub // step_c + 1) * step_c, last, n_q)
            rows_c = min(-(-nxt * Q_BLOCK // CMP_STRIDE // LANES) * LANES, n_cmp)
            variants.append((sub, rows_c, rows_k))
            sub = nxt
        first = min(last, n_q)
    kern = functools.partial(_nsa_attn_kernel, seq=seq, key_tile=key_tile, n_sel=n_sel, variants=tuple(variants))
    full = lambda *dims: pl.BlockSpec((None, None) + dims, lambda b, g, i: (b, g) + (0,) * len(dims))
    qspec = pl.BlockSpec((None, None, None, LANES, GROUP_HEADS * Q_BLOCK), lambda b, g, i: (b, g, i, 0, 0))
    return pl.pallas_call(
        kern,
        grid=(bsz, NSA_KV_GROUPS, n_q),
        in_specs=[qspec, qspec, full(n_cmp, LANES), full(LANES, n_cmp),
                  full(seq, 2 * LANES), full(seq // key_tile, LANES, key_tile),
                  full(seq, LANES), full(n_q, LANES, Q_BLOCK),
                  pl.BlockSpec((Q_BLOCK, LANES), lambda b, g, i: (b * n_q + i, COL_GATE // LANES + g)),
                  pl.BlockSpec((MAX_SLC_BLOCKS, n_cmp), lambda b, g, i: (0, 0))],
        out_specs=pl.BlockSpec((Q_BLOCK, GROUP_HEADS * HEAD_DIM), lambda b, g, i: (b * n_q + i, g)),
        out_shape=jax.ShapeDtypeStruct((bsz * seq, NSA_WIDTH), F32),
        scratch_shapes=[pltpu.VMEM((key_tile, GROUP_HEADS * Q_BLOCK), F32)] * 2
        + [pltpu.VMEM((2 * LANES, GROUP_HEADS * Q_BLOCK), BF16)],
        compiler_params=_params("parallel", "parallel", "arbitrary"),
    )(qpt, qrt, kcmp, vcmpt, kaug, vst, kw, vwt, za, ovt)


def _split3(x):
    hi = x.astype(BF16)
    r1 = x - hi.astype(F32)
    mid = r1.astype(BF16)
    return hi, mid, (r1 - mid.astype(F32)).astype(BF16)


def _rwkv_token_terms(z, prev_row, mu, w0, a0, k_k, k_a, r_k, ww2, wa2, wg2, seg):
    row = lax.broadcasted_iota(jnp.int32, z.shape, 0)
    prev = jnp.where(row == 0, prev_row, pltpu.roll(z, 1, 0))
    zs = z + (prev - z) * mu
    w = RWKV_WIDTH
    r, k, v, lo = zs[:, 0:w], zs[:, w:2 * w], zs[:, 2 * w:3 * w], zs[:, 3 * w:]
    wl = w0 + _dot(jnp.tanh(lo), ww2)
    w_log = -(jnp.maximum(-wl, 0.0) + jnp.log1p(jnp.exp(-jnp.abs(wl)))) - 0.5
    a = _sigmoid(a0 + _dot(lo, wa2))
    lw = -jnp.exp(w_log)
    g = _dot(_sigmoid(lo), wg2)
    k2 = k * (1.0 + (a - 1.0) * k_a)
    kkraw = k * k_k
    kk = kkraw * lax.rsqrt(jnp.maximum(_dot(kkraw * kkraw, seg), 1e-24))
    b = kk * a
    bonus = _dot(r * k2 * r_k, seg) * v
    step, c = z.shape[0], RWKV_CHUNK
    shift = int(math.log2(c))
    ri = lax.broadcasted_iota(jnp.int32, (step, step), 0)
    ci = lax.broadcasted_iota(jnp.int32, (step, step), 1)
    tri = jnp.where((ri >= ci) & (jnp.right_shift(ri, shift) == jnp.right_shift(ci, shift)), 1.0, 0.0).astype(BF16)
    cum = sum(jnp.dot(tri, piece, preferred_element_type=F32) for piece in _split3(lw))
    tot = jnp.concatenate([jnp.broadcast_to(cum[j * c + c - 1:(j + 1) * c, :], (c, w))
                           for j in range(step // c)], axis=0)
    e_neg = jnp.exp(-cum)
    e_rem = jnp.exp(tot - cum)
    scaled = (r * jnp.exp(cum), k2 * e_neg, b * e_neg, kk * jnp.exp(cum - lw), k2 * e_rem, b * e_rem,
              v, jnp.exp(tot))
    return scaled, bonus, g


RWKV_GROUP = LANES * 2 // RWKV_HEAD


def _group_tiles(x):
    c, gw = RWKV_CHUNK, RWKV_GROUP * RWKV_HEAD
    x3 = x.reshape(x.shape[0] // c, c, x.shape[1])
    return jnp.concatenate([x3[:, :, g * gw:(g + 1) * gw] for g in range(x.shape[1] // gw)], axis=0)


def _ungroup_tiles(t, n_chunks):
    c, gw = RWKV_CHUNK, t.shape[2]
    return jnp.concatenate([t[g * n_chunks:(g + 1) * n_chunks].reshape(n_chunks * c, gw)
                            for g in range(t.shape[0] // n_chunks)], axis=1)


def _block_diag(y):
    y = y.astype(BF16)
    t = jnp.concatenate([y] * (y.shape[2] // y.shape[1]), axis=1)
    shift = int(math.log2(y.shape[1]))
    same = (jnp.right_shift(lax.broadcasted_iota(jnp.int32, t.shape, 1), shift)
            == jnp.right_shift(lax.broadcasted_iota(jnp.int32, t.shape, 2), shift))
    return jnp.where(same, t, jnp.zeros_like(t))


def _head_diag(p):
    n = RWKV_HEAD
    head = jnp.right_shift(lax.broadcasted_iota(jnp.int32, (p.shape[0], n, p.shape[2]), 2), int(math.log2(n)))
    out = p[:, 0:n]
    for h in range(1, p.shape[1] // n):
        out = jnp.where(head == h, p[:, h * n:(h + 1) * n], out)
    return out


def _bdot(a, b, dims):
    return lax.dot_general(a.astype(BF16), b.astype(BF16), dims, preferred_element_type=F32)


_B_NN = (((2,), (1,)), ((0,), (0,)))
_B_NT = (((2,), (2,)), ((0,), (0,)))
_B_TN = (((1,), (1,)), ((0,), (0,)))


def _rwkv_chunk_terms(r_t, k_t, b_t, k_a, k_h, b_h, v, gamma):
    nb, c, gw = r_t.shape
    ri = lax.broadcasted_iota(jnp.int32, (nb, c, gw), 1)
    ci = lax.broadcasted_iota(jnp.int32, (nb, c, gw), 2) & (c - 1)
    incl, strict, eye = ri >= ci, ri > ci, ri == ci
    lhs = jnp.concatenate([r_t, k_a], axis=1).astype(BF16)
    to_k = _bdot(lhs, _block_diag(k_t), _B_NT)
    to_b = _bdot(lhs, _block_diag(b_t), _B_NT)
    a_rk = jnp.where(incl, to_k[:, 0:c], 0.0)
    a_kk = jnp.where(strict, to_k[:, c:], 0.0)
    a_rb = jnp.where(incl, to_b[:, 0:c], 0.0).astype(BF16)
    a_kb = jnp.where(strict, to_b[:, c:], 0.0)
    p = -a_kb
    t_inv = jnp.where(eye, 1.0, 0.0) + p
    p_bd = _block_diag(p)
    for _ in range(int(math.log2(c)) - 1):
        p = _bdot(p, p_bd, _B_NN)
        p_bd = _block_diag(p)
        t_inv = t_inv + _bdot(t_inv, p_bd, _B_NN)
    t_inv, v_bd = t_inv.astype(BF16), _block_diag(v)
    w_k = _bdot(t_inv, _block_diag(k_a), _B_NN).astype(BF16)
    u_v = _bdot(t_inv, _block_diag(_bdot(a_kk, v_bd, _B_NN)), _B_NN).astype(BF16)
    r_q = r_t - _bdot(a_rb, _block_diag(w_k), _B_NN)
    y_v = _bdot(a_rk, v_bd, _B_NN) - _bdot(a_rb, _block_diag(u_v), _B_NN)
    v = v.astype(BF16)
    rhs = jnp.concatenate([jnp.concatenate([v, jnp.zeros_like(v)], axis=2),
                           jnp.concatenate([-u_v, w_k], axis=2)], axis=1)
    both = _bdot(jnp.concatenate([k_h, b_h], axis=1), rhs, _B_TN)
    g = _head_diag(both[:, :, 0:gw])
    m = jnp.where(eye, gamma, 0.0) - _head_diag(both[:, :, gw:])
    return r_q, y_v, m, g


def _rwkv_kernel(zb_ref, mu_ref, w0_ref, a0_ref, kk_ref, ka_ref, rk_ref, ww2_ref, wa2_ref, wg2_ref, seg_ref,
                 lnw_ref, lnb_ref, o_ref, h_ref, prev_ref):
    @pl.when(pl.program_id(1) == 0)
    def _():
        h_ref[...] = jnp.zeros_like(h_ref)
        prev_ref[...] = jnp.zeros_like(prev_ref)

    z = zb_ref[...]
    step = z.shape[0]
    seg = seg_ref[...]
    scaled, bonus, gate = _rwkv_token_terms(
        z, prev_ref[...], mu_ref[...], w0_ref[...], a0_ref[...], kk_ref[...], ka_ref[...], rk_ref[...],
        ww2_ref[...], wa2_ref[...], wg2_ref[...], seg)
    prev_ref[...] = z[step - 1:step, :]
    c = RWKV_CHUNK
    n_chunks = step // c
    groups = h_ref.shape[0]
    terms = _rwkv_chunk_terms(*(_group_tiles(x) for x in scaled))
    r_q, y_v, m, g = (x.reshape(groups, n_chunks, c, x.shape[2]) for x in terms)
    h = h_ref[...]
    ys = []
    for j in range(n_chunks):
        h_bd = _block_diag(h)
        ys.append(_bdot(r_q[:, j], h_bd, _B_NN) + y_v[:, j])
        h = _bdot(m[:, j], h_bd, _B_NN) + g[:, j]
    h_ref[...] = h
    y = _ungroup_tiles(jnp.stack(ys, axis=1).reshape(groups * n_chunks, c, -1), n_chunks)
    inv_n = 1.0 / RWKV_HEAD
    mean = _dot(y, seg) * inv_n
    d = y - mean
    var = _dot(d * d, seg) * inv_n
    yn = d * lax.rsqrt(var + LNX_EPS) * lnw_ref[...] + lnb_ref[...]
    o_ref[...] = (yn + bonus) * gate


def _rwkv(zb, mu, w0, a0, k_k, k_a, r_k, ww2, wa2, wg2, seg, lnx_w, lnx_b, bsz, seq):
    step = min(RWKV_STEP, seq)
    per_seq = seq // step
    const = lambda *shape: pl.BlockSpec(shape, lambda b, i: (0,) * len(shape))
    row = lambda n: const(1, n)
    mat = const(RWKV_LORA_PAD, RWKV_WIDTH)
    return pl.pallas_call(
        _rwkv_kernel,
        grid=(bsz, per_seq),
        in_specs=[pl.BlockSpec((step, RWKV_COLS), lambda b, i: (b * per_seq + i, 0)),
                  row(RWKV_COLS)] + [row(RWKV_WIDTH)] * 5 + [mat, mat, mat, const(RWKV_WIDTH, RWKV_WIDTH),
                                                              row(RWKV_WIDTH), row(RWKV_WIDTH)],
        out_specs=pl.BlockSpec((step, RWKV_WIDTH), lambda b, i: (b * per_seq + i, 0)),
        out_shape=jax.ShapeDtypeStruct((bsz * seq, RWKV_WIDTH), F32),
        scratch_shapes=[pltpu.VMEM((RWKV_HEADS // RWKV_GROUP, RWKV_HEAD, RWKV_GROUP * RWKV_HEAD), F32),
                        pltpu.VMEM((1, RWKV_COLS), F32)],
        compiler_params=_params("parallel", "arbitrary"),
    )(zb, mu, w0, a0, k_k, k_a, r_k, ww2, wa2, wg2, seg, lnx_w, lnx_b)


def _merge_ffn_kernel(x_ref, oa_ref, ob_ref, zc_ref, wa_ref, wb_ref, wo_ref, g1_ref,
                      g2pre_ref, g2post_ref, wg_ref, wu_ref, wd_ref, o_ref, *, ff_chunk):
    gate_a = _sigmoid(zc_ref[:, 0:D_MODEL])
    gate_b = _sigmoid(zc_ref[:, D_MODEL:2 * D_MODEL])
    mixed = gate_a * _dot(oa_ref[...], wa_ref[...]) + gate_b * _dot(ob_ref[...], wb_ref[...])
    x = x_ref[...] + _rms_norm(_dot(mixed, wo_ref[...]), g1_ref[...])
    h = _rms_norm(x, g2pre_ref[...]).astype(BF16)
    acc = jnp.zeros(x.shape, F32)
    for c in range(0, D_FF, ff_chunk):
        gt = jnp.dot(h, wg_ref[:, c:c + ff_chunk], preferred_element_type=F32)
        up = jnp.dot(h, wu_ref[:, c:c + ff_chunk], preferred_element_type=F32)
        acc = acc + _dot(gt * _sigmoid(gt) * up, wd_ref[c:c + ff_chunk, :])
    o_ref[...] = x + _rms_norm(acc, g2post_ref[...])


def _merge_ffn(x2, oa, ob, zc, wa, wb, wo, g1, g2pre, g2post, wg, wu, wd, tm=DENSE_TILE, ff_chunk=256):
    t = x2.shape[0]
    tile = lambda n: pl.BlockSpec((tm, n), lambda i: (i, 0))
    const = _resident
    return pl.pallas_call(
        functools.partial(_merge_ffn_kernel, ff_chunk=ff_chunk),
        grid=(t // tm,),
        in_specs=[tile(D_MODEL), tile(NSA_WIDTH), tile(RWKV_WIDTH), tile(MERGE_COLS),
                  const(NSA_WIDTH, D_MODEL), const(RWKV_WIDTH, D_MODEL), const(D_MODEL, D_MODEL),
                  const(1, D_MODEL), const(1, D_MODEL), const(1, D_MODEL),
                  const(D_MODEL, D_FF), const(D_MODEL, D_FF), const(D_FF, D_MODEL)],
        out_specs=tile(D_MODEL),
        out_shape=jax.ShapeDtypeStruct((t, D_MODEL), F32),
        compiler_params=_params("parallel"),
    )(x2, oa, ob, zc, wa, wb, wo, g1, g2pre, g2post, wg, wu, wd)


def _pad_cols(a, n):
    return jnp.pad(a, ((0, 0), (0, n - a.shape[1])))


def _pack_w_in(w_in):
    g0 = NSA_WIDTH + 6 * KV_WIDTH
    r0 = g0 + 3 * NSA_HEADS
    l0 = r0 + 3 * RWKV_WIDTH
    m0 = l0 + RWKV_LORA
    gates = w_in[:, g0:r0].reshape(-1, 3, NSA_KV_GROUPS, GROUP_HEADS).transpose(0, 2, 1, 3)
    gates = jnp.pad(gates.reshape(-1, NSA_KV_GROUPS, 3 * GROUP_HEADS),
                    ((0, 0), (0, 0), (0, LANES - 3 * GROUP_HEADS))).reshape(-1, NSA_KV_GROUPS * LANES)
    return jnp.concatenate([w_in[:, :g0], gates,
                            w_in[:, r0:l0], _pad_cols(w_in[:, l0:m0], RWKV_LORA_PAD),
                            w_in[:, m0:]], axis=1).astype(BF16)


def _rope_tables(seq):
    inv = 1.0 / (ROPE_THETA ** (jnp.arange(0, HEAD_DIM, 2, dtype=F32) / HEAD_DIM))
    ang = jnp.arange(seq, dtype=F32)[:, None] * inv[None, :]
    cos, sin = jnp.cos(ang), jnp.sin(ang)
    reps = LANES // HEAD_DIM
    return (jnp.concatenate([cos, cos] * reps, axis=1), jnp.concatenate([-sin, sin] * reps, axis=1))


def _layer(x, norm1_pre, norm1_post, w_in,
           cmp_pe_k, cmp_w1_k, cmp_b1_k, cmp_w2_k, cmp_pe_v, cmp_w1_v, cmp_b1_v, cmp_w2_v,
           mu_r, mu_k, mu_v, mu_w, mu_a, mu_g, w0, w_w2, a0, w_a2, w_g2,
           k_k, k_a, r_k, lnx_w, lnx_b, w_branch_a, w_branch_b, w_out,
           norm2_pre, norm2_post, w_gate, w_up, w_down):
    bsz, seq, _ = x.shape
    assert seq % RWKV_STEP == 0 or seq < RWKV_STEP
    assert seq // SLC_LEN <= MAX_SLC_BLOCKS and seq >= WINDOW + Q_BLOCK
    t = bsz * seq
    x2 = x.reshape(t, D_MODEL)
    row = lambda a: a.reshape(1, -1)


    cos_t, sin_t = _rope_tables(seq)
    za, zb, zc, qpt, qrt, kaug, kwin, vselt, vwint = _in_proj(
        x2, row(norm1_pre), _pack_w_in(w_in), cos_t, sin_t, bsz, seq)
    g_, hd = NSA_KV_GROUPS, HEAD_DIM

    def both_groups(w, cols):
        z = jnp.zeros(w.shape[:-2] + (g_ * w.shape[-2], g_ * cols), w.dtype)
        for g in range(g_):
            z = z.at[..., g * w.shape[-2]:(g + 1) * w.shape[-2], g * cols:g * cols + w.shape[-1]].set(w)
        return z

    w1 = jnp.stack([cmp_w1_k, cmp_w1_v]).reshape(2, CMP_LEN, hd, CMP_HIDDEN)
    cmp_out, cmp_out_t = _compress(za,
                        jnp.tile(jnp.stack([cmp_pe_k, cmp_pe_v]), (1, 1, g_)),
                        both_groups(w1, CMP_HIDDEN).astype(BF16),
                        jnp.tile(jnp.stack([cmp_b1_k, cmp_b1_v]).reshape(2, 1, CMP_HIDDEN), (1, 1, g_)),
                        both_groups(jnp.stack([cmp_w2_k, cmp_w2_v]), LANES).astype(BF16),
                        bsz, seq)

    n_half = seq // CMP_STRIDE
    n_slc = seq // SLC_LEN
    cmp_start = jnp.arange(n_half) * CMP_STRIDE
    slc_start = jnp.arange(MAX_SLC_BLOCKS) * SLC_LEN
    ovt = ((cmp_start[None, :] < slc_start[:, None] + SLC_LEN)
           & (cmp_start[None, :] + CMP_LEN - 1 >= slc_start[:, None])
           & (jnp.arange(MAX_SLC_BLOCKS)[:, None] < n_slc)).astype(BF16)
    o_a = _nsa_attn(qpt, qrt, cmp_out[0], cmp_out_t[1], kaug, vselt, kwin, vwint, za, ovt, bsz, seq)

    mu = _pad_cols(jnp.concatenate([mu_r, mu_k, mu_v, mu_w, mu_a, mu_g]).reshape(1, -1), RWKV_COLS)
    lora = jnp.zeros((3, RWKV_LORA_PAD, RWKV_WIDTH), F32)
    lora = lora.at[0, 0:DECAY_LORA].set(w_w2)
    lora = lora.at[1, DECAY_LORA:DECAY_LORA + AAA_LORA].set(w_a2)
    lora = lora.at[2, DECAY_LORA + AAA_LORA:RWKV_LORA].set(w_g2).astype(BF16)
    lanes = jnp.arange(RWKV_WIDTH) // RWKV_HEAD
    seg = (lanes[:, None] == lanes[None, :]).astype(BF16)
    o_b = _rwkv(zb, mu, row(w0), row(a0), row(k_k), row(k_a), row(r_k), lora[0], lora[1], lora[2], seg,
                row(lnx_w), row(lnx_b), bsz, seq)

    out = _merge_ffn(x2, o_a, o_b, zc, w_branch_a.astype(BF16), w_branch_b.astype(BF16), w_out.astype(BF16),
                     row(norm1_post), row(norm2_pre), row(norm2_post),
                     w_gate.astype(BF16), w_up.astype(BF16), w_down.astype(BF16))
    return out.reshape(bsz, seq, D_MODEL)


def kernel(x, norm1_pre, norm1_post, w_in, cmp_pe_k, cmp_w1_k, cmp_b1_k, cmp_w2_k, cmp_pe_v, cmp_w1_v, cmp_b1_v, cmp_w2_v, mu_r, mu_k, mu_v, mu_w, mu_a, mu_g, w0, w_w2, a0, w_a2, w_g2, k_k, k_a, r_k, lnx_w, lnx_b, w_branch_a, w_branch_b, w_out, norm2_pre, norm2_post, w_gate, w_up, w_down):
    params = (norm1_pre, norm1_post, w_in, cmp_pe_k, cmp_w1_k, cmp_b1_k, cmp_w2_k, cmp_pe_v, cmp_w1_v,
              cmp_b1_v, cmp_w2_v, mu_r, mu_k, mu_v, mu_w, mu_a, mu_g, w0, w_w2, a0, w_a2, w_g2,
              k_k, k_a, r_k, lnx_w, lnx_b, w_branch_a, w_branch_b, w_out,
              norm2_pre, norm2_post, w_gate, w_up, w_down)
    for layer in range(norm1_pre.shape[0]):
        x = _layer(x, *[p[layer] for p in params])
    return x
```

```python
import functools
import math

import jax
import jax.numpy as jnp
from jax import lax
from jax.experimental import pallas as pl
from jax.experimental.pallas import tpu as pltpu

F32 = jnp.float32
BF16 = jnp.bfloat16

D_MODEL = 1024
NSA_HEADS = 8
NSA_KV_GROUPS = 2
GROUP_HEADS = NSA_HEADS // NSA_KV_GROUPS
HEAD_DIM = 64
NSA_WIDTH = NSA_HEADS * HEAD_DIM
KV_WIDTH = NSA_KV_GROUPS * HEAD_DIM
CMP_LEN = 32
CMP_STRIDE = 16
CMP_HIDDEN = 256
SLC_LEN = 64
SLC_TOPK = 16
WINDOW = 512
Q_BLOCK = 128
ROPE_THETA = 10000.0
RWKV_HEADS = 8
RWKV_HEAD = 64
RWKV_WIDTH = RWKV_HEADS * RWKV_HEAD
DECAY_LORA = 32
AAA_LORA = 32
GATE_LORA = 96
LNX_EPS = 64e-5
D_FF = 2816
NORM_EPS = 1e-6
BIG = 1e30

LANES = 128
MAX_SLC_BLOCKS = 128
UNSELECTED_BIAS = -30000.0
REMOVED = -3.0e38

NSA_COLS = NSA_WIDTH + 6 * KV_WIDTH + NSA_KV_GROUPS * LANES
RWKV_LORA = DECAY_LORA + AAA_LORA + GATE_LORA
RWKV_LORA_PAD = 256
RWKV_COLS = 3 * RWKV_WIDTH + RWKV_LORA_PAD
MERGE_COLS = 2 * D_MODEL
COL_KC, COL_VC, COL_KS, COL_VS, COL_KW, COL_VW = (NSA_WIDTH + i * KV_WIDTH for i in range(6))
COL_GATE = NSA_WIDTH + 6 * KV_WIDTH

RWKV_CHUNK = 64
RWKV_STEP = 256
VMEM_LIMIT = 56 * 1024 * 1024
DENSE_TILE = 512

_NT = (((1,), (1,)), ((), ()))
_TN = (((0,), (0,)), ((), ()))


def _params(*sem):
    return pltpu.CompilerParams(dimension_semantics=sem, vmem_limit_bytes=VMEM_LIMIT)


def _sigmoid(x):
    return 1.0 / (1.0 + jnp.exp(-x))


def _rms_norm(x, g):
    return x * lax.rsqrt(jnp.mean(x * x, axis=-1, keepdims=True) + NORM_EPS) * g


def _dot(a, b):
    return jnp.dot(a.astype(BF16), b.astype(BF16), preferred_element_type=F32)


def _dot_nt(a, b):
    return lax.dot_general(a.astype(BF16), b.astype(BF16), _NT, preferred_element_type=F32)


def _dot_tn(a, b):
    return lax.dot_general(a.astype(BF16), b.astype(BF16), _TN, preferred_element_type=F32)


def _resident(*shape):
    return pl.BlockSpec(shape, lambda i: (0,) * len(shape), pipeline_mode=pl.Buffered(1))


def _col_chunks(width, step=512):
    return [(c, min(step, width - c)) for c in range(0, width, step)]


def _in_proj_kernel(x_ref, g_ref, w_ref, cos_ref, sin_ref, za_ref, zb_ref, zc_ref,
                    qpt_ref, qrt_ref, kaug_ref, kwin_ref, vselt_ref, vwint_ref, *, tiles_per_seq):
    h = _rms_norm(x_ref[...], g_ref[...]).astype(BF16)
    base = 0
    for o_ref in (za_ref, zb_ref, zc_ref):
        for c, n in _col_chunks(o_ref.shape[1]):
            o_ref[:, c:c + n] = jnp.dot(h, w_ref[:, base + c:base + c + n], preferred_element_type=F32)
        base += o_ref.shape[1]
    _nsa_prep_tile(za_ref, cos_ref[...], sin_ref[...], pl.program_id(0) % tiles_per_seq,
                   qpt_ref, qrt_ref, kaug_ref, kwin_ref, vselt_ref, vwint_ref)


def _in_proj(x2, g, w, cos_t, sin_t, bsz, seq):
    tm = min(DENSE_TILE, seq)
    t = x2.shape[0]
    ncols = w.shape[1]
    per_seq = seq // tm
    n_qb = tm // Q_BLOCK
    n_q = seq // Q_BLOCK
    g_ = NSA_KV_GROUPS
    cols = GROUP_HEADS * Q_BLOCK
    rows = lambda n: pl.BlockSpec((tm, n), lambda i: (i, 0))
    table = pl.BlockSpec((tm, LANES), lambda i: (i % per_seq, 0))
    qt_spec = pl.BlockSpec((None, g_, n_qb, LANES, cols), lambda i: (i // per_seq, 0, i % per_seq, 0, 0))
    qt_shape = jax.ShapeDtypeStruct((bsz, g_, n_q, LANES, cols), BF16)
    kv = lambda width: pl.BlockSpec((None, g_, tm, width), lambda i: (i // per_seq, 0, i % per_seq, 0))
    kvs = lambda width: jax.ShapeDtypeStruct((bsz, g_, seq, width), BF16)
    return pl.pallas_call(
        functools.partial(_in_proj_kernel, tiles_per_seq=per_seq),
        grid=(t // tm,),
        in_specs=[rows(D_MODEL), _resident(1, D_MODEL), _resident(D_MODEL, ncols), table, table],
        out_specs=[rows(NSA_COLS), rows(RWKV_COLS), rows(MERGE_COLS), qt_spec, qt_spec, kv(2 * LANES), kv(LANES),
                   pl.BlockSpec((None, g_, None, LANES, tm), lambda i: (i // per_seq, 0, i % per_seq, 0, 0)),
                   pl.BlockSpec((None, g_, n_qb, LANES, Q_BLOCK), lambda i: (i // per_seq, 0, i % per_seq, 0, 0))],
        out_shape=[jax.ShapeDtypeStruct((t, NSA_COLS), F32),
                   jax.ShapeDtypeStruct((t, RWKV_COLS), F32),
                   jax.ShapeDtypeStruct((t, MERGE_COLS), F32),
                   qt_shape, qt_shape, kvs(2 * LANES), kvs(LANES),
                   jax.ShapeDtypeStruct((bsz, g_, per_seq, LANES, tm), BF16),
                   jax.ShapeDtypeStruct((bsz, g_, n_q, LANES, Q_BLOCK), BF16)],
        compiler_params=_params("parallel"),
    )(x2, g, w, cos_t, sin_t)


def _rope(x, cos, sin_signed):
    w = x.shape[1]
    lane = lax.broadcasted_iota(jnp.int32, x.shape, 1)
    rot = jnp.where((lane & (HEAD_DIM - 1)) < HEAD_DIM // 2,
                    pltpu.roll(x, w - HEAD_DIM // 2, 1), pltpu.roll(x, HEAD_DIM // 2, 1))
    return x * cos + rot * sin_signed


def _pad_heads(x):
    low = lax.broadcasted_iota(jnp.int32, x.shape, 1) < HEAD_DIM
    return jnp.where(low, x, 0.0), jnp.where(low, pltpu.roll(x, HEAD_DIM, 1), 0.0)


def _nsa_prep_tile(za_ref, cos, sin, tile_in_seq, qpt_ref, qrt_ref, kaug_ref, kwin_ref, vselt_ref, vwint_ref):
    qscale = HEAD_DIM ** -0.5 * math.log2(math.e)
    tm = cos.shape[0]
    n_qb = tm // Q_BLOCK
    for pair in range(NSA_HEADS // 2):
        q = za_ref[:, pair * LANES:(pair + 1) * LANES]
        for o_ref, val in ((qpt_ref, q * qscale), (qrt_ref, _rope(q, cos, sin) * qscale)):
            for head, padded in zip((2 * pair, 2 * pair + 1), _pad_heads(val)):
                g, r = divmod(head, GROUP_HEADS)
                for qb in range(n_qb):
                    o_ref[g, qb, :, r * Q_BLOCK:(r + 1) * Q_BLOCK] = jnp.transpose(
                        padded[qb * Q_BLOCK:(qb + 1) * Q_BLOCK]).astype(BF16)
    lane = lax.broadcasted_iota(jnp.int32, (tm, LANES), 1)
    pos = tile_in_seq * tm + lax.broadcasted_iota(jnp.int32, (tm, LANES), 0)
    onehot = jnp.where(jnp.right_shift(pos, int(math.log2(SLC_LEN))) == lane, 1.0, 0.0).astype(BF16)
    ks = _pad_heads(_rope(za_ref[:, COL_KS:COL_KS + LANES], cos, sin))
    kw = _pad_heads(_rope(za_ref[:, COL_KW:COL_KW + LANES], cos, sin))
    vs = _pad_heads(za_ref[:, COL_VS:COL_VS + LANES])
    vw = _pad_heads(za_ref[:, COL_VW:COL_VW + LANES])
    ones_row = lane == HEAD_DIM
    for g in range(NSA_KV_GROUPS):
        kaug_ref[g, :, 0:LANES] = onehot
        kaug_ref[g, :, LANES:2 * LANES] = ks[g].astype(BF16)
        kwin_ref[g] = kw[g].astype(BF16)
        vselt_ref[g] = jnp.transpose(jnp.where(ones_row, 1.0, vs[g])).astype(BF16)
        vwt = jnp.transpose(jnp.where(ones_row, 1.0, vw[g])).astype(BF16)
        for qb in range(n_qb):
            vwint_ref[g, qb] = vwt[:, qb * Q_BLOCK:(qb + 1) * Q_BLOCK]


def _compress_kernel(z_ref, pe_ref, w1_ref, b1_ref, w2_ref, o_ref, ot_ref):
    n_half = z_ref.shape[0] // CMP_STRIDE
    y_lo = jnp.zeros((n_half, NSA_KV_GROUPS * CMP_HIDDEN), F32)
    y_hi = jnp.zeros((n_half, NSA_KV_GROUPS * CMP_HIDDEN), F32)
    for l in range(CMP_STRIDE):
        rows = z_ref[pl.ds(l, n_half, stride=CMP_STRIDE), :]
        y_lo = y_lo + _dot(rows + pe_ref[l:l + 1, :], w1_ref[l])
        y_hi = y_hi + _dot(rows + pe_ref[CMP_STRIDE + l:CMP_STRIDE + l + 1, :], w1_ref[CMP_STRIDE + l])
    pre = y_lo + pltpu.roll(y_hi, n_half - 1, 0) + b1_ref[...]
    h = 0.5 * pre * (1.0 + jnp.tanh(math.sqrt(2.0 / math.pi) * (pre + 0.044715 * (pre * pre * pre))))
    out = _dot(h, w2_ref[...])
    for g in range(NSA_KV_GROUPS):
        tile = out[:, g * LANES:(g + 1) * LANES]
        o_ref[g] = tile.astype(o_ref.dtype)
        ot_ref[g] = jnp.transpose(tile).astype(ot_ref.dtype)


def _compress(za, pe, w1, b1, w2, bsz, seq):
    n_half = seq // CMP_STRIDE
    hid = NSA_KV_GROUPS * CMP_HIDDEN
    return pl.pallas_call(
        _compress_kernel,
        grid=(2, bsz),
        in_specs=[pl.BlockSpec((seq, LANES), lambda s, b: (b, COL_KC // LANES + s)),
                  pl.BlockSpec((None, CMP_LEN, LANES), lambda s, b: (s, 0, 0)),
                  pl.BlockSpec((None, CMP_LEN, LANES, hid), lambda s, b: (s, 0, 0, 0)),
                  pl.BlockSpec((None, 1, hid), lambda s, b: (s, 0, 0)),
                  pl.BlockSpec((None, hid, NSA_KV_GROUPS * LANES), lambda s, b: (s, 0, 0))],
        out_specs=[pl.BlockSpec((None, None, NSA_KV_GROUPS, n_half, LANES), lambda s, b: (s, b, 0, 0, 0)),
                   pl.BlockSpec((None, None, NSA_KV_GROUPS, LANES, n_half), lambda s, b: (s, b, 0, 0, 0))],
        out_shape=[jax.ShapeDtypeStruct((2, bsz, NSA_KV_GROUPS, n_half, LANES), BF16),
                   jax.ShapeDtypeStruct((2, bsz, NSA_KV_GROUPS, LANES, n_half), BF16)],
        compiler_params=_params("parallel", "parallel"),
    )(za, pe, w1, b1, w2)


def _nsa_attn_kernel(qpt_ref, qrt_ref, kc_ref, vct_ref, kaug_ref, vst_ref, kw_ref, vwt_ref, gate_ref, ovt_ref,
                     o_ref, sa_ref, sb_ref, *, seq, key_tile, n_sel):
    cols = GROUP_HEADS * Q_BLOCK
    q0 = pl.program_id(2) * Q_BLOCK
    t_row = q0 + (lax.broadcasted_iota(jnp.int32, (1, cols), 1) & (Q_BLOCK - 1))
    t_q = q0 + lax.broadcasted_iota(jnp.int32, (1, Q_BLOCK), 1)

    def all_heads(mask_bias):
        return jnp.concatenate([mask_bias] * GROUP_HEADS, axis=1)

    qrt = qrt_ref[...]
    n_chunks = WINDOW // Q_BLOCK + 1
    c0 = jnp.maximum(pl.program_id(2) - WINDOW // Q_BLOCK, 0)
    kstart = pl.multiple_of(c0 * Q_BLOCK, Q_BLOCK)

    def front():
        rows_c, rows_k = kc_ref.shape[0], MAX_SLC_BLOCKS
        s_c = jnp.dot(kc_ref[0:rows_c, :], qpt_ref[...], preferred_element_type=F32)
        s_w = jnp.dot(kw_ref[pl.ds(kstart, n_chunks * Q_BLOCK), :], qrt, preferred_element_type=F32)

        cmp_end = lax.broadcasted_iota(jnp.int32, (rows_c, 1), 0) * CMP_STRIDE + (CMP_LEN - 1)
        s_c = s_c + all_heads(jnp.where(cmp_end <= t_q, 0.0, -BIG))
        e_c = jnp.exp2(s_c - jnp.max(s_c, axis=0, keepdims=True))
        p_c = e_c * jnp.where(t_row >= CMP_LEN - 1, 1.0 / jnp.sum(e_c, axis=0, keepdims=True), 0.0)
        o_c = jnp.dot(vct_ref[:, 0:rows_c], p_c.astype(BF16), preferred_element_type=F32)

        p_sum = p_c[:, 0:Q_BLOCK]
        for r in range(1, GROUP_HEADS):
            p_sum = p_sum + p_c[:, r * Q_BLOCK:(r + 1) * Q_BLOCK]
        p_hi = p_sum.astype(BF16)
        p_lo = (p_sum - p_hi.astype(F32)).astype(BF16)
        ovt = ovt_ref[0:rows_k, 0:rows_c]
        imp = (jnp.dot(ovt, p_hi, preferred_element_type=F32)
               + jnp.dot(ovt, p_lo, preferred_element_type=F32))
        blk = lax.broadcasted_iota(jnp.int32, imp.shape, 0)
        cur = jnp.right_shift(q0 + lax.broadcasted_iota(jnp.int32, imp.shape, 1), int(math.log2(SLC_LEN)))
        imp = jnp.where(blk > cur, -BIG, imp)
        imp = jnp.where((blk == 0) | (blk == cur), BIG, imp)

        age = t_q - (kstart + lax.broadcasted_iota(jnp.int32, (n_chunks * Q_BLOCK, 1), 0))
        s_w = s_w + all_heads(jnp.where((age >= 0) & (age < WINDOW), 0.0, -BIG))
        p_w = jnp.exp2(s_w - jnp.max(s_w, axis=0, keepdims=True)).astype(BF16)
        acc_w = jnp.dot(vwt_ref[c0], p_w[0:Q_BLOCK], preferred_element_type=F32)
        for c in range(1, n_chunks):
            acc_w = acc_w + jnp.dot(vwt_ref[c0 + c], p_w[c * Q_BLOCK:(c + 1) * Q_BLOCK],
                                    preferred_element_type=F32)
        o_w = acc_w * (1.0 / acc_w[HEAD_DIM:HEAD_DIM + 1, :])

        blk_f = blk.astype(F32)
        bias_t = jnp.full(imp.shape, UNSELECTED_BIAS, F32)
        for _ in range(n_sel):
            mx = jnp.max(imp, axis=0, keepdims=True)
            first = jnp.min(jnp.where(imp == mx, blk_f, float(MAX_SLC_BLOCKS)), axis=0, keepdims=True)
            hit = blk_f == first
            bias_t = jnp.where(hit, 0.0, bias_t)
            imp = jnp.where(hit, REMOVED, imp)
        return o_c, o_w, bias_t

    o_c, o_w, bias_t = front()

    q_aug = jnp.concatenate([jnp.concatenate([bias_t.astype(BF16)] * GROUP_HEADS, axis=1), qrt], axis=0)
    k_iota = lax.broadcasted_iota(jnp.int32, (key_tile, 1), 0)

    def scores(kt):
        return jnp.dot(kaug_ref[pl.ds(pl.multiple_of(kt * key_tile, key_tile), key_tile), :], q_aug,
                       preferred_element_type=F32)

    def tile_update(kt, s, m, acc):
        m_new = jnp.maximum(m, jnp.max(s, axis=0, keepdims=True))
        p = jnp.exp2(s - m_new)
        acc = jnp.exp2(m - m_new) * acc + jnp.dot(vst_ref[kt], p.astype(BF16), preferred_element_type=F32)
        return m_new, acc

    def causal(kt, s):
        return s + all_heads(jnp.where(kt * key_tile + k_iota <= t_q, 0.0, -BIG))

    n_last = q0 // key_tile
    n_pairs = n_last // 2
    sa_ref[...] = scores(0)

    def pair(j, carry):
        m, acc = carry
        sb_ref[...] = scores(2 * j + 1)
        m, acc = tile_update(2 * j, sa_ref[...], m, acc)
        sa_ref[...] = scores(2 * j + 2)
        return tile_update(2 * j + 1, sb_ref[...], m, acc)

    m_s, acc_s = lax.fori_loop(0, n_pairs, pair,
                               (jnp.full((1, cols), -BIG, F32), jnp.zeros((LANES, cols), F32)))
    odd = n_last > 2 * n_pairs
    sb_ref[...] = scores(n_last)
    m_s, acc_s = tile_update(2 * n_pairs, causal(2 * n_pairs, sa_ref[...]), m_s, acc_s)
    _, acc_s = lax.cond(odd, lambda: tile_update(n_last, causal(n_last, sb_ref[...]), m_s, acc_s),
                        lambda: (m_s, acc_s))
    o_s = acc_s * (1.0 / acc_s[HEAD_DIM:HEAD_DIM + 1, :])

    gate_t = jnp.transpose(_sigmoid(gate_ref[...]))
    low = lax.broadcasted_iota(jnp.int32, (Q_BLOCK, LANES), 1) < HEAD_DIM
    heads = []
    for r in range(GROUP_HEADS):
        sl = slice(r * Q_BLOCK, (r + 1) * Q_BLOCK)
        heads.append(jnp.transpose(
            gate_t[r:r + 1, :] * o_c[:, sl]
            + gate_t[GROUP_HEADS + r:GROUP_HEADS + r + 1, :] * o_s[:, sl]
            + gate_t[2 * GROUP_HEADS + r:2 * GROUP_HEADS + r + 1, :] * o_w[:, sl]))
    for pair in range(GROUP_HEADS // 2):
        o_ref[:, pair * LANES:(pair + 1) * LANES] = jnp.where(
            low, heads[2 * pair], pltpu.roll(heads[2 * pair + 1], HEAD_DIM, 1))


def _nsa_attn(qpt, qrt, kcmp, vcmpt, kaug, vst, kw, vwt, za, ovt, bsz, seq):
    n_cmp = kcmp.shape[2]
    n_q = seq // Q_BLOCK
    key_tile = vst.shape[-1]
    n_sel = min(SLC_TOPK, seq // SLC_LEN)
    kern = functools.partial(_nsa_attn_kernel, seq=seq, key_tile=key_tile, n_sel=n_sel)
    full = lambda *dims: pl.BlockSpec((None, None) + dims, lambda b, g, i: (b, g) + (0,) * len(dims))
    qspec = pl.BlockSpec((None, None, None, LANES, GROUP_HEADS * Q_BLOCK), lambda b, g, i: (b, g, i, 0, 0))
    return pl.pallas_call(
        kern,
        grid=(bsz, NSA_KV_GROUPS, n_q),
        in_specs=[qspec, qspec, full(n_cmp, LANES), full(LANES, n_cmp),
                  full(seq, 2 * LANES), full(seq // key_tile, LANES, key_tile),
                  full(seq, LANES), full(n_q, LANES, Q_BLOCK),
                  pl.BlockSpec((Q_BLOCK, LANES), lambda b, g, i: (b * n_q + i, COL_GATE // LANES + g)),
                  pl.BlockSpec((MAX_SLC_BLOCKS, n_cmp), lambda b, g, i: (0, 0))],
        out_specs=pl.BlockSpec((Q_BLOCK, GROUP_HEADS * HEAD_DIM), lambda b, g, i: (b * n_q + i, g)),
        out_shape=jax.ShapeDtypeStruct((bsz * seq, NSA_WIDTH), F32),
        scratch_shapes=[pltpu.VMEM((key_tile, GROUP_HEADS * Q_BLOCK), F32)] * 2,
        compiler_params=_params("parallel", "parallel", "arbitrary"),
    )(qpt, qrt, kcmp, vcmpt, kaug, vst, kw, vwt, za, ovt)


def _split3(x):
    hi = x.astype(BF16)
    r1 = x - hi.astype(F32)
    mid = r1.astype(BF16)
    return hi, mid, (r1 - mid.astype(F32)).astype(BF16)


def _rwkv_token_terms(z, prev_row, mu, w0, a0, k_k, k_a, r_k, ww2, wa2, wg2, seg):
    row = lax.broadcasted_iota(jnp.int32, z.shape, 0)
    prev = jnp.where(row == 0, prev_row, pltpu.roll(z, 1, 0))
    zs = z + (prev - z) * mu
    w = RWKV_WIDTH
    r, k, v, lo = zs[:, 0:w], zs[:, w:2 * w], zs[:, 2 * w:3 * w], zs[:, 3 * w:]
    wl = w0 + _dot(jnp.tanh(lo), ww2)
    w_log = -(jnp.maximum(-wl, 0.0) + jnp.log1p(jnp.exp(-jnp.abs(wl)))) - 0.5
    a = _sigmoid(a0 + _dot(lo, wa2))
    lw = -jnp.exp(w_log)
    g = _dot(_sigmoid(lo), wg2)
    k2 = k * (1.0 + (a - 1.0) * k_a)
    kkraw = k * k_k
    kk = kkraw * lax.rsqrt(jnp.maximum(_dot(kkraw * kkraw, seg), 1e-24))
    b = kk * a
    bonus = _dot(r * k2 * r_k, seg) * v
    step, c = z.shape[0], RWKV_CHUNK
    shift = int(math.log2(c))
    ri = lax.broadcasted_iota(jnp.int32, (step, step), 0)
    ci = lax.broadcasted_iota(jnp.int32, (step, step), 1)
    tri = jnp.where((ri >= ci) & (jnp.right_shift(ri, shift) == jnp.right_shift(ci, shift)), 1.0, 0.0).astype(BF16)
    cum = sum(jnp.dot(tri, piece, preferred_element_type=F32) for piece in _split3(lw))
    tot = jnp.concatenate([jnp.broadcast_to(cum[j * c + c - 1:(j + 1) * c, :], (c, w))
                           for j in range(step // c)], axis=0)
    e_neg = jnp.exp(-cum)
    e_rem = jnp.exp(tot - cum)
    scaled = (r * jnp.exp(cum), k2 * e_neg, b * e_neg, kk * jnp.exp(cum - lw), k2 * e_rem, b * e_rem,
              v, jnp.exp(tot))
    return scaled, bonus, g


RWKV_GROUP = LANES * 2 // RWKV_HEAD


def _group_tiles(x):
    c, gw = RWKV_CHUNK, RWKV_GROUP * RWKV_HEAD
    x3 = x.reshape(x.shape[0] // c, c, x.shape[1])
    return jnp.concatenate([x3[:, :, g * gw:(g + 1) * gw] for g in range(x.shape[1] // gw)], axis=0)


def _ungroup_tiles(t, n_chunks):
    c, gw = RWKV_CHUNK, t.shape[2]
    return jnp.concatenate([t[g * n_chunks:(g + 1) * n_chunks].reshape(n_chunks * c, gw)
                            for g in range(t.shape[0] // n_chunks)], axis=1)


def _block_diag(y):
    y = y.astype(BF16)
    t = jnp.concatenate([y] * (y.shape[2] // y.shape[1]), axis=1)
    shift = int(math.log2(y.shape[1]))
    same = (jnp.right_shift(lax.broadcasted_iota(jnp.int32, t.shape, 1), shift)
            == jnp.right_shift(lax.broadcasted_iota(jnp.int32, t.shape, 2), shift))
    return jnp.where(same, t, jnp.zeros_like(t))


def _head_diag(p):
    n = RWKV_HEAD
    head = jnp.right_shift(lax.broadcasted_iota(jnp.int32, (p.shape[0], n, p.shape[2]), 2), int(math.log2(n)))
    out = p[:, 0:n]
    for h in range(1, p.shape[1] // n):
        out = jnp.where(head == h, p[:, h * n:(h + 1) * n], out)
    return out


def _bdot(a, b, dims):
    return lax.dot_general(a.astype(BF16), b.astype(BF16), dims, preferred_element_type=F32)


_B_NN = (((2,), (1,)), ((0,), (0,)))
_B_NT = (((2,), (2,)), ((0,), (0,)))
_B_TN = (((1,), (1,)), ((0,), (0,)))


def _rwkv_chunk_terms(r_t, k_t, b_t, k_a, k_h, b_h, v, gamma):
    nb, c, gw = r_t.shape
    ri = lax.broadcasted_iota(jnp.int32, (nb, c, gw), 1)
    ci = lax.broadcasted_iota(jnp.int32, (nb, c, gw), 2) & (c - 1)
    incl, strict, eye = ri >= ci, ri > ci, ri == ci
    lhs = jnp.concatenate([r_t, k_a], axis=1).astype(BF16)
    to_k = _bdot(lhs, _block_diag(k_t), _B_NT)
    to_b = _bdot(lhs, _block_diag(b_t), _B_NT)
    a_rk = jnp.where(incl, to_k[:, 0:c], 0.0)
    a_kk = jnp.where(strict, to_k[:, c:], 0.0)
    a_rb = jnp.where(incl, to_b[:, 0:c], 0.0).astype(BF16)
    a_kb = jnp.where(strict, to_b[:, c:], 0.0)
    p = -a_kb
    t_inv = jnp.where(eye, 1.0, 0.0) + p
    p_bd = _block_diag(p)
    for _ in range(int(math.log2(c)) - 1):
        p = _bdot(p, p_bd, _B_NN)
        p_bd = _block_diag(p)
        t_inv = t_inv + _bdot(t_inv, p_bd, _B_NN)
    t_inv, v_bd = t_inv.astype(BF16), _block_diag(v)
    w_k = _bdot(t_inv, _block_diag(k_a), _B_NN).astype(BF16)
    u_v = _bdot(t_inv, _block_diag(_bdot(a_kk, v_bd, _B_NN)), _B_NN).astype(BF16)
    r_q = r_t - _bdot(a_rb, _block_diag(w_k), _B_NN)
    y_v = _bdot(a_rk, v_bd, _B_NN) - _bdot(a_rb, _block_diag(u_v), _B_NN)
    v = v.astype(BF16)
    rhs = jnp.concatenate([jnp.concatenate([v, jnp.zeros_like(v)], axis=2),
                           jnp.concatenate([-u_v, w_k], axis=2)], axis=1)
    both = _bdot(jnp.concatenate([k_h, b_h], axis=1), rhs, _B_TN)
    g = _head_diag(both[:, :, 0:gw])
    m = jnp.where(eye, gamma, 0.0) - _head_diag(both[:, :, gw:])
    return r_q, y_v, m, g


def _rwkv_kernel(zb_ref, mu_ref, w0_ref, a0_ref, kk_ref, ka_ref, rk_ref, ww2_ref, wa2_ref, wg2_ref, seg_ref,
                 lnw_ref, lnb_ref, o_ref, h_ref, prev_ref):
    @pl.when(pl.program_id(1) == 0)
    def _():
        h_ref[...] = jnp.zeros_like(h_ref)
        prev_ref[...] = jnp.zeros_like(prev_ref)

    z = zb_ref[...]
    step = z.shape[0]
    seg = seg_ref[...]
    scaled, bonus, gate = _rwkv_token_terms(
        z, prev_ref[...], mu_ref[...], w0_ref[...], a0_ref[...], kk_ref[...], ka_ref[...], rk_ref[...],
        ww2_ref[...], wa2_ref[...], wg2_ref[...], seg)
    prev_ref[...] = z[step - 1:step, :]
    c = RWKV_CHUNK
    n_chunks = step // c
    groups = h_ref.shape[0]
    terms = _rwkv_chunk_terms(*(_group_tiles(x) for x in scaled))
    r_q, y_v, m, g = (x.reshape(groups, n_chunks, c, x.shape[2]) for x in terms)
    h = h_ref[...]
    ys = []
    for j in range(n_chunks):
        h_bd = _block_diag(h)
        ys.append(_bdot(r_q[:, j], h_bd, _B_NN) + y_v[:, j])
        h = _bdot(m[:, j], h_bd, _B_NN) + g[:, j]
    h_ref[...] = h
    y = _ungroup_tiles(jnp.stack(ys, axis=1).reshape(groups * n_chunks, c, -1), n_chunks)
    inv_n = 1.0 / RWKV_HEAD
    mean = _dot(y, seg) * inv_n
    d = y - mean
    var = _dot(d * d, seg) * inv_n
    yn = d * lax.rsqrt(var + LNX_EPS) * lnw_ref[...] + lnb_ref[...]
    o_ref[...] = (yn + bonus) * gate


def _rwkv(zb, mu, w0, a0, k_k, k_a, r_k, ww2, wa2, wg2, seg, lnx_w, lnx_b, bsz, seq):
    step = min(RWKV_STEP, seq)
    per_seq = seq // step
    const = lambda *shape: pl.BlockSpec(shape, lambda b, i: (0,) * len(shape))
    row = lambda n: const(1, n)
    mat = const(RWKV_LORA_PAD, RWKV_WIDTH)
    return pl.pallas_call(
        _rwkv_kernel,
        grid=(bsz, per_seq),
        in_specs=[pl.BlockSpec((step, RWKV_COLS), lambda b, i: (b * per_seq + i, 0)),
                  row(RWKV_COLS)] + [row(RWKV_WIDTH)] * 5 + [mat, mat, mat, const(RWKV_WIDTH, RWKV_WIDTH),
                                                              row(RWKV_WIDTH), row(RWKV_WIDTH)],
        out_specs=pl.BlockSpec((step, RWKV_WIDTH), lambda b, i: (b * per_seq + i, 0)),
        out_shape=jax.ShapeDtypeStruct((bsz * seq, RWKV_WIDTH), F32),
        scratch_shapes=[pltpu.VMEM((RWKV_HEADS // RWKV_GROUP, RWKV_HEAD, RWKV_GROUP * RWKV_HEAD), F32),
                        pltpu.VMEM((1, RWKV_COLS), F32)],
        compiler_params=_params("parallel", "arbitrary"),
    )(zb, mu, w0, a0, k_k, k_a, r_k, ww2, wa2, wg2, seg, lnx_w, lnx_b)


def _merge_ffn_kernel(x_ref, oa_ref, ob_ref, zc_ref, wa_ref, wb_ref, wo_ref, g1_ref,
                      g2pre_ref, g2post_ref, wg_ref, wu_ref, wd_ref, o_ref, *, ff_chunk):
    gate_a = _sigmoid(zc_ref[:, 0:D_MODEL])
    gate_b = _sigmoid(zc_ref[:, D_MODEL:2 * D_MODEL])
    mixed = gate_a * _dot(oa_ref[...], wa_ref[...]) + gate_b * _dot(ob_ref[...], wb_ref[...])
    x = x_ref[...] + _rms_norm(_dot(mixed, wo_ref[...]), g1_ref[...])
    h = _rms_norm(x, g2pre_ref[...]).astype(BF16)
    acc = jnp.zeros(x.shape, F32)
    for c in range(0, D_FF, ff_chunk):
        gt = jnp.dot(h, wg_ref[:, c:c + ff_chunk], preferred_element_type=F32)
        up = jnp.dot(h, wu_ref[:, c:c + ff_chunk], preferred_element_type=F32)
        acc = acc + _dot(gt * _sigmoid(gt) * up, wd_ref[c:c + ff_chunk, :])
    o_ref[...] = x + _rms_norm(acc, g2post_ref[...])


def _merge_ffn(x2, oa, ob, zc, wa, wb, wo, g1, g2pre, g2post, wg, wu, wd, tm=DENSE_TILE, ff_chunk=256):
    t = x2.shape[0]
    tile = lambda n: pl.BlockSpec((tm, n), lambda i: (i, 0))
    const = _resident
    return pl.pallas_call(
        functools.partial(_merge_ffn_kernel, ff_chunk=ff_chunk),
        grid=(t // tm,),
        in_specs=[tile(D_MODEL), tile(NSA_WIDTH), tile(RWKV_WIDTH), tile(MERGE_COLS),
                  const(NSA_WIDTH, D_MODEL), const(RWKV_WIDTH, D_MODEL), const(D_MODEL, D_MODEL),
                  const(1, D_MODEL), const(1, D_MODEL), const(1, D_MODEL),
                  const(D_MODEL, D_FF), const(D_MODEL, D_FF), const(D_FF, D_MODEL)],
        out_specs=tile(D_MODEL),
        out_shape=jax.ShapeDtypeStruct((t, D_MODEL), F32),
        compiler_params=_params("parallel"),
    )(x2, oa, ob, zc, wa, wb, wo, g1, g2pre, g2post, wg, wu, wd)


def _pad_cols(a, n):
    return jnp.pad(a, ((0, 0), (0, n - a.shape[1])))


def _pack_w_in(w_in):
    g0 = NSA_WIDTH + 6 * KV_WIDTH
    r0 = g0 + 3 * NSA_HEADS
    l0 = r0 + 3 * RWKV_WIDTH
    m0 = l0 + RWKV_LORA
    gates = w_in[:, g0:r0].reshape(-1, 3, NSA_KV_GROUPS, GROUP_HEADS).transpose(0, 2, 1, 3)
    gates = jnp.pad(gates.reshape(-1, NSA_KV_GROUPS, 3 * GROUP_HEADS),
                    ((0, 0), (0, 0), (0, LANES - 3 * GROUP_HEADS))).reshape(-1, NSA_KV_GROUPS * LANES)
    return jnp.concatenate([w_in[:, :g0], gates,
                            w_in[:, r0:l0], _pad_cols(w_in[:, l0:m0], RWKV_LORA_PAD),
                            w_in[:, m0:]], axis=1).astype(BF16)


def _rope_tables(seq):
    inv = 1.0 / (ROPE_THETA ** (jnp.arange(0, HEAD_DIM, 2, dtype=F32) / HEAD_DIM))
    ang = jnp.arange(seq, dtype=F32)[:, None] * inv[None, :]
    cos, sin = jnp.cos(ang), jnp.sin(ang)
    reps = LANES // HEAD_DIM
    return (jnp.concatenate([cos, cos] * reps, axis=1), jnp.concatenate([-sin, sin] * reps, axis=1))


def _layer(x, norm1_pre, norm1_post, w_in,
           cmp_pe_k, cmp_w1_k, cmp_b1_k, cmp_w2_k, cmp_pe_v, cmp_w1_v, cmp_b1_v, cmp_w2_v,
           mu_r, mu_k, mu_v, mu_w, mu_a, mu_g, w0, w_w2, a0, w_a2, w_g2,
           k_k, k_a, r_k, lnx_w, lnx_b, w_branch_a, w_branch_b, w_out,
           norm2_pre, norm2_post, w_gate, w_up, w_down):
    bsz, seq, _ = x.shape
    assert seq % RWKV_STEP == 0 or seq < RWKV_STEP
    assert seq // SLC_LEN <= MAX_SLC_BLOCKS and seq >= WINDOW + Q_BLOCK
    t = bsz * seq
    x2 = x.reshape(t, D_MODEL)
    row = lambda a: a.reshape(1, -1)


    cos_t, sin_t = _rope_tables(seq)
    za, zb, zc, qpt, qrt, kaug, kwin, vselt, vwint = _in_proj(
        x2, row(norm1_pre), _pack_w_in(w_in), cos_t, sin_t, bsz, seq)
    g_, hd = NSA_KV_GROUPS, HEAD_DIM

    def both_groups(w, cols):
        z = jnp.zeros(w.shape[:-2] + (g_ * w.shape[-2], g_ * cols), w.dtype)
        for g in range(g_):
            z = z.at[..., g * w.shape[-2]:(g + 1) * w.shape[-2], g * cols:g * cols + w.shape[-1]].set(w)
        return z

    w1 = jnp.stack([cmp_w1_k, cmp_w1_v]).reshape(2, CMP_LEN, hd, CMP_HIDDEN)
    cmp_out, cmp_out_t = _compress(za,
                        jnp.tile(jnp.stack([cmp_pe_k, cmp_pe_v]), (1, 1, g_)),
                        both_groups(w1, CMP_HIDDEN).astype(BF16),
                        jnp.tile(jnp.stack([cmp_b1_k, cmp_b1_v]).reshape(2, 1, CMP_HIDDEN), (1, 1, g_)),
                        both_groups(jnp.stack([cmp_w2_k, cmp_w2_v]), LANES).astype(BF16),
                        bsz, seq)

    n_half = seq // CMP_STRIDE
    n_slc = seq // SLC_LEN
    cmp_start = jnp.arange(n_half) * CMP_STRIDE
    slc_start = jnp.arange(MAX_SLC_BLOCKS) * SLC_LEN
    ovt = ((cmp_start[None, :] < slc_start[:, None] + SLC_LEN)
           & (cmp_start[None, :] + CMP_LEN - 1 >= slc_start[:, None])
           & (jnp.arange(MAX_SLC_BLOCKS)[:, None] < n_slc)).astype(BF16)
    o_a = _nsa_attn(qpt, qrt, cmp_out[0], cmp_out_t[1], kaug, vselt, kwin, vwint, za, ovt, bsz, seq)

    mu = _pad_cols(jnp.concatenate([mu_r, mu_k, mu_v, mu_w, mu_a, mu_g]).reshape(1, -1), RWKV_COLS)
    lora = jnp.zeros((3, RWKV_LORA_PAD, RWKV_WIDTH), F32)
    lora = lora.at[0, 0:DECAY_LORA].set(w_w2)
    lora = lora.at[1, DECAY_LORA:DECAY_LORA + AAA_LORA].set(w_a2)
    lora = lora.at[2, DECAY_LORA + AAA_LORA:RWKV_LORA].set(w_g2).astype(BF16)
    lanes = jnp.arange(RWKV_WIDTH) // RWKV_HEAD
    seg = (lanes[:, None] == lanes[None, :]).astype(BF16)
    o_b = _rwkv(zb, mu, row(w0), row(a0), row(k_k), row(k_a), row(r_k), lora[0], lora[1], lora[2], seg,
                row(lnx_w), row(lnx_b), bsz, seq)

    out = _merge_ffn(x2, o_a, o_b, zc, w_branch_a.astype(BF16), w_branch_b.astype(BF16), w_out.astype(BF16),
                     row(norm1_post), row(norm2_pre), row(norm2_post),
                     w_gate.astype(BF16), w_up.astype(BF16), w_down.astype(BF16))
    return out.reshape(bsz, seq, D_MODEL)


def kernel(x, norm1_pre, norm1_post, w_in, cmp_pe_k, cmp_w1_k, cmp_b1_k, cmp_w2_k, cmp_pe_v, cmp_w1_v, cmp_b1_v, cmp_w2_v, mu_r, mu_k, mu_v, mu_w, mu_a, mu_g, w0, w_w2, a0, w_a2, w_g2, k_k, k_a, r_k, lnx_w, lnx_b, w_branch_a, w_branch_b, w_out, norm2_pre, norm2_post, w_gate, w_up, w_down):
    params = (norm1_pre, norm1_post, w_in, cmp_pe_k, cmp_w1_k, cmp_b1_k, cmp_w2_k, cmp_pe_v, cmp_w1_v,
              cmp_b1_v, cmp_w2_v, mu_r, mu_k, mu_v, mu_w, mu_a, mu_g, w0, w_w2, a0, w_a2, w_g2,
              k_k, k_a, r_k, lnx_w, lnx_b, w_branch_a, w_branch_b, w_out,
              norm2_pre, norm2_post, w_gate, w_up, w_down)
    for layer in range(norm1_pre.shape[0]):
        x = _layer(x, *[p[layer] for p in params])
    return x
```

```python
import functools
import math

import jax
import jax.numpy as jnp
from jax import lax
from jax.experimental import pallas as pl
from jax.experimental.pallas import tpu as pltpu

F32 = jnp.float32
BF16 = jnp.bfloat16

D_MODEL = 1024
NSA_HEADS = 8
NSA_KV_GROUPS = 2
GROUP_HEADS = NSA_HEADS // NSA_KV_GROUPS
HEAD_DIM = 64
NSA_WIDTH = NSA_HEADS * HEAD_DIM
KV_WIDTH = NSA_KV_GROUPS * HEAD_DIM
CMP_LEN = 32
CMP_STRIDE = 16
CMP_HIDDEN = 256
SLC_LEN = 64
SLC_TOPK = 16
WINDOW = 512
Q_BLOCK = 128
ROPE_THETA = 10000.0
RWKV_HEADS = 8
RWKV_HEAD = 64
RWKV_WIDTH = RWKV_HEADS * RWKV_HEAD
DECAY_LORA = 32
AAA_LORA = 32
GATE_LORA = 96
LNX_EPS = 64e-5
D_FF = 2816
NORM_EPS = 1e-6
BIG = 1e30

LANES = 128
MAX_SLC_BLOCKS = 128
UNSELECTED_BIAS = -30000.0
REMOVED = -3.0e38

NSA_COLS = NSA_WIDTH + 6 * KV_WIDTH + NSA_KV_GROUPS * LANES
RWKV_LORA = DECAY_LORA + AAA_LORA + GATE_LORA
RWKV_LORA_PAD = 256
RWKV_COLS = 3 * RWKV_WIDTH + RWKV_LORA_PAD
MERGE_COLS = 2 * D_MODEL
COL_KC, COL_VC, COL_KS, COL_VS, COL_KW, COL_VW = (NSA_WIDTH + i * KV_WIDTH for i in range(6))
COL_GATE = NSA_WIDTH + 6 * KV_WIDTH

RWKV_CHUNK = 64
RWKV_STEP = 256
VMEM_LIMIT = 56 * 1024 * 1024
DENSE_TILE = 512

_NT = (((1,), (1,)), ((), ()))
_TN = (((0,), (0,)), ((), ()))


def _params(*sem):
    return pltpu.CompilerParams(dimension_semantics=sem, vmem_limit_bytes=VMEM_LIMIT)


def _sigmoid(x):
    return 1.0 / (1.0 + jnp.exp(-x))


def _rms_norm(x, g):
    return x * lax.rsqrt(jnp.mean(x * x, axis=-1, keepdims=True) + NORM_EPS) * g


def _dot(a, b):
    return jnp.dot(a.astype(BF16), b.astype(BF16), preferred_element_type=F32)


def _dot_nt(a, b):
    return lax.dot_general(a.astype(BF16), b.astype(BF16), _NT, preferred_element_type=F32)


def _dot_tn(a, b):
    return lax.dot_general(a.astype(BF16), b.astype(BF16), _TN, preferred_element_type=F32)


def _resident(*shape):
    return pl.BlockSpec(shape, lambda i: (0,) * len(shape), pipeline_mode=pl.Buffered(1))


def _col_chunks(width, step=512):
    return [(c, min(step, width - c)) for c in range(0, width, step)]


def _in_proj_kernel(x_ref, g_ref, w_ref, cos_ref, sin_ref, za_ref, zb_ref, zc_ref,
                    qpt_ref, qrt_ref, kaug_ref, kwin_ref, vselt_ref, vwint_ref, *, tiles_per_seq):
    h = _rms_norm(x_ref[...], g_ref[...]).astype(BF16)
    base = 0
    for o_ref in (za_ref, zb_ref, zc_ref):
        for c, n in _col_chunks(o_ref.shape[1]):
            o_ref[:, c:c + n] = jnp.dot(h, w_ref[:, base + c:base + c + n], preferred_element_type=F32)
        base += o_ref.shape[1]
    _nsa_prep_tile(za_ref, cos_ref[...], sin_ref[...], pl.program_id(0) % tiles_per_seq,
                   qpt_ref, qrt_ref, kaug_ref, kwin_ref, vselt_ref, vwint_ref)


def _in_proj(x2, g, w, cos_t, sin_t, bsz, seq):
    tm = min(DENSE_TILE, seq)
    t = x2.shape[0]
    ncols = w.shape[1]
    per_seq = seq // tm
    n_qb = tm // Q_BLOCK
    n_q = seq // Q_BLOCK
    g_ = NSA_KV_GROUPS
    cols = GROUP_HEADS * Q_BLOCK
    rows = lambda n: pl.BlockSpec((tm, n), lambda i: (i, 0))
    table = pl.BlockSpec((tm, LANES), lambda i: (i % per_seq, 0))
    qt_spec = pl.BlockSpec((None, g_, n_qb, LANES, cols), lambda i: (i // per_seq, 0, i % per_seq, 0, 0))
    qt_shape = jax.ShapeDtypeStruct((bsz, g_, n_q, LANES, cols), BF16)
    kv = lambda width: pl.BlockSpec((None, g_, tm, width), lambda i: (i // per_seq, 0, i % per_seq, 0))
    kvs = lambda width: jax.ShapeDtypeStruct((bsz, g_, seq, width), BF16)
    return pl.pallas_call(
        functools.partial(_in_proj_kernel, tiles_per_seq=per_seq),
        grid=(t // tm,),
        in_specs=[rows(D_MODEL), _resident(1, D_MODEL), _resident(D_MODEL, ncols), table, table],
        out_specs=[rows(NSA_COLS), rows(RWKV_COLS), rows(MERGE_COLS), qt_spec, qt_spec, kv(2 * LANES), kv(LANES),
                   pl.BlockSpec((None, g_, None, LANES, tm), lambda i: (i // per_seq, 0, i % per_seq, 0, 0)),
                   pl.BlockSpec((None, g_, n_qb, LANES, Q_BLOCK), lambda i: (i // per_seq, 0, i % per_seq, 0, 0))],
        out_shape=[jax.ShapeDtypeStruct((t, NSA_COLS), F32),
                   jax.ShapeDtypeStruct((t, RWKV_COLS), F32),
                   jax.ShapeDtypeStruct((t, MERGE_COLS), F32),
                   qt_shape, qt_shape, kvs(2 * LANES), kvs(LANES),
                   jax.ShapeDtypeStruct((bsz, g_, per_seq, LANES, tm), BF16),
                   jax.ShapeDtypeStruct((bsz, g_, n_q, LANES, Q_BLOCK), BF16)],
        compiler_params=_params("parallel"),
    )(x2, g, w, cos_t, sin_t)


def _rope(x, cos, sin_signed):
    w = x.shape[1]
    lane = lax.broadcasted_iota(jnp.int32, x.shape, 1)
    rot = jnp.where((lane & (HEAD_DIM - 1)) < HEAD_DIM // 2,
                    pltpu.roll(x, w - HEAD_DIM // 2, 1), pltpu.roll(x, HEAD_DIM // 2, 1))
    return x * cos + rot * sin_signed


def _pad_heads(x):
    low = lax.broadcasted_iota(jnp.int32, x.shape, 1) < HEAD_DIM
    return jnp.where(low, x, 0.0), jnp.where(low, pltpu.roll(x, HEAD_DIM, 1), 0.0)


def _nsa_prep_tile(za_ref, cos, sin, tile_in_seq, qpt_ref, qrt_ref, kaug_ref, kwin_ref, vselt_ref, vwint_ref):
    qscale = HEAD_DIM ** -0.5 * math.log2(math.e)
    tm = cos.shape[0]
    n_qb = tm // Q_BLOCK
    for pair in range(NSA_HEADS // 2):
        q = za_ref[:, pair * LANES:(pair + 1) * LANES]
        for o_ref, val in ((qpt_ref, q * qscale), (qrt_ref, _rope(q, cos, sin) * qscale)):
            for head, padded in zip((2 * pair, 2 * pair + 1), _pad_heads(val)):
                g, r = divmod(head, GROUP_HEADS)
                for qb in range(n_qb):
                    o_ref[g, qb, :, r * Q_BLOCK:(r + 1) * Q_BLOCK] = jnp.transpose(
                        padded[qb * Q_BLOCK:(qb + 1) * Q_BLOCK]).astype(BF16)
    lane = lax.broadcasted_iota(jnp.int32, (tm, LANES), 1)
    pos = tile_in_seq * tm + lax.broadcasted_iota(jnp.int32, (tm, LANES), 0)
    onehot = jnp.where(jnp.right_shift(pos, int(math.log2(SLC_LEN))) == lane, 1.0, 0.0).astype(BF16)
    ks = _pad_heads(_rope(za_ref[:, COL_KS:COL_KS + LANES], cos, sin))
    kw = _pad_heads(_rope(za_ref[:, COL_KW:COL_KW + LANES], cos, sin))
    vs = _pad_heads(za_ref[:, COL_VS:COL_VS + LANES])
    vw = _pad_heads(za_ref[:, COL_VW:COL_VW + LANES])
    ones_row = lane == HEAD_DIM
    for g in range(NSA_KV_GROUPS):
        kaug_ref[g, :, 0:LANES] = onehot
        kaug_ref[g, :, LANES:2 * LANES] = ks[g].astype(BF16)
        kwin_ref[g] = kw[g].astype(BF16)
        vselt_ref[g] = jnp.transpose(jnp.where(ones_row, 1.0, vs[g])).astype(BF16)
        vwt = jnp.transpose(jnp.where(ones_row, 1.0, vw[g])).astype(BF16)
        for qb in range(n_qb):
            vwint_ref[g, qb] = vwt[:, qb * Q_BLOCK:(qb + 1) * Q_BLOCK]


def _compress_kernel(z_ref, pe_ref, w1_ref, b1_ref, w2_ref, o_ref, ot_ref):
    n_half = z_ref.shape[0] // CMP_STRIDE
    y_lo = jnp.zeros((n_half, NSA_KV_GROUPS * CMP_HIDDEN), F32)
    y_hi = jnp.zeros((n_half, NSA_KV_GROUPS * CMP_HIDDEN), F32)
    for l in range(CMP_STRIDE):
        rows = z_ref[pl.ds(l, n_half, stride=CMP_STRIDE), :]
        y_lo = y_lo + _dot(rows + pe_ref[l:l + 1, :], w1_ref[l])
        y_hi = y_hi + _dot(rows + pe_ref[CMP_STRIDE + l:CMP_STRIDE + l + 1, :], w1_ref[CMP_STRIDE + l])
    pre = y_lo + pltpu.roll(y_hi, n_half - 1, 0) + b1_ref[...]
    h = 0.5 * pre * (1.0 + jnp.tanh(math.sqrt(2.0 / math.pi) * (pre + 0.044715 * (pre * pre * pre))))
    out = _dot(h, w2_ref[...])
    for g in range(NSA_KV_GROUPS):
        tile = out[:, g * LANES:(g + 1) * LANES]
        o_ref[g] = tile.astype(o_ref.dtype)
        ot_ref[g] = jnp.transpose(tile).astype(ot_ref.dtype)


def _compress(za, pe, w1, b1, w2, bsz, seq):
    n_half = seq // CMP_STRIDE
    hid = NSA_KV_GROUPS * CMP_HIDDEN
    return pl.pallas_call(
        _compress_kernel,
        grid=(2, bsz),
        in_specs=[pl.BlockSpec((seq, LANES), lambda s, b: (b, COL_KC // LANES + s)),
                  pl.BlockSpec((None, CMP_LEN, LANES), lambda s, b: (s, 0, 0)),
                  pl.BlockSpec((None, CMP_LEN, LANES, hid), lambda s, b: (s, 0, 0, 0)),
                  pl.BlockSpec((None, 1, hid), lambda s, b: (s, 0, 0)),
                  pl.BlockSpec((None, hid, NSA_KV_GROUPS * LANES), lambda s, b: (s, 0, 0))],
        out_specs=[pl.BlockSpec((None, None, NSA_KV_GROUPS, n_half, LANES), lambda s, b: (s, b, 0, 0, 0)),
                   pl.BlockSpec((None, None, NSA_KV_GROUPS, LANES, n_half), lambda s, b: (s, b, 0, 0, 0))],
        out_shape=[jax.ShapeDtypeStruct((2, bsz, NSA_KV_GROUPS, n_half, LANES), BF16),
                   jax.ShapeDtypeStruct((2, bsz, NSA_KV_GROUPS, LANES, n_half), BF16)],
        compiler_params=_params("parallel", "parallel"),
    )(za, pe, w1, b1, w2)


def _nsa_attn_kernel(qpt_ref, qrt_ref, kc_ref, vct_ref, kaug_ref, vst_ref, kw_ref, vwt_ref, gate_ref, ovt_ref,
                     o_ref, sa_ref, sb_ref, *, seq, key_tile, n_sel):
    cols = GROUP_HEADS * Q_BLOCK
    q0 = pl.program_id(2) * Q_BLOCK
    t_row = q0 + (lax.broadcasted_iota(jnp.int32, (1, cols), 1) & (Q_BLOCK - 1))
    t_q = q0 + lax.broadcasted_iota(jnp.int32, (1, Q_BLOCK), 1)

    def all_heads(mask_bias):
        return jnp.concatenate([mask_bias] * GROUP_HEADS, axis=1)

    qrt = qrt_ref[...]
    n_chunks = WINDOW // Q_BLOCK + 1
    c0 = jnp.maximum(pl.program_id(2) - WINDOW // Q_BLOCK, 0)
    kstart = pl.multiple_of(c0 * Q_BLOCK, Q_BLOCK)

    def front():
        rows_c, rows_k = kc_ref.shape[0], MAX_SLC_BLOCKS
        s_c = jnp.dot(kc_ref[0:rows_c, :], qpt_ref[...], preferred_element_type=F32)
        s_w = jnp.dot(kw_ref[pl.ds(kstart, n_chunks * Q_BLOCK), :], qrt, preferred_element_type=F32)

        cmp_end = lax.broadcasted_iota(jnp.int32, (rows_c, 1), 0) * CMP_STRIDE + (CMP_LEN - 1)
        s_c = s_c + all_heads(jnp.where(cmp_end <= t_q, 0.0, -BIG))
        e_c = jnp.exp2(s_c - jnp.max(s_c, axis=0, keepdims=True))
        p_c = e_c * jnp.where(t_row >= CMP_LEN - 1, 1.0 / jnp.sum(e_c, axis=0, keepdims=True), 0.0)
        o_c = jnp.dot(vct_ref[:, 0:rows_c], p_c.astype(BF16), preferred_element_type=F32)

        p_sum = p_c[:, 0:Q_BLOCK]
        for r in range(1, GROUP_HEADS):
            p_sum = p_sum + p_c[:, r * Q_BLOCK:(r + 1) * Q_BLOCK]
        p_hi = p_sum.astype(BF16)
        p_lo = (p_sum - p_hi.astype(F32)).astype(BF16)
        ovt = ovt_ref[0:rows_k, 0:rows_c]
        imp = (jnp.dot(ovt, p_hi, preferred_element_type=F32)
               + jnp.dot(ovt, p_lo, preferred_element_type=F32))
        blk = lax.broadcasted_iota(jnp.int32, imp.shape, 0)
        cur = jnp.right_shift(q0 + lax.broadcasted_iota(jnp.int32, imp.shape, 1), int(math.log2(SLC_LEN)))
        imp = jnp.where(blk > cur, -BIG, imp)
        imp = jnp.where((blk == 0) | (blk == cur), BIG, imp)

        age = t_q - (kstart + lax.broadcasted_iota(jnp.int32, (n_chunks * Q_BLOCK, 1), 0))
        s_w = s_w + all_heads(jnp.where((age >= 0) & (age < WINDOW), 0.0, -BIG))
        p_w = jnp.exp2(s_w - jnp.max(s_w, axis=0, keepdims=True)).astype(BF16)
        acc_w = jnp.dot(vwt_ref[c0], p_w[0:Q_BLOCK], preferred_element_type=F32)
        for c in range(1, n_chunks):
            acc_w = acc_w + jnp.dot(vwt_ref[c0 + c], p_w[c * Q_BLOCK:(c + 1) * Q_BLOCK],
                                    preferred_element_type=F32)
        o_w = acc_w * (1.0 / acc_w[HEAD_DIM:HEAD_DIM + 1, :])

        blk_f = blk.astype(F32)
        bias_t = jnp.full(imp.shape, UNSELECTED_BIAS, F32)
        for _ in range(n_sel):
            mx = jnp.max(imp, axis=0, keepdims=True)
            first = jnp.min(jnp.where(imp == mx, blk_f, float(MAX_SLC_BLOCKS)), axis=0, keepdims=True)
            hit = blk_f == first
            bias_t = jnp.where(hit, 0.0, bias_t)
            imp = jnp.where(hit, REMOVED, imp)
        return o_c, o_w, bias_t

    o_c, o_w, bias_t = front()

    q_aug = jnp.concatenate([jnp.concatenate([bias_t.astype(BF16)] * GROUP_HEADS, axis=1), qrt], axis=0)
    k_iota = lax.broadcasted_iota(jnp.int32, (key_tile, 1), 0)

    def scores(kt):
        return jnp.dot(kaug_ref[pl.ds(pl.multiple_of(kt * key_tile, key_tile), key_tile), :], q_aug,
                       preferred_element_type=F32)

    def tile_update(kt, s, m, acc):
        m_new = jnp.maximum(m, jnp.max(s, axis=0, keepdims=True))
        p = jnp.exp2(s - m_new)
        acc = jnp.exp2(m - m_new) * acc + jnp.dot(vst_ref[kt], p.astype(BF16), preferred_element_type=F32)
        return m_new, acc

    def causal(kt, s):
        return s + all_heads(jnp.where(kt * key_tile + k_iota <= t_q, 0.0, -BIG))

    n_last = q0 // key_tile
    n_pairs = n_last // 2
    sa_ref[...] = scores(0)

    def pair(j, carry):
        m, acc = carry
        sb_ref[...] = scores(2 * j + 1)
        m, acc = tile_update(2 * j, sa_ref[...], m, acc)
        sa_ref[...] = scores(2 * j + 2)
        return tile_update(2 * j + 1, sb_ref[...], m, acc)

    m_s, acc_s = lax.fori_loop(0, n_pairs, pair,
                               (jnp.full((1, cols), -BIG, F32), jnp.zeros((LANES, cols), F32)))
    odd = n_last > 2 * n_pairs
    sb_ref[...] = scores(n_last)
    m_s, acc_s = tile_update(2 * n_pairs, causal(2 * n_pairs, sa_ref[...]), m_s, acc_s)
    _, acc_s = lax.cond(odd, lambda: tile_update(n_last, causal(n_last, sb_ref[...]), m_s, acc_s),
                        lambda: (m_s, acc_s))
    o_s = acc_s * (1.0 / acc_s[HEAD_DIM:HEAD_DIM + 1, :])

    gate_t = jnp.transpose(_sigmoid(gate_ref[...]))
    low = lax.broadcasted_iota(jnp.int32, (Q_BLOCK, LANES), 1) < HEAD_DIM
    heads = []
    for r in range(GROUP_HEADS):
        sl = slice(r * Q_BLOCK, (r + 1) * Q_BLOCK)
        heads.append(jnp.transpose(
            gate_t[r:r + 1, :] * o_c[:, sl]
            + gate_t[GROUP_HEADS + r:GROUP_HEADS + r + 1, :] * o_s[:, sl]
            + gate_t[2 * GROUP_HEADS + r:2 * GROUP_HEADS + r + 1, :] * o_w[:, sl]))
    for pair in range(GROUP_HEADS // 2):
        o_ref[:, pair * LANES:(pair + 1) * LANES] = jnp.where(
            low, heads[2 * pair], pltpu.roll(heads[2 * pair + 1], HEAD_DIM, 1))


def _nsa_attn(qpt, qrt, kcmp, vcmpt, kaug, vst, kw, vwt, za, ovt, bsz, seq):
    n_cmp = kcmp.shape[2]
    n_q = seq // Q_BLOCK
    key_tile = vst.shape[-1]
    n_sel = min(SLC_TOPK, seq // SLC_LEN)
    kern = functools.partial(_nsa_attn_kernel, seq=seq, key_tile=key_tile, n_sel=n_sel)
    full = lambda *dims: pl.BlockSpec((None, None) + dims, lambda b, g, i: (b, g) + (0,) * len(dims))
    qspec = pl.BlockSpec((None, None, None, LANES, GROUP_HEADS * Q_BLOCK), lambda b, g, i: (b, g, i, 0, 0))
    return pl.pallas_call(
        kern,
        grid=(bsz, NSA_KV_GROUPS, n_q),
        in_specs=[qspec, qspec, full(n_cmp, LANES), full(LANES, n_cmp),
                  full(seq, 2 * LANES), full(seq // key_tile, LANES, key_tile),
                  full(seq, LANES), full(n_q, LANES, Q_BLOCK),
                  pl.BlockSpec((Q_BLOCK, LANES), lambda b, g, i: (b * n_q + i, COL_GATE // LANES + g)),
                  pl.BlockSpec((MAX_SLC_BLOCKS, n_cmp), lambda b, g, i: (0, 0))],
        out_specs=pl.BlockSpec((Q_BLOCK, GROUP_HEADS * HEAD_DIM), lambda b, g, i: (b * n_q + i, g)),
        out_shape=jax.ShapeDtypeStruct((bsz * seq, NSA_WIDTH), F32),
        scratch_shapes=[pltpu.VMEM((key_tile, GROUP_HEADS * Q_BLOCK), F32)] * 2,
        compiler_params=_params("parallel", "parallel", "arbitrary"),
    )(qpt, qrt, kcmp, vcmpt, kaug, vst, kw, vwt, za, ovt)


def _split3(x):
    hi = x.astype(BF16)
    r1 = x - hi.astype(F32)
    mid = r1.astype(BF16)
    return hi, mid, (r1 - mid.astype(F32)).astype(BF16)


def _rwkv_token_terms(z, prev_row, mu, w0, a0, k_k, k_a, r_k, ww2, wa2, wg2, seg):
    row = lax.broadcasted_iota(jnp.int32, z.shape, 0)
    prev = jnp.where(row == 0, prev_row, pltpu.roll(z, 1, 0))
    zs = z + (prev - z) * mu
    w = RWKV_WIDTH
    r, k, v, lo = zs[:, 0:w], zs[:, w:2 * w], zs[:, 2 * w:3 * w], zs[:, 3 * w:]
    wl = w0 + _dot(jnp.tanh(lo), ww2)
    w_log = -(jnp.maximum(-wl, 0.0) + jnp.log1p(jnp.exp(-jnp.abs(wl)))) - 0.5
    a = _sigmoid(a0 + _dot(lo, wa2))
    lw = -jnp.exp(w_log)
    g = _dot(_sigmoid(lo), wg2)
    k2 = k * (1.0 + (a - 1.0) * k_a)
    kkraw = k * k_k
    kk = kkraw * lax.rsqrt(jnp.maximum(_dot(kkraw * kkraw, seg), 1e-24))
    b = kk * a
    bonus = _dot(r * k2 * r_k, seg) * v
    step, c = z.shape[0], RWKV_CHUNK
    shift = int(math.log2(c))
    ri = lax.broadcasted_iota(jnp.int32, (step, step), 0)
    ci = lax.broadcasted_iota(jnp.int32, (step, step), 1)
    tri = jnp.where((ri >= ci) & (jnp.right_shift(ri, shift) == jnp.right_shift(ci, shift)), 1.0, 0.0).astype(BF16)
    cum = sum(jnp.dot(tri, piece, preferred_element_type=F32) for piece in _split3(lw))
    tot = jnp.concatenate([jnp.broadcast_to(cum[j * c + c - 1:(j + 1) * c, :], (c, w))
                           for j in range(step // c)], axis=0)
    e_neg = jnp.exp(-cum)
    e_rem = jnp.exp(tot - cum)
    scaled = (r * jnp.exp(cum), k2 * e_neg, b * e_neg, kk * jnp.exp(cum - lw), k2 * e_rem, b * e_rem,
              v, jnp.exp(tot))
    return scaled, bonus, g


RWKV_GROUP = LANES * 2 // RWKV_HEAD


def _group_tiles(x):
    c, gw = RWKV_CHUNK, RWKV_GROUP * RWKV_HEAD
    x3 = x.reshape(x.shape[0] // c, c, x.shape[1])
    return jnp.concatenate([x3[:, :, g * gw:(g + 1) * gw] for g in range(x.shape[1] // gw)], axis=0)


def _ungroup_tiles(t, n_chunks):
    c, gw = RWKV_CHUNK, t.shape[2]
    return jnp.concatenate([t[g * n_chunks:(g + 1) * n_chunks].reshape(n_chunks * c, gw)
                            for g in range(t.shape[0] // n_chunks)], axis=1)


def _block_diag(y):
    y = y.astype(BF16)
    t = jnp.concatenate([y] * (y.shape[2] // y.shape[1]), axis=1)
    shift = int(math.log2(y.shape[1]))
    same = (jnp.right_shift(lax.broadcasted_iota(jnp.int32, t.shape, 1), shift)
            == jnp.right_shift(lax.broadcasted_iota(jnp.int32, t.shape, 2), shift))
    return jnp.where(same, t, jnp.zeros_like(t))


def _head_diag(p):
    n = RWKV_HEAD
    head = jnp.right_shift(lax.broadcasted_iota(jnp.int32, (p.shape[0], n, p.shape[2]), 2), int(math.log2(n)))
    out = p[:, 0:n]
    for h in range(1, p.shape[1] // n):
        out = jnp.where(head == h, p[:, h * n:(h + 1) * n], out)
    return out


def _bdot(a, b, dims):
    return lax.dot_general(a.astype(BF16), b.astype(BF16), dims, preferred_element_type=F32)


_B_NN = (((2,), (1,)), ((0,), (0,)))
_B_NT = (((2,), (2,)), ((0,), (0,)))
_B_TN = (((1,), (1,)), ((0,), (0,)))


def _rwkv_chunk_terms(r_t, k_t, b_t, k_a, k_h, b_h, v, gamma):
    nb, c, gw = r_t.shape
    ri = lax.broadcasted_iota(jnp.int32, (nb, c, gw), 1)
    ci = lax.broadcasted_iota(jnp.int32, (nb, c, gw), 2) & (c - 1)
    incl, strict, eye = ri >= ci, ri > ci, ri == ci
    lhs = jnp.concatenate([r_t, k_a], axis=1).astype(BF16)
    to_k = _bdot(lhs, _block_diag(k_t), _B_NT)
    to_b = _bdot(lhs, _block_diag(b_t), _B_NT)
    a_rk = jnp.where(incl, to_k[:, 0:c], 0.0)
    a_kk = jnp.where(strict, to_k[:, c:], 0.0)
    a_rb = jnp.where(incl, to_b[:, 0:c], 0.0).astype(BF16)
    a_kb = jnp.where(strict, to_b[:, c:], 0.0)
    p = -a_kb
    t_inv = jnp.where(eye, 1.0, 0.0) + p
    p = _bdot(p, _block_diag(p), _B_NN)
    for _ in range(int(math.log2(c)) - 2):
        both = _bdot(jnp.concatenate([t_inv, p], axis=1), _block_diag(p), _B_NN)
        t_inv, p = t_inv + both[:, 0:c], both[:, c:]
    t_inv = t_inv + _bdot(t_inv, _block_diag(p), _B_NN)
    t_inv, v_bd = t_inv.astype(BF16), _block_diag(v)
    w_k = _bdot(t_inv, _block_diag(k_a), _B_NN).astype(BF16)
    u_v = _bdot(t_inv, _block_diag(_bdot(a_kk, v_bd, _B_NN)), _B_NN).astype(BF16)
    r_q = r_t - _bdot(a_rb, _block_diag(w_k), _B_NN)
    y_v = _bdot(a_rk, v_bd, _B_NN) - _bdot(a_rb, _block_diag(u_v), _B_NN)
    v = v.astype(BF16)
    rhs = jnp.concatenate([jnp.concatenate([v, jnp.zeros_like(v)], axis=2),
                           jnp.concatenate([-u_v, w_k], axis=2)], axis=1)
    both = _bdot(jnp.concatenate([k_h, b_h], axis=1), rhs, _B_TN)
    g = _head_diag(both[:, :, 0:gw])
    m = jnp.where(eye, gamma, 0.0) - _head_diag(both[:, :, gw:])
    return r_q, y_v, m, g


def _rwkv_kernel(zb_ref, mu_ref, w0_ref, a0_ref, kk_ref, ka_ref, rk_ref, ww2_ref, wa2_ref, wg2_ref, seg_ref,
                 lnw_ref, lnb_ref, o_ref, h_ref, prev_ref):
    @pl.when(pl.program_id(1) == 0)
    def _():
        h_ref[...] = jnp.zeros_like(h_ref)
        prev_ref[...] = jnp.zeros_like(prev_ref)

    z = zb_ref[...]
    step = z.shape[0]
    seg = seg_ref[...]
    scaled, bonus, gate = _rwkv_token_terms(
        z, prev_ref[...], mu_ref[...], w0_ref[...], a0_ref[...], kk_ref[...], ka_ref[...], rk_ref[...],
        ww2_ref[...], wa2_ref[...], wg2_ref[...], seg)
    prev_ref[...] = z[step - 1:step, :]
    c = RWKV_CHUNK
    n_chunks = step // c
    groups = h_ref.shape[0]
    terms = _rwkv_chunk_terms(*(_group_tiles(x) for x in scaled))
    r_q, y_v, m, g = (x.reshape(groups, n_chunks, c, x.shape[2]) for x in terms)
    h = h_ref[...]
    ys = []
    for j in range(n_chunks):
        h_bd = _block_diag(h)
        ys.append(_bdot(r_q[:, j], h_bd, _B_NN) + y_v[:, j])
        h = _bdot(m[:, j], h_bd, _B_NN) + g[:, j]
    h_ref[...] = h
    y = _ungroup_tiles(jnp.stack(ys, axis=1).reshape(groups * n_chunks, c, -1), n_chunks)
    inv_n = 1.0 / RWKV_HEAD
    mean = _dot(y, seg) * inv_n
    d = y - mean
    var = _dot(d * d, seg) * inv_n
    yn = d * lax.rsqrt(var + LNX_EPS) * lnw_ref[...] + lnb_ref[...]
    o_ref[...] = (yn + bonus) * gate


def _rwkv(zb, mu, w0, a0, k_k, k_a, r_k, ww2, wa2, wg2, seg, lnx_w, lnx_b, bsz, seq):
    step = min(RWKV_STEP, seq)
    per_seq = seq // step
    const = lambda *shape: pl.BlockSpec(shape, lambda b, i: (0,) * len(shape))
    row = lambda n: const(1, n)
    mat = const(RWKV_LORA_PAD, RWKV_WIDTH)
    return pl.pallas_call(
        _rwkv_kernel,
        grid=(bsz, per_seq),
        in_specs=[pl.BlockSpec((step, RWKV_COLS), lambda b, i: (b * per_seq + i, 0)),
                  row(RWKV_COLS)] + [row(RWKV_WIDTH)] * 5 + [mat, mat, mat, const(RWKV_WIDTH, RWKV_WIDTH),
                                                              row(RWKV_WIDTH), row(RWKV_WIDTH)],
        out_specs=pl.BlockSpec((step, RWKV_WIDTH), lambda b, i: (b * per_seq + i, 0)),
        out_shape=jax.ShapeDtypeStruct((bsz * seq, RWKV_WIDTH), F32),
        scratch_shapes=[pltpu.VMEM((RWKV_HEADS // RWKV_GROUP, RWKV_HEAD, RWKV_GROUP * RWKV_HEAD), F32),
                        pltpu.VMEM((1, RWKV_COLS), F32)],
        compiler_params=_params("parallel", "arbitrary"),
    )(zb, mu, w0, a0, k_k, k_a, r_k, ww2, wa2, wg2, seg, lnx_w, lnx_b)


def _merge_ffn_kernel(x_ref, oa_ref, ob_ref, zc_ref, wa_ref, wb_ref, wo_ref, g1_ref,
                      g2pre_ref, g2post_ref, wg_ref, wu_ref, wd_ref, o_ref, *, ff_chunk):
    gate_a = _sigmoid(zc_ref[:, 0:D_MODEL])
    gate_b = _sigmoid(zc_ref[:, D_MODEL:2 * D_MODEL])
    mixed = gate_a * _dot(oa_ref[...], wa_ref[...]) + gate_b * _dot(ob_ref[...], wb_ref[...])
    x = x_ref[...] + _rms_norm(_dot(mixed, wo_ref[...]), g1_ref[...])
    h = _rms_norm(x, g2pre_ref[...]).astype(BF16)
    acc = jnp.zeros(x.shape, F32)
    for c in range(0, D_FF, ff_chunk):
        gt = jnp.dot(h, wg_ref[:, c:c + ff_chunk], preferred_element_type=F32)
        up = jnp.dot(h, wu_ref[:, c:c + ff_chunk], preferred_element_type=F32)
        acc = acc + _dot(gt * _sigmoid(gt) * up, wd_ref[c:c + ff_chunk, :])
    o_ref[...] = x + _rms_norm(acc, g2post_ref[...])


def _merge_ffn(x2, oa, ob, zc, wa, wb, wo, g1, g2pre, g2post, wg, wu, wd, tm=DENSE_TILE, ff_chunk=256):
    t = x2.shape[0]
    tile = lambda n: pl.BlockSpec((tm, n), lambda i: (i, 0))
    const = _resident
    return pl.pallas_call(
        functools.partial(_merge_ffn_kernel, ff_chunk=ff_chunk),
        grid=(t // tm,),
        in_specs=[tile(D_MODEL), tile(NSA_WIDTH), tile(RWKV_WIDTH), tile(MERGE_COLS),
                  const(NSA_WIDTH, D_MODEL), const(RWKV_WIDTH, D_MODEL), const(D_MODEL, D_MODEL),
                  const(1, D_MODEL), const(1, D_MODEL), const(1, D_MODEL),
                  const(D_MODEL, D_FF), const(D_MODEL, D_FF), const(D_FF, D_MODEL)],
        out_specs=tile(D_MODEL),
        out_shape=jax.ShapeDtypeStruct((t, D_MODEL), F32),
        compiler_params=_params("parallel"),
    )(x2, oa, ob, zc, wa, wb, wo, g1, g2pre, g2post, wg, wu, wd)


def _pad_cols(a, n):
    return jnp.pad(a, ((0, 0), (0, n - a.shape[1])))


def _pack_w_in(w_in):
    g0 = NSA_WIDTH + 6 * KV_WIDTH
    r0 = g0 + 3 * NSA_HEADS
    l0 = r0 + 3 * RWKV_WIDTH
    m0 = l0 + RWKV_LORA
    gates = w_in[:, g0:r0].reshape(-1, 3, NSA_KV_GROUPS, GROUP_HEADS).transpose(0, 2, 1, 3)
    gates = jnp.pad(gates.reshape(-1, NSA_KV_GROUPS, 3 * GROUP_HEADS),
                    ((0, 0), (0, 0), (0, LANES - 3 * GROUP_HEADS))).reshape(-1, NSA_KV_GROUPS * LANES)
    return jnp.concatenate([w_in[:, :g0], gates,
                            w_in[:, r0:l0], _pad_cols(w_in[:, l0:m0], RWKV_LORA_PAD),
                            w_in[:, m0:]], axis=1).astype(BF16)


def _rope_tables(seq):
    inv = 1.0 / (ROPE_THETA ** (jnp.arange(0, HEAD_DIM, 2, dtype=F32) / HEAD_DIM))
    ang = jnp.arange(seq, dtype=F32)[:, None] * inv[None, :]
    cos, sin = jnp.cos(ang), jnp.sin(ang)
    reps = LANES // HEAD_DIM
    return (jnp.concatenate([cos, cos] * reps, axis=1), jnp.concatenate([-sin, sin] * reps, axis=1))


def _layer(x, norm1_pre, norm1_post, w_in,
           cmp_pe_k, cmp_w1_k, cmp_b1_k, cmp_w2_k, cmp_pe_v, cmp_w1_v, cmp_b1_v, cmp_w2_v,
           mu_r, mu_k, mu_v, mu_w, mu_a, mu_g, w0, w_w2, a0, w_a2, w_g2,
           k_k, k_a, r_k, lnx_w, lnx_b, w_branch_a, w_branch_b, w_out,
           norm2_pre, norm2_post, w_gate, w_up, w_down):
    bsz, seq, _ = x.shape
    assert seq % RWKV_STEP == 0 or seq < RWKV_STEP
    assert seq // SLC_LEN <= MAX_SLC_BLOCKS and seq >= WINDOW + Q_BLOCK
    t = bsz * seq
    x2 = x.reshape(t, D_MODEL)
    row = lambda a: a.reshape(1, -1)


    cos_t, sin_t = _rope_tables(seq)
    za, zb, zc, qpt, qrt, kaug, kwin, vselt, vwint = _in_proj(
        x2, row(norm1_pre), _pack_w_in(w_in), cos_t, sin_t, bsz, seq)
    g_, hd = NSA_KV_GROUPS, HEAD_DIM

    def both_groups(w, cols):
        z = jnp.zeros(w.shape[:-2] + (g_ * w.shape[-2], g_ * cols), w.dtype)
        for g in range(g_):
            z = z.at[..., g * w.shape[-2]:(g + 1) * w.shape[-2], g * cols:g * cols + w.shape[-1]].set(w)
        return z

    w1 = jnp.stack([cmp_w1_k, cmp_w1_v]).reshape(2, CMP_LEN, hd, CMP_HIDDEN)
    cmp_out, cmp_out_t = _compress(za,
                        jnp.tile(jnp.stack([cmp_pe_k, cmp_pe_v]), (1, 1, g_)),
                        both_groups(w1, CMP_HIDDEN).astype(BF16),
                        jnp.tile(jnp.stack([cmp_b1_k, cmp_b1_v]).reshape(2, 1, CMP_HIDDEN), (1, 1, g_)),
                        both_groups(jnp.stack([cmp_w2_k, cmp_w2_v]), LANES).astype(BF16),
                        bsz, seq)

    n_half = seq // CMP_STRIDE
    n_slc = seq // SLC_LEN
    cmp_start = jnp.arange(n_half) * CMP_STRIDE
    slc_start = jnp.arange(MAX_SLC_BLOCKS) * SLC_LEN
    ovt = ((cmp_start[None, :] < slc_start[:, None] + SLC_LEN)
           & (cmp_start[None, :] + CMP_LEN - 1 >= slc_start[:, None])
           & (jnp.arange(MAX_SLC_BLOCKS)[:, None] < n_slc)).astype(BF16)
    o_a = _nsa_attn(qpt, qrt, cmp_out[0], cmp_out_t[1], kaug, vselt, kwin, vwint, za, ovt, bsz, seq)

    mu = _pad_cols(jnp.concatenate([mu_r, mu_k, mu_v, mu_w, mu_a, mu_g]).reshape(1, -1), RWKV_COLS)
    lora = jnp.zeros((3, RWKV_LORA_PAD, RWKV_WIDTH), F32)
    lora = lora.at[0, 0:DECAY_LORA].set(w_w2)
    lora = lora.at[1, DECAY_LORA:DECAY_LORA + AAA_LORA].set(w_a2)
    lora = lora.at[2, DECAY_LORA + AAA_LORA:RWKV_LORA].set(w_g2).astype(BF16)
    lanes = jnp.arange(RWKV_WIDTH) // RWKV_HEAD
    seg = (lanes[:, None] == lanes[None, :]).astype(BF16)
    o_b = _rwkv(zb, mu, row(w0), row(a0), row(k_k), row(k_a), row(r_k), lora[0], lora[1], lora[2], seg,
                row(lnx_w), row(lnx_b), bsz, seq)

    out = _merge_ffn(x2, o_a, o_b, zc, w_branch_a.astype(BF16), w_branch_b.astype(BF16), w_out.astype(BF16),
                     row(norm1_post), row(norm2_pre), row(norm2_post),
                     w_gate.astype(BF16), w_up.astype(BF16), w_down.astype(BF16))
    return out.reshape(bsz, seq, D_MODEL)


def kernel(x, norm1_pre, norm1_post, w_in, cmp_pe_k, cmp_w1_k, cmp_b1_k, cmp_w2_k, cmp_pe_v, cmp_w1_v, cmp_b1_v, cmp_w2_v, mu_r, mu_k, mu_v, mu_w, mu_a, mu_g, w0, w_w2, a0, w_a2, w_g2, k_k, k_a, r_k, lnx_w, lnx_b, w_branch_a, w_branch_b, w_out, norm2_pre, norm2_post, w_gate, w_up, w_down):
    params = (norm1_pre, norm1_post, w_in, cmp_pe_k, cmp_w1_k, cmp_b1_k, cmp_w2_k, cmp_pe_v, cmp_w1_v,
              cmp_b1_v, cmp_w2_v, mu_r, mu_k, mu_v, mu_w, mu_a, mu_g, w0, w_w2, a0, w_a2, w_g2,
              k_k, k_a, r_k, lnx_w, lnx_b, w_branch_a, w_branch_b, w_out,
              norm2_pre, norm2_post, w_gate, w_up, w_down)
    for layer in range(norm1_pre.shape[0]):
        x = _layer(x, *[p[layer] for p in params])
    return x
```

```python
import functools
import math

import jax
import jax.numpy as jnp
from jax import lax
from jax.experimental import pallas as pl
from jax.experimental.pallas import tpu as pltpu

F32 = jnp.float32
BF16 = jnp.bfloat16

D_MODEL = 1024
NSA_HEADS = 8
NSA_KV_GROUPS = 2
GROUP_HEADS = NSA_HEADS // NSA_KV_GROUPS
HEAD_DIM = 64
NSA_WIDTH = NSA_HEADS * HEAD_DIM
KV_WIDTH = NSA_KV_GROUPS * HEAD_DIM
CMP_LEN = 32
CMP_STRIDE = 16
CMP_HIDDEN = 256
SLC_LEN = 64
SLC_TOPK = 16
WINDOW = 512
Q_BLOCK = 128
ROPE_THETA = 10000.0
RWKV_HEADS = 8
RWKV_HEAD = 64
RWKV_WIDTH = RWKV_HEADS * RWKV_HEAD
DECAY_LORA = 32
AAA_LORA = 32
GATE_LORA = 96
LNX_EPS = 64e-5
D_FF = 2816
NORM_EPS = 1e-6
BIG = 1e30

LANES = 128
MAX_SLC_BLOCKS = 128
UNSELECTED_BIAS = -30000.0
REMOVED = -3.0e38

NSA_COLS = NSA_WIDTH + 6 * KV_WIDTH + NSA_KV_GROUPS * LANES
RWKV_LORA = DECAY_LORA + AAA_LORA + GATE_LORA
RWKV_LORA_PAD = 256
RWKV_COLS = 3 * RWKV_WIDTH + RWKV_LORA_PAD
MERGE_COLS = 2 * D_MODEL
COL_KC, COL_VC, COL_KS, COL_VS, COL_KW, COL_VW = (NSA_WIDTH + i * KV_WIDTH for i in range(6))
COL_GATE = NSA_WIDTH + 6 * KV_WIDTH

RWKV_CHUNK = 64
RWKV_STEP = 256
VMEM_LIMIT = 56 * 1024 * 1024
DENSE_TILE = 512

_NT = (((1,), (1,)), ((), ()))
_TN = (((0,), (0,)), ((), ()))


def _params(*sem):
    return pltpu.CompilerParams(dimension_semantics=sem, vmem_limit_bytes=VMEM_LIMIT)


def _sigmoid(x):
    return 1.0 / (1.0 + jnp.exp(-x))


def _rms_norm(x, g):
    return x * lax.rsqrt(jnp.mean(x * x, axis=-1, keepdims=True) + NORM_EPS) * g


def _dot(a, b):
    return jnp.dot(a.astype(BF16), b.astype(BF16), preferred_element_type=F32)


def _dot_nt(a, b):
    return lax.dot_general(a.astype(BF16), b.astype(BF16), _NT, preferred_element_type=F32)


def _dot_tn(a, b):
    return lax.dot_general(a.astype(BF16), b.astype(BF16), _TN, preferred_element_type=F32)


def _resident(*shape):
    return pl.BlockSpec(shape, lambda i: (0,) * len(shape), pipeline_mode=pl.Buffered(1))


def _col_chunks(width, step=512):
    return [(c, min(step, width - c)) for c in range(0, width, step)]


def _in_proj_kernel(x_ref, g_ref, w_ref, cos_ref, sin_ref, za_ref, zb_ref, zc_ref,
                    qpt_ref, qrt_ref, kaug_ref, kwin_ref, vselt_ref, vwint_ref, *, tiles_per_seq):
    h = _rms_norm(x_ref[...], g_ref[...]).astype(BF16)
    base = 0
    for o_ref in (za_ref, zb_ref, zc_ref):
        for c, n in _col_chunks(o_ref.shape[1]):
            o_ref[:, c:c + n] = jnp.dot(h, w_ref[:, base + c:base + c + n], preferred_element_type=F32)
        base += o_ref.shape[1]
    _nsa_prep_tile(za_ref, cos_ref[...], sin_ref[...], pl.program_id(0) % tiles_per_seq,
                   qpt_ref, qrt_ref, kaug_ref, kwin_ref, vselt_ref, vwint_ref)


def _in_proj(x2, g, w, cos_t, sin_t, bsz, seq):
    tm = min(DENSE_TILE, seq)
    t = x2.shape[0]
    ncols = w.shape[1]
    per_seq = seq // tm
    n_qb = tm // Q_BLOCK
    n_q = seq // Q_BLOCK
    g_ = NSA_KV_GROUPS
    cols = GROUP_HEADS * Q_BLOCK
    rows = lambda n: pl.BlockSpec((tm, n), lambda i: (i, 0))
    table = pl.BlockSpec((tm, LANES), lambda i: (i % per_seq, 0))
    qt_spec = pl.BlockSpec((None, g_, n_qb, LANES, cols), lambda i: (i // per_seq, 0, i % per_seq, 0, 0))
    qt_shape = jax.ShapeDtypeStruct((bsz, g_, n_q, LANES, cols), BF16)
    kv = lambda width: pl.BlockSpec((None, g_, tm, width), lambda i: (i // per_seq, 0, i % per_seq, 0))
    kvs = lambda width: jax.ShapeDtypeStruct((bsz, g_, seq, width), BF16)
    return pl.pallas_call(
        functools.partial(_in_proj_kernel, tiles_per_seq=per_seq),
        grid=(t // tm,),
        in_specs=[rows(D_MODEL), _resident(1, D_MODEL), _resident(D_MODEL, ncols), table, table],
        out_specs=[rows(NSA_COLS), rows(RWKV_COLS), rows(MERGE_COLS), qt_spec, qt_spec, kv(2 * LANES), kv(LANES),
                   pl.BlockSpec((None, g_, None, LANES, tm), lambda i: (i // per_seq, 0, i % per_seq, 0, 0)),
                   pl.BlockSpec((None, g_, n_qb, LANES, Q_BLOCK), lambda i: (i // per_seq, 0, i % per_seq, 0, 0))],
        out_shape=[jax.ShapeDtypeStruct((t, NSA_COLS), F32),
                   jax.ShapeDtypeStruct((t, RWKV_COLS), F32),
                   jax.ShapeDtypeStruct((t, MERGE_COLS), F32),
                   qt_shape, qt_shape, kvs(2 * LANES), kvs(LANES),
                   jax.ShapeDtypeStruct((bsz, g_, per_seq, LANES, tm), BF16),
                   jax.ShapeDtypeStruct((bsz, g_, n_q, LANES, Q_BLOCK), BF16)],
        compiler_params=_params("parallel"),
    )(x2, g, w, cos_t, sin_t)


def _rope(x, cos, sin_signed):
    w = x.shape[1]
    lane = lax.broadcasted_iota(jnp.int32, x.shape, 1)
    rot = jnp.where((lane & (HEAD_DIM - 1)) < HEAD_DIM // 2,
                    pltpu.roll(x, w - HEAD_DIM // 2, 1), pltpu.roll(x, HEAD_DIM // 2, 1))
    return x * cos + rot * sin_signed


def _pad_heads(x):
    low = lax.broadcasted_iota(jnp.int32, x.shape, 1) < HEAD_DIM
    return jnp.where(low, x, 0.0), jnp.where(low, pltpu.roll(x, HEAD_DIM, 1), 0.0)


def _nsa_prep_tile(za_ref, cos, sin, tile_in_seq, qpt_ref, qrt_ref, kaug_ref, kwin_ref, vselt_ref, vwint_ref):
    qscale = HEAD_DIM ** -0.5 * math.log2(math.e)
    tm = cos.shape[0]
    n_qb = tm // Q_BLOCK
    for pair in range(NSA_HEADS // 2):
        q = za_ref[:, pair * LANES:(pair + 1) * LANES]
        for o_ref, val in ((qpt_ref, q * qscale), (qrt_ref, _rope(q, cos, sin) * qscale)):
            for head, padded in zip((2 * pair, 2 * pair + 1), _pad_heads(val)):
                g, r = divmod(head, GROUP_HEADS)
                for qb in range(n_qb):
                    o_ref[g, qb, :, r * Q_BLOCK:(r + 1) * Q_BLOCK] = jnp.transpose(
                        padded[qb * Q_BLOCK:(qb + 1) * Q_BLOCK]).astype(BF16)
    lane = lax.broadcasted_iota(jnp.int32, (tm, LANES), 1)
    pos = tile_in_seq * tm + lax.broadcasted_iota(jnp.int32, (tm, LANES), 0)
    onehot = jnp.where(jnp.right_shift(pos, int(math.log2(SLC_LEN))) == lane, 1.0, 0.0).astype(BF16)
    ks = _pad_heads(_rope(za_ref[:, COL_KS:COL_KS + LANES], cos, sin))
    kw = _pad_heads(_rope(za_ref[:, COL_KW:COL_KW + LANES], cos, sin))
    vs = _pad_heads(za_ref[:, COL_VS:COL_VS + LANES])
    vw = _pad_heads(za_ref[:, COL_VW:COL_VW + LANES])
    ones_row = lane == HEAD_DIM
    for g in range(NSA_KV_GROUPS):
        kaug_ref[g, :, 0:LANES] = onehot
        kaug_ref[g, :, LANES:2 * LANES] = ks[g].astype(BF16)
        kwin_ref[g] = kw[g].astype(BF16)
        vselt_ref[g] = jnp.transpose(jnp.where(ones_row, 1.0, vs[g])).astype(BF16)
        vwt = jnp.transpose(jnp.where(ones_row, 1.0, vw[g])).astype(BF16)
        for qb in range(n_qb):
            vwint_ref[g, qb] = vwt[:, qb * Q_BLOCK:(qb + 1) * Q_BLOCK]


def _compress_kernel(z_ref, pe_ref, w1_ref, b1_ref, w2_ref, o_ref, ot_ref):
    n_half = z_ref.shape[0] // CMP_STRIDE
    y_lo = jnp.zeros((n_half, NSA_KV_GROUPS * CMP_HIDDEN), F32)
    y_hi = jnp.zeros((n_half, NSA_KV_GROUPS * CMP_HIDDEN), F32)
    for l in range(CMP_STRIDE):
        rows = z_ref[pl.ds(l, n_half, stride=CMP_STRIDE), :]
        y_lo = y_lo + _dot(rows + pe_ref[l:l + 1, :], w1_ref[l])
        y_hi = y_hi + _dot(rows + pe_ref[CMP_STRIDE + l:CMP_STRIDE + l + 1, :], w1_ref[CMP_STRIDE + l])
    pre = y_lo + pltpu.roll(y_hi, n_half - 1, 0) + b1_ref[...]
    h = 0.5 * pre * (1.0 + jnp.tanh(math.sqrt(2.0 / math.pi) * (pre + 0.044715 * (pre * pre * pre))))
    out = _dot(h, w2_ref[...])
    for g in range(NSA_KV_GROUPS):
        tile = out[:, g * LANES:(g + 1) * LANES]
        o_ref[g] = tile.astype(o_ref.dtype)
        ot_ref[g] = jnp.transpose(tile).astype(ot_ref.dtype)


def _compress(za, pe, w1, b1, w2, bsz, seq):
    n_half = seq // CMP_STRIDE
    hid = NSA_KV_GROUPS * CMP_HIDDEN
    return pl.pallas_call(
        _compress_kernel,
        grid=(2, bsz),
        in_specs=[pl.BlockSpec((seq, LANES), lambda s, b: (b, COL_KC // LANES + s)),
                  pl.BlockSpec((None, CMP_LEN, LANES), lambda s, b: (s, 0, 0)),
                  pl.BlockSpec((None, CMP_LEN, LANES, hid), lambda s, b: (s, 0, 0, 0)),
                  pl.BlockSpec((None, 1, hid), lambda s, b: (s, 0, 0)),
                  pl.BlockSpec((None, hid, NSA_KV_GROUPS * LANES), lambda s, b: (s, 0, 0))],
        out_specs=[pl.BlockSpec((None, None, NSA_KV_GROUPS, n_half, LANES), lambda s, b: (s, b, 0, 0, 0)),
                   pl.BlockSpec((None, None, NSA_KV_GROUPS, LANES, n_half), lambda s, b: (s, b, 0, 0, 0))],
        out_shape=[jax.ShapeDtypeStruct((2, bsz, NSA_KV_GROUPS, n_half, LANES), BF16),
                   jax.ShapeDtypeStruct((2, bsz, NSA_KV_GROUPS, LANES, n_half), BF16)],
        compiler_params=_params("parallel", "parallel"),
    )(za, pe, w1, b1, w2)


def _nsa_attn_kernel(qpt_ref, qrt_ref, kc_ref, vct_ref, kaug_ref, vst_ref, kw_ref, vwt_ref, gate_ref, ovt_ref,
                     o_ref, sa_ref, sb_ref, sc_ref, *, seq, key_tile, n_sel):
    cols = GROUP_HEADS * Q_BLOCK
    q0 = pl.program_id(2) * Q_BLOCK
    t_row = q0 + (lax.broadcasted_iota(jnp.int32, (1, cols), 1) & (Q_BLOCK - 1))
    t_q = q0 + lax.broadcasted_iota(jnp.int32, (1, Q_BLOCK), 1)

    def all_heads(mask_bias):
        return jnp.concatenate([mask_bias] * GROUP_HEADS, axis=1)

    qrt = qrt_ref[...]
    n_chunks = WINDOW // Q_BLOCK + 1
    c0 = jnp.maximum(pl.program_id(2) - WINDOW // Q_BLOCK, 0)
    kstart = pl.multiple_of(c0 * Q_BLOCK, Q_BLOCK)

    def front():
        rows_c, rows_k = kc_ref.shape[0], MAX_SLC_BLOCKS
        s_c = jnp.dot(kc_ref[0:rows_c, :], qpt_ref[...], preferred_element_type=F32)
        s_w = jnp.dot(kw_ref[pl.ds(kstart, n_chunks * Q_BLOCK), :], qrt, preferred_element_type=F32)

        cmp_end = lax.broadcasted_iota(jnp.int32, (rows_c, 1), 0) * CMP_STRIDE + (CMP_LEN - 1)
        s_c = s_c + all_heads(jnp.where(cmp_end <= t_q, 0.0, -BIG))
        e_c = jnp.exp2(s_c - jnp.max(s_c, axis=0, keepdims=True))
        p_c = e_c * jnp.where(t_row >= CMP_LEN - 1, 1.0 / jnp.sum(e_c, axis=0, keepdims=True), 0.0)
        o_c = jnp.dot(vct_ref[:, 0:rows_c], p_c.astype(BF16), preferred_element_type=F32)

        p_sum = p_c[:, 0:Q_BLOCK]
        for r in range(1, GROUP_HEADS):
            p_sum = p_sum + p_c[:, r * Q_BLOCK:(r + 1) * Q_BLOCK]
        p_hi = p_sum.astype(BF16)
        p_lo = (p_sum - p_hi.astype(F32)).astype(BF16)
        ovt = ovt_ref[0:rows_k, 0:rows_c]
        imp = (jnp.dot(ovt, p_hi, preferred_element_type=F32)
               + jnp.dot(ovt, p_lo, preferred_element_type=F32))
        blk = lax.broadcasted_iota(jnp.int32, imp.shape, 0)
        cur = jnp.right_shift(q0 + lax.broadcasted_iota(jnp.int32, imp.shape, 1), int(math.log2(SLC_LEN)))
        imp = jnp.where(blk > cur, -BIG, imp)
        imp = jnp.where((blk == 0) | (blk == cur), BIG, imp)

        age = t_q - (kstart + lax.broadcasted_iota(jnp.int32, (n_chunks * Q_BLOCK, 1), 0))
        s_w = s_w + all_heads(jnp.where((age >= 0) & (age < WINDOW), 0.0, -BIG))
        p_w = jnp.exp2(s_w - jnp.max(s_w, axis=0, keepdims=True)).astype(BF16)
        acc_w = jnp.dot(vwt_ref[c0], p_w[0:Q_BLOCK], preferred_element_type=F32)
        for c in range(1, n_chunks):
            acc_w = acc_w + jnp.dot(vwt_ref[c0 + c], p_w[c * Q_BLOCK:(c + 1) * Q_BLOCK],
                                    preferred_element_type=F32)
        o_w = acc_w * (1.0 / acc_w[HEAD_DIM:HEAD_DIM + 1, :])

        blk_f = blk.astype(F32)
        bias_t = jnp.full(imp.shape, UNSELECTED_BIAS, F32)
        for _ in range(n_sel):
            mx = jnp.max(imp, axis=0, keepdims=True)
            first = jnp.min(jnp.where(imp == mx, blk_f, float(MAX_SLC_BLOCKS)), axis=0, keepdims=True)
            hit = blk_f == first
            bias_t = jnp.where(hit, 0.0, bias_t)
            imp = jnp.where(hit, REMOVED, imp)
        return o_c, o_w, bias_t

    o_c, o_w, bias_t = front()

    q_aug = jnp.concatenate([jnp.concatenate([bias_t.astype(BF16)] * GROUP_HEADS, axis=1), qrt], axis=0)
    k_iota = lax.broadcasted_iota(jnp.int32, (key_tile, 1), 0)

    def scores(kt):
        return jnp.dot(kaug_ref[pl.ds(pl.multiple_of(kt * key_tile, key_tile), key_tile), :], q_aug,
                       preferred_element_type=F32)

    def tile_update(kt, s, m, acc):
        m_new = jnp.maximum(m, jnp.max(s, axis=0, keepdims=True))
        p = jnp.exp2(s - m_new)
        acc = jnp.exp2(m - m_new) * acc + jnp.dot(vst_ref[kt], p.astype(BF16), preferred_element_type=F32)
        return m_new, acc

    def causal(kt, s):
        return s + all_heads(jnp.where(kt * key_tile + k_iota <= t_q, 0.0, -BIG))

    n_tiles = kaug_ref.shape[0] // key_tile
    n_last = q0 // key_tile
    n_trips = n_last // 3
    in_range = lambda kt: jnp.minimum(kt, n_tiles - 1)
    sa_ref[...] = scores(0)
    sb_ref[...] = scores(in_range(1))

    def trip(j, carry):
        m, acc = carry
        sc_ref[...] = scores(3 * j + 2)
        m, acc = tile_update(3 * j, sa_ref[...], m, acc)
        sa_ref[...] = scores(in_range(3 * j + 3))
        m, acc = tile_update(3 * j + 1, sb_ref[...], m, acc)
        sb_ref[...] = scores(in_range(3 * j + 4))
        return tile_update(3 * j + 2, sc_ref[...], m, acc)

    m_s, acc_s = lax.fori_loop(0, n_trips, trip,
                               (jnp.full((1, cols), -BIG, F32), jnp.zeros((LANES, cols), F32)))
    t0 = 3 * n_trips
    sc_ref[...] = scores(in_range(t0 + 2))
    m_s, acc_s = tile_update(t0, causal(t0, sa_ref[...]), m_s, acc_s)
    m_s, acc_s = lax.cond(n_last > t0, lambda: tile_update(t0 + 1, causal(t0 + 1, sb_ref[...]), m_s, acc_s),
                          lambda: (m_s, acc_s))
    _, acc_s = lax.cond(n_last > t0 + 1, lambda: tile_update(t0 + 2, causal(t0 + 2, sc_ref[...]), m_s, acc_s),
                        lambda: (m_s, acc_s))
    o_s = acc_s * (1.0 / acc_s[HEAD_DIM:HEAD_DIM + 1, :])

    gate_t = jnp.transpose(_sigmoid(gate_ref[...]))
    low = lax.broadcasted_iota(jnp.int32, (Q_BLOCK, LANES), 1) < HEAD_DIM
    heads = []
    for r in range(GROUP_HEADS):
        sl = slice(r * Q_BLOCK, (r + 1) * Q_BLOCK)
        heads.append(jnp.transpose(
            gate_t[r:r + 1, :] * o_c[:, sl]
            + gate_t[GROUP_HEADS + r:GROUP_HEADS + r + 1, :] * o_s[:, sl]
            + gate_t[2 * GROUP_HEADS + r:2 * GROUP_HEADS + r + 1, :] * o_w[:, sl]))
    for pair in range(GROUP_HEADS // 2):
        o_ref[:, pair * LANES:(pair + 1) * LANES] = jnp.where(
            low, heads[2 * pair], pltpu.roll(heads[2 * pair + 1], HEAD_DIM, 1))


def _nsa_attn(qpt, qrt, kcmp, vcmpt, kaug, vst, kw, vwt, za, ovt, bsz, seq):
    n_cmp = kcmp.shape[2]
    n_q = seq // Q_BLOCK
    key_tile = vst.shape[-1]
    n_sel = min(SLC_TOPK, seq // SLC_LEN)
    kern = functools.partial(_nsa_attn_kernel, seq=seq, key_tile=key_tile, n_sel=n_sel)
    full = lambda *dims: pl.BlockSpec((None, None) + dims, lambda b, g, i: (b, g) + (0,) * len(dims))
    qspec = pl.BlockSpec((None, None, None, LANES, GROUP_HEADS * Q_BLOCK), lambda b, g, i: (b, g, i, 0, 0))
    return pl.pallas_call(
        kern,
        grid=(bsz, NSA_KV_GROUPS, n_q),
        in_specs=[qspec, qspec, full(n_cmp, LANES), full(LANES, n_cmp),
                  full(seq, 2 * LANES), full(seq // key_tile, LANES, key_tile),
                  full(seq, LANES), full(n_q, LANES, Q_BLOCK),
                  pl.BlockSpec((Q_BLOCK, LANES), lambda b, g, i: (b * n_q + i, COL_GATE // LANES + g)),
                  pl.BlockSpec((MAX_SLC_BLOCKS, n_cmp), lambda b, g, i: (0, 0))],
        out_specs=pl.BlockSpec((Q_BLOCK, GROUP_HEADS * HEAD_DIM), lambda b, g, i: (b * n_q + i, g)),
        out_shape=jax.ShapeDtypeStruct((bsz * seq, NSA_WIDTH), F32),
        scratch_shapes=[pltpu.VMEM((key_tile, GROUP_HEADS * Q_BLOCK), F32)] * 3,
        compiler_params=_params("parallel", "parallel", "arbitrary"),
    )(qpt, qrt, kcmp, vcmpt, kaug, vst, kw, vwt, za, ovt)


def _split3(x):
    hi = x.astype(BF16)
    r1 = x - hi.astype(F32)
    mid = r1.astype(BF16)
    return hi, mid, (r1 - mid.astype(F32)).astype(BF16)


def _rwkv_token_terms(z, prev_row, mu, w0, a0, k_k, k_a, r_k, ww2, wa2, wg2, seg):
    row = lax.broadcasted_iota(jnp.int32, z.shape, 0)
    prev = jnp.where(row == 0, prev_row, pltpu.roll(z, 1, 0))
    zs = z + (prev - z) * mu
    w = RWKV_WIDTH
    r, k, v, lo = zs[:, 0:w], zs[:, w:2 * w], zs[:, 2 * w:3 * w], zs[:, 3 * w:]
    wl = w0 + _dot(jnp.tanh(lo), ww2)
    w_log = -(jnp.maximum(-wl, 0.0) + jnp.log1p(jnp.exp(-jnp.abs(wl)))) - 0.5
    a = _sigmoid(a0 + _dot(lo, wa2))
    lw = -jnp.exp(w_log)
    g = _dot(_sigmoid(lo), wg2)
    k2 = k * (1.0 + (a - 1.0) * k_a)
    kkraw = k * k_k
    kk = kkraw * lax.rsqrt(jnp.maximum(_dot(kkraw * kkraw, seg), 1e-24))
    b = kk * a
    bonus = _dot(r * k2 * r_k, seg) * v
    step, c = z.shape[0], RWKV_CHUNK
    shift = int(math.log2(c))
    ri = lax.broadcasted_iota(jnp.int32, (step, step), 0)
    ci = lax.broadcasted_iota(jnp.int32, (step, step), 1)
    tri = jnp.where((ri >= ci) & (jnp.right_shift(ri, shift) == jnp.right_shift(ci, shift)), 1.0, 0.0).astype(BF16)
    cum = sum(jnp.dot(tri, piece, preferred_element_type=F32) for piece in _split3(lw))
    tot = jnp.concatenate([jnp.broadcast_to(cum[j * c + c - 1:(j + 1) * c, :], (c, w))
                           for j in range(step // c)], axis=0)
    e_neg = jnp.exp(-cum)
    e_rem = jnp.exp(tot - cum)
    scaled = (r * jnp.exp(cum), k2 * e_neg, b * e_neg, kk * jnp.exp(cum - lw), k2 * e_rem, b * e_rem,
              v, jnp.exp(tot))
    return scaled, bonus, g


RWKV_GROUP = LANES * 2 // RWKV_HEAD


def _group_tiles(x):
    c, gw = RWKV_CHUNK, RWKV_GROUP * RWKV_HEAD
    x3 = x.reshape(x.shape[0] // c, c, x.shape[1])
    return jnp.concatenate([x3[:, :, g * gw:(g + 1) * gw] for g in range(x.shape[1] // gw)], axis=0)


def _ungroup_tiles(t, n_chunks):
    c, gw = RWKV_CHUNK, t.shape[2]
    return jnp.concatenate([t[g * n_chunks:(g + 1) * n_chunks].reshape(n_chunks * c, gw)
                            for g in range(t.shape[0] // n_chunks)], axis=1)


def _block_diag(y):
    y = y.astype(BF16)
    t = jnp.concatenate([y] * (y.shape[2] // y.shape[1]), axis=1)
    shift = int(math.log2(y.shape[1]))
    same = (jnp.right_shift(lax.broadcasted_iota(jnp.int32, t.shape, 1), shift)
            == jnp.right_shift(lax.broadcasted_iota(jnp.int32, t.shape, 2), shift))
    return jnp.where(same, t, jnp.zeros_like(t))


def _head_diag(p):
    n = RWKV_HEAD
    head = jnp.right_shift(lax.broadcasted_iota(jnp.int32, (p.shape[0], n, p.shape[2]), 2), int(math.log2(n)))
    out = p[:, 0:n]
    for h in range(1, p.shape[1] // n):
        out = jnp.where(head == h, p[:, h * n:(h + 1) * n], out)
    return out


def _bdot(a, b, dims):
    return lax.dot_general(a.astype(BF16), b.astype(BF16), dims, preferred_element_type=F32)


_B_NN = (((2,), (1,)), ((0,), (0,)))
_B_NT = (((2,), (2,)), ((0,), (0,)))
_B_TN = (((1,), (1,)), ((0,), (0,)))


def _rwkv_chunk_terms(r_t, k_t, b_t, k_a, k_h, b_h, v, gamma):
    nb, c, gw = r_t.shape
    ri = lax.broadcasted_iota(jnp.int32, (nb, c, gw), 1)
    ci = lax.broadcasted_iota(jnp.int32, (nb, c, gw), 2) & (c - 1)
    incl, strict, eye = ri >= ci, ri > ci, ri == ci
    lhs = jnp.concatenate([r_t, k_a], axis=1).astype(BF16)
    to_k = _bdot(lhs, _block_diag(k_t), _B_NT)
    to_b = _bdot(lhs, _block_diag(b_t), _B_NT)
    a_rk = jnp.where(incl, to_k[:, 0:c], 0.0)
    a_kk = jnp.where(strict, to_k[:, c:], 0.0)
    a_rb = jnp.where(incl, to_b[:, 0:c], 0.0).astype(BF16)
    a_kb = jnp.where(strict, to_b[:, c:], 0.0)
    p = -a_kb
    t_inv = jnp.where(eye, 1.0, 0.0) + p
    p = _bdot(p, _block_diag(p), _B_NN)
    for _ in range(int(math.log2(c)) - 2):
        both = _bdot(jnp.concatenate([t_inv, p], axis=1), _block_diag(p), _B_NN)
        t_inv, p = t_inv + both[:, 0:c], both[:, c:]
    t_inv = t_inv + _bdot(t_inv, _block_diag(p), _B_NN)
    t_inv, v_bd = t_inv.astype(BF16), _block_diag(v)
    w_k = _bdot(t_inv, _block_diag(k_a), _B_NN).astype(BF16)
    u_v = _bdot(t_inv, _block_diag(_bdot(a_kk, v_bd, _B_NN)), _B_NN).astype(BF16)
    r_q = r_t - _bdot(a_rb, _block_diag(w_k), _B_NN)
    y_v = _bdot(a_rk, v_bd, _B_NN) - _bdot(a_rb, _block_diag(u_v), _B_NN)
    v = v.astype(BF16)
    rhs = jnp.concatenate([jnp.concatenate([v, jnp.zeros_like(v)], axis=2),
                           jnp.concatenate([-u_v, w_k], axis=2)], axis=1)
    both = _bdot(jnp.concatenate([k_h, b_h], axis=1), rhs, _B_TN)
    g = _head_diag(both[:, :, 0:gw])
    m = jnp.where(eye, gamma, 0.0) - _head_diag(both[:, :, gw:])
    return r_q, y_v, m, g


def _rwkv_kernel(zb_ref, mu_ref, w0_ref, a0_ref, kk_ref, ka_ref, rk_ref, ww2_ref, wa2_ref, wg2_ref, seg_ref,
                 lnw_ref, lnb_ref, o_ref, h_ref, prev_ref):
    @pl.when(pl.program_id(1) == 0)
    def _():
        h_ref[...] = jnp.zeros_like(h_ref)
        prev_ref[...] = jnp.zeros_like(prev_ref)

    z = zb_ref[...]
    step = z.shape[0]
    seg = seg_ref[...]
    scaled, bonus, gate = _rwkv_token_terms(
        z, prev_ref[...], mu_ref[...], w0_ref[...], a0_ref[...], kk_ref[...], ka_ref[...], rk_ref[...],
        ww2_ref[...], wa2_ref[...], wg2_ref[...], seg)
    prev_ref[...] = z[step - 1:step, :]
    c = RWKV_CHUNK
    n_chunks = step // c
    groups = h_ref.shape[0]
    terms = _rwkv_chunk_terms(*(_group_tiles(x) for x in scaled))
    r_q, y_v, m, g = (x.reshape(groups, n_chunks, c, x.shape[2]) for x in terms)
    h = h_ref[...]
    ys = []
    for j in range(n_chunks):
        h_bd = _block_diag(h)
        ys.append(_bdot(r_q[:, j], h_bd, _B_NN) + y_v[:, j])
        h = _bdot(m[:, j], h_bd, _B_NN) + g[:, j]
    h_ref[...] = h
    y = _ungroup_tiles(jnp.stack(ys, axis=1).reshape(groups * n_chunks, c, -1), n_chunks)
    inv_n = 1.0 / RWKV_HEAD
    mean = _dot(y, seg) * inv_n
    d = y - mean
    var = _dot(d * d, seg) * inv_n
    yn = d * lax.rsqrt(var + LNX_EPS) * lnw_ref[...] + lnb_ref[...]
    o_ref[...] = (yn + bonus) * gate


def _rwkv(zb, mu, w0, a0, k_k, k_a, r_k, ww2, wa2, wg2, seg, lnx_w, lnx_b, bsz, seq):
    step = min(RWKV_STEP, seq)
    per_seq = seq // step
    const = lambda *shape: pl.BlockSpec(shape, lambda b, i: (0,) * len(shape))
    row = lambda n: const(1, n)
    mat = const(RWKV_LORA_PAD, RWKV_WIDTH)
    return pl.pallas_call(
        _rwkv_kernel,
        grid=(bsz, per_seq),
        in_specs=[pl.BlockSpec((step, RWKV_COLS), lambda b, i: (b * per_seq + i, 0)),
                  row(RWKV_COLS)] + [row(RWKV_WIDTH)] * 5 + [mat, mat, mat, const(RWKV_WIDTH, RWKV_WIDTH),
                                                              row(RWKV_WIDTH), row(RWKV_WIDTH)],
        out_specs=pl.BlockSpec((step, RWKV_WIDTH), lambda b, i: (b * per_seq + i, 0)),
        out_shape=jax.ShapeDtypeStruct((bsz * seq, RWKV_WIDTH), F32),
        scratch_shapes=[pltpu.VMEM((RWKV_HEADS // RWKV_GROUP, RWKV_HEAD, RWKV_GROUP * RWKV_HEAD), F32),
                        pltpu.VMEM((1, RWKV_COLS), F32)],
        compiler_params=_params("parallel", "arbitrary"),
    )(zb, mu, w0, a0, k_k, k_a, r_k, ww2, wa2, wg2, seg, lnx_w, lnx_b)


def _merge_ffn_kernel(x_ref, oa_ref, ob_ref, zc_ref, wa_ref, wb_ref, wo_ref, g1_ref,
                      g2pre_ref, g2post_ref, wg_ref, wu_ref, wd_ref, o_ref, *, ff_chunk):
    gate_a = _sigmoid(zc_ref[:, 0:D_MODEL])
    gate_b = _sigmoid(zc_ref[:, D_MODEL:2 * D_MODEL])
    mixed = gate_a * _dot(oa_ref[...], wa_ref[...]) + gate_b * _dot(ob_ref[...], wb_ref[...])
    x = x_ref[...] + _rms_norm(_dot(mixed, wo_ref[...]), g1_ref[...])
    h = _rms_norm(x, g2pre_ref[...]).astype(BF16)
    acc = jnp.zeros(x.shape, F32)
    for c in range(0, D_FF, ff_chunk):
        gt = jnp.dot(h, wg_ref[:, c:c + ff_chunk], preferred_element_type=F32)
        up = jnp.dot(h, wu_ref[:, c:c + ff_chunk], preferred_element_type=F32)
        acc = acc + _dot(gt * _sigmoid(gt) * up, wd_ref[c:c + ff_chunk, :])
    o_ref[...] = x + _rms_norm(acc, g2post_ref[...])


def _merge_ffn(x2, oa, ob, zc, wa, wb, wo, g1, g2pre, g2post, wg, wu, wd, tm=DENSE_TILE, ff_chunk=256):
    t = x2.shape[0]
    tile = lambda n: pl.BlockSpec((tm, n), lambda i: (i, 0))
    const = _resident
    return pl.pallas_call(
        functools.partial(_merge_ffn_kernel, ff_chunk=ff_chunk),
        grid=(t // tm,),
        in_specs=[tile(D_MODEL), tile(NSA_WIDTH), tile(RWKV_WIDTH), tile(MERGE_COLS),
                  const(NSA_WIDTH, D_MODEL), const(RWKV_WIDTH, D_MODEL), const(D_MODEL, D_MODEL),
                  const(1, D_MODEL), const(1, D_MODEL), const(1, D_MODEL),
                  const(D_MODEL, D_FF), const(D_MODEL, D_FF), const(D_FF, D_MODEL)],
        out_specs=tile(D_MODEL),
        out_shape=jax.ShapeDtypeStruct((t, D_MODEL), F32),
        compiler_params=_params("parallel"),
    )(x2, oa, ob, zc, wa, wb, wo, g1, g2pre, g2post, wg, wu, wd)


def _pad_cols(a, n):
    return jnp.pad(a, ((0, 0), (0, n - a.shape[1])))


def _pack_w_in(w_in):
    g0 = NSA_WIDTH + 6 * KV_WIDTH
    r0 = g0 + 3 * NSA_HEADS
    l0 = r0 + 3 * RWKV_WIDTH
    m0 = l0 + RWKV_LORA
    gates = w_in[:, g0:r0].reshape(-1, 3, NSA_KV_GROUPS, GROUP_HEADS).transpose(0, 2, 1, 3)
    gates = jnp.pad(gates.reshape(-1, NSA_KV_GROUPS, 3 * GROUP_HEADS),
                    ((0, 0), (0, 0), (0, LANES - 3 * GROUP_HEADS))).reshape(-1, NSA_KV_GROUPS * LANES)
    return jnp.concatenate([w_in[:, :g0], gates,
                            w_in[:, r0:l0], _pad_cols(w_in[:, l0:m0], RWKV_LORA_PAD),
                            w_in[:, m0:]], axis=1).astype(BF16)


def _rope_tables(seq):
    inv = 1.0 / (ROPE_THETA ** (jnp.arange(0, HEAD_DIM, 2, dtype=F32) / HEAD_DIM))
    ang = jnp.arange(seq, dtype=F32)[:, None] * inv[None, :]
    cos, sin = jnp.cos(ang), jnp.sin(ang)
    reps = LANES // HEAD_DIM
    return (jnp.concatenate([cos, cos] * reps, axis=1), jnp.concatenate([-sin, sin] * reps, axis=1))


def _layer(x, norm1_pre, norm1_post, w_in,
           cmp_pe_k, cmp_w1_k, cmp_b1_k, cmp_w2_k, cmp_pe_v, cmp_w1_v, cmp_b1_v, cmp_w2_v,
           mu_r, mu_k, mu_v, mu_w, mu_a, mu_g, w0, w_w2, a0, w_a2, w_g2,
           k_k, k_a, r_k, lnx_w, lnx_b, w_branch_a, w_branch_b, w_out,
           norm2_pre, norm2_post, w_gate, w_up, w_down):
    bsz, seq, _ = x.shape
    assert seq % RWKV_STEP == 0 or seq < RWKV_STEP
    assert seq // SLC_LEN <= MAX_SLC_BLOCKS and seq >= WINDOW + Q_BLOCK
    t = bsz * seq
    x2 = x.reshape(t, D_MODEL)
    row = lambda a: a.reshape(1, -1)


    cos_t, sin_t = _rope_tables(seq)
    za, zb, zc, qpt, qrt, kaug, kwin, vselt, vwint = _in_proj(
        x2, row(norm1_pre), _pack_w_in(w_in), cos_t, sin_t, bsz, seq)
    g_, hd = NSA_KV_GROUPS, HEAD_DIM

    def both_groups(w, cols):
        z = jnp.zeros(w.shape[:-2] + (g_ * w.shape[-2], g_ * cols), w.dtype)
        for g in range(g_):
            z = z.at[..., g * w.shape[-2]:(g + 1) * w.shape[-2], g * cols:g * cols + w.shape[-1]].set(w)
        return z

    w1 = jnp.stack([cmp_w1_k, cmp_w1_v]).reshape(2, CMP_LEN, hd, CMP_HIDDEN)
    cmp_out, cmp_out_t = _compress(za,
                        jnp.tile(jnp.stack([cmp_pe_k, cmp_pe_v]), (1, 1, g_)),
                        both_groups(w1, CMP_HIDDEN).astype(BF16),
                        jnp.tile(jnp.stack([cmp_b1_k, cmp_b1_v]).reshape(2, 1, CMP_HIDDEN), (1, 1, g_)),
                        both_groups(jnp.stack([cmp_w2_k, cmp_w2_v]), LANES).astype(BF16),
                        bsz, seq)

    n_half = seq // CMP_STRIDE
    n_slc = seq // SLC_LEN
    cmp_start = jnp.arange(n_half) * CMP_STRIDE
    slc_start = jnp.arange(MAX_SLC_BLOCKS) * SLC_LEN
    ovt = ((cmp_start[None, :] < slc_start[:, None] + SLC_LEN)
           & (cmp_start[None, :] + CMP_LEN - 1 >= slc_start[:, None])
           & (jnp.arange(MAX_SLC_BLOCKS)[:, None] < n_slc)).astype(BF16)
    o_a = _nsa_attn(qpt, qrt, cmp_out[0], cmp_out_t[1], kaug, vselt, kwin, vwint, za, ovt, bsz, seq)

    mu = _pad_cols(jnp.concatenate([mu_r, mu_k, mu_v, mu_w, mu_a, mu_g]).reshape(1, -1), RWKV_COLS)
    lora = jnp.zeros((3, RWKV_LORA_PAD, RWKV_WIDTH), F32)
    lora = lora.at[0, 0:DECAY_LORA].set(w_w2)
    lora = lora.at[1, DECAY_LORA:DECAY_LORA + AAA_LORA].set(w_a2)
    lora = lora.at[2, DECAY_LORA + AAA_LORA:RWKV_LORA].set(w_g2).astype(BF16)
    lanes = jnp.arange(RWKV_WIDTH) // RWKV_HEAD
    seg = (lanes[:, None] == lanes[None, :]).astype(BF16)
    o_b = _rwkv(zb, mu, row(w0), row(a0), row(k_k), row(k_a), row(r_k), lora[0], lora[1], lora[2], seg,
                row(lnx_w), row(lnx_b), bsz, seq)

    out = _merge_ffn(x2, o_a, o_b, zc, w_branch_a.astype(BF16), w_branch_b.astype(BF16), w_out.astype(BF16),
                     row(norm1_post), row(norm2_pre), row(norm2_post),
                     w_gate.astype(BF16), w_up.astype(BF16), w_down.astype(BF16))
    return out.reshape(bsz, seq, D_MODEL)


def kernel(x, norm1_pre, norm1_post, w_in, cmp_pe_k, cmp_w1_k, cmp_b1_k, cmp_w2_k, cmp_pe_v, cmp_w1_v, cmp_b1_v, cmp_w2_v, mu_r, mu_k, mu_v, mu_w, mu_a, mu_g, w0, w_w2, a0, w_a2, w_g2, k_k, k_a, r_k, lnx_w, lnx_b, w_branch_a, w_branch_b, w_out, norm2_pre, norm2_post, w_gate, w_up, w_down):
    params = (norm1_pre, norm1_post, w_in, cmp_pe_k, cmp_w1_k, cmp_b1_k, cmp_w2_k, cmp_pe_v, cmp_w1_v,
              cmp_b1_v, cmp_w2_v, mu_r, mu_k, mu_v, mu_w, mu_a, mu_g, w0, w_w2, a0, w_a2, w_g2,
              k_k, k_a, r_k, lnx_w, lnx_b, w_branch_a, w_branch_b, w_out,
              norm2_pre, norm2_post, w_gate, w_up, w_down)
    for layer in range(norm1_pre.shape[0]):
        x = _layer(x, *[p[layer] for p in params])
    return x
```
